```python
import jax, jax.numpy as jnp
from jax import lax
import numpy as np

D_MODEL = 2048
BATCH = 2
SEQ = 8192
DEPTH = 1

CHUNK = 64
EPS = 1e-6
M_HEADS = 4
M_HEAD_DIM = 256
M_WIDTH = M_HEADS * M_HEAD_DIM
CONV_WIDTH = 4
A_HEADS = 16
A_HEAD_DIM = 64
A_WIDTH = A_HEADS * A_HEAD_DIM
LEFT_CHUNKS = 8
BAND_CHUNKS = LEFT_CHUNKS + 1
MAX_REL = 128
N_MEM = 256
X_HEADS = 4
X_HEAD_DIM = 128
X_WIDTH = X_HEADS * X_HEAD_DIM
N_EXPERTS = 32
TOP_K = 4
D_FF = D_MODEL
SWIGLU_LIMIT = 7.0
SWIGLU_ALPHA = 1.702
MOE_BLOCK = 256
IN_COLS = 4 * M_WIDTH + 2 * M_HEADS + 3 * A_WIDTH + 2 * D_MODEL

kernel_name = "hybrid_mlstm_chunkattn_memxattn_moe"


def _in_proj_offsets():
    sizes = (M_WIDTH, M_WIDTH, M_WIDTH, M_WIDTH, 2 * M_HEADS, A_WIDTH, A_WIDTH, A_WIDTH, D_MODEL)
    offs, t = [], 0
    for s in sizes:
        t += s
        offs.append(t)
    return offs


def rmsnorm(x, w):
    xf = x.astype(jnp.float32)
    y = xf * lax.rsqrt(jnp.mean(xf * xf, axis=-1, keepdims=True) + EPS)
    return (y * w.astype(jnp.float32)).astype(x.dtype)


def causal_dwconv(u, w, b):
    S = u.shape[1]
    up = jnp.pad(u, ((0, 0), (CONV_WIDTH - 1, 0), (0, 0)))
    out = up[:, 0:S] * w[0]
    for j in range(1, CONV_WIDTH):
        out = out + up[:, j:j + S] * w[j]
    return out + b


def mlstm_chunkwise(q, k, v, i_pre, f_pre):
    B, H, S, d = q.shape
    NC = S // CHUNK
    f32 = jnp.float32
    q = q.astype(f32).reshape(B, H, NC, CHUNK, d)
    k = (k.astype(f32) * (d ** -0.5)).reshape(B, H, NC, CHUNK, d)
    v = v.astype(f32).reshape(B, H, NC, CHUNK, d)
    ig = i_pre.reshape(B, H, NC, CHUNK)
    b = jnp.cumsum(jax.nn.log_sigmoid(f_pre).reshape(B, H, NC, CHUNK), axis=-1)
    g = b[..., -1]
    w_end = g[..., None] - b + ig
    a = jnp.max(w_end, axis=-1)
    e_end = jnp.exp(w_end - a[..., None])
    C_loc = jnp.einsum('bhclv,bhclk->bhcvk', v * e_end[..., None], k)
    n_loc = jnp.einsum('bhcl,bhclk->bhck', e_end, k)

    def step(carry, inp):
        C, n, m = carry
        Cl, nl, al, gl = inp
        m_new = jnp.maximum(gl + m, al)
        s_prev = jnp.exp(gl + m - m_new)
        s_loc = jnp.exp(al - m_new)
        C_new = s_prev[..., None, None] * C + s_loc[..., None, None] * Cl
        n_new = s_prev[..., None] * n + s_loc[..., None] * nl
        return (C_new, n_new, m_new), (C, n, m)

    init = (jnp.zeros((B, H, d, d), f32), jnp.zeros((B, H, d), f32), jnp.zeros((B, H), f32))
    xs = (jnp.moveaxis(C_loc, 2, 0), jnp.moveaxis(n_loc, 2, 0),
          jnp.moveaxis(a, 2, 0), jnp.moveaxis(g, 2, 0))
    _, (C_in, n_in, m_in) = lax.scan(step, init, xs)
    C_in = jnp.moveaxis(C_in, 0, 2)
    n_in = jnp.moveaxis(n_in, 0, 2)
    m_in = jnp.moveaxis(m_in, 0, 2)

    causal = jnp.tril(jnp.ones((CHUNK, CHUNK), dtype=bool))
    Dlog = b[..., :, None] - b[..., None, :] + ig[..., None, :]
    Dlog = jnp.where(causal, Dlog, -jnp.inf)
    inter_log = b + m_in[..., None]
    m_j = jnp.maximum(inter_log, jnp.max(Dlog, axis=-1))
    s = jnp.einsum('bhcjd,bhcsd->bhcjs', q, k) * jnp.exp(Dlog - m_j[..., None])
    w_inter = jnp.exp(inter_log - m_j)
    num = (jnp.einsum('bhcjs,bhcsv->bhcjv', s, v)
           + w_inter[..., None] * jnp.einsum('bhcvk,bhcjk->bhcjv', C_in, q))
    den = jnp.sum(s, axis=-1) + w_inter * jnp.einsum('bhck,bhcjk->bhcj', n_in, q)
    h = num / jnp.maximum(jnp.abs(den), jnp.exp(-m_j))[..., None]
    return h.reshape(B, H, S, d)


def chunked_band_attention(q, k, v, rel_table):
    B, H, S, dh = q.shape
    NC = S // CHUNK
    qc = q.reshape(B, H, NC, CHUNK, dh)
    pad = ((0, 0), (0, 0), (LEFT_CHUNKS, 0), (0, 0), (0, 0))
    kp = jnp.pad(k.reshape(B, H, NC, CHUNK, dh), pad)
    vp = jnp.pad(v.reshape(B, H, NC, CHUNK, dh), pad)
    band_idx = jnp.arange(NC)[:, None] + jnp.arange(BAND_CHUNKS)[None, :]
    k_band = jnp.take(kp, band_idx, axis=2).reshape(B, H, NC, BAND_CHUNKS * CHUNK, dh)
    v_band = jnp.take(vp, band_idx, axis=2).reshape(B, H, NC, BAND_CHUNKS * CHUNK, dh)
    valid = jnp.repeat(band_idx >= LEFT_CHUNKS, CHUNK, axis=1)
    q_pos = LEFT_CHUNKS * CHUNK + jnp.arange(CHUNK)
    k_pos = jnp.arange(BAND_CHUNKS * CHUNK)
    rel = jnp.clip(q_pos[:, None] - k_pos[None, :], -MAX_REL, MAX_REL) + MAX_REL
    bias = rel_table[:, rel].astype(jnp.float32)
    scores = jnp.einsum('bhcqd,bhckd->bhcqk', qc, k_band).astype(jnp.float32) * (dh ** -0.5)
    scores = scores + bias[None, :, None]
    scores = jnp.where(valid[None, None, :, None, :], scores, -1e30)
    p = jax.nn.softmax(scores, axis=-1).astype(v.dtype)
    o = jnp.einsum('bhcqk,bhckd->bhcqd', p, v_band)
    return o.reshape(B, H, S, dh).transpose(0, 2, 1, 3).reshape(B, S, H * dh)


def memory_cross_attention(h, mem_n, wq, wkv, wo):
    B, S, _ = h.shape
    q = (h @ wq).reshape(B, S, X_HEADS, X_HEAD_DIM)
    kv = mem_n @ wkv
    km, vm = jnp.split(kv, 2, axis=-1)
    km = km.reshape(B, -1, X_HEADS, X_HEAD_DIM)
    vm = vm.reshape(B, -1, X_HEADS, X_HEAD_DIM)
    scores = jnp.einsum('bshd,bnhd->bhsn', q, km).astype(jnp.float32) * (X_HEAD_DIM ** -0.5)
    p = jax.nn.softmax(scores, axis=-1).astype(vm.dtype)
    o = jnp.einsum('bhsn,bnhd->bshd', p, vm).reshape(B, S, X_WIDTH)
    return o @ wo


def moe_ffn(h, w_router, b_router, w_gu, b_gu, w_down, b_down):
    B, S, D = h.shape
    T = B * S
    xt = h.reshape(T, D)
    logits = (xt @ w_router + b_router).astype(jnp.float32)
    top_logit, top_e = lax.top_k(logits, TOP_K)
    gate = jax.nn.softmax(top_logit, axis=-1)
    A = T * TOP_K
    flat_e = top_e.reshape(A).astype(jnp.int32)
    flat_tok = jnp.arange(A, dtype=jnp.int32) // TOP_K
    flat_gate = gate.reshape(A)
    order = jnp.argsort(flat_e)
    se = flat_e[order]
    counts = jnp.bincount(flat_e, length=N_EXPERTS)
    padded = ((counts + MOE_BLOCK - 1) // MOE_BLOCK) * MOE_BLOCK
    start = jnp.cumsum(counts) - counts
    pend = jnp.cumsum(padded)
    pstart = pend - padded
    dest = pstart[se] + (jnp.arange(A, dtype=jnp.int32) - start[se])
    n_blocks = -(-A // MOE_BLOCK) + N_EXPERTS
    n_rows = n_blocks * MOE_BLOCK
    row_tok = jnp.full((n_rows,), T, dtype=jnp.int32).at[dest].set(flat_tok[order])
    row_gate = jnp.zeros((n_rows,), jnp.float32).at[dest].set(flat_gate[order])
    block_e = jnp.searchsorted(pend, jnp.arange(n_blocks) * MOE_BLOCK, side='right')
    block_e = jnp.minimum(block_e, N_EXPERTS - 1).astype(jnp.int32)
    x_pad = jnp.concatenate([xt, jnp.zeros((1, D), xt.dtype)], axis=0)

    def expert_block(args):
        tok, e = args
        xb = x_pad[tok]
        gu = xb @ w_gu[e] + b_gu[e]
        g, u = jnp.split(gu, 2, axis=-1)
        g = jnp.minimum(g, SWIGLU_LIMIT)
        u = jnp.clip(u, -SWIGLU_LIMIT, SWIGLU_LIMIT)
        act = (u + 1) * (g * jax.nn.sigmoid(SWIGLU_ALPHA * g))
        return act @ w_down[e] + b_down[e]

    y_rows = lax.map(expert_block, (row_tok.reshape(n_blocks, MOE_BLOCK), block_e))
    y_rows = y_rows.reshape(n_rows, D) * row_gate[:, None].astype(h.dtype)
    y = jnp.zeros((T + 1, D), h.dtype).at[row_tok].add(y_rows)[:T]
    return y.reshape(B, S, D)


def setup_inputs(seed: int = 0) -> dict:
    key = jax.random.key(seed)
    ks = jax.random.split(key, 32)
    L, D = DEPTH, D_MODEL
    nrm = jax.random.normal
    f32 = jnp.float32
    i_b = 0.1 * nrm(ks[5], (L, M_HEADS), f32)
    f_b = jnp.linspace(3.0, 6.0, M_HEADS, dtype=f32)[None, :] + 0.1 * nrm(ks[6], (L, M_HEADS), f32)
    return {
        "x": nrm(ks[0], (BATCH, SEQ, D), f32),
        "mem": nrm(ks[1], (BATCH, N_MEM, D), f32),
        "norm_mix_w": 1.0 + 0.05 * nrm(ks[2], (L, D), f32),
        "w_in": nrm(ks[3], (L, D, IN_COLS), f32) * D ** -0.5,
        "conv_w": nrm(ks[4], (L, CONV_WIDTH, 2 * M_WIDTH), f32) * CONV_WIDTH ** -0.5,
        "conv_b": 0.02 * nrm(ks[7], (L, 2 * M_WIDTH), f32),
        "if_bias": jnp.concatenate([i_b, f_b], axis=-1),
        "m_head_norm_w": 1.0 + 0.05 * nrm(ks[8], (L, M_HEADS, M_HEAD_DIM), f32),
        "w_branch_m": nrm(ks[9], (L, M_WIDTH, D), f32) * M_WIDTH ** -0.5,
        "w_branch_a": nrm(ks[10], (L, A_WIDTH, D), f32) * A_WIDTH ** -0.5,
        "rel_bias": 0.1 * nrm(ks[11], (L, A_HEADS, 2 * MAX_REL + 1), f32),
        "w_out": nrm(ks[12], (L, D, D), f32) * D ** -0.5,
        "norm_x_w": 1.0 + 0.05 * nrm(ks[13], (L, D), f32),
        "norm_mem_w": 1.0 + 0.05 * nrm(ks[14], (L, D), f32),
        "wq_x": nrm(ks[15], (L, D, X_WIDTH), f32) * D ** -0.5,
        "wkv_x": nrm(ks[16], (L, D, 2 * X_WIDTH), f32) * D ** -0.5,
        "wo_x": nrm(ks[17], (L, X_WIDTH, D), f32) * X_WIDTH ** -0.5,
        "norm_ffn_w": 1.0 + 0.05 * nrm(ks[18], (L, D), f32),
        "w_router": nrm(ks[19], (L, D, N_EXPERTS), f32) * D ** -0.5,
        "b_router": 0.01 * nrm(ks[20], (L, N_EXPERTS), f32),
        "w_gu": nrm(ks[21], (L, N_EXPERTS, D, 2 * D_FF), f32) * D ** -0.5,
        "b_gu": 0.01 * nrm(ks[22], (L, N_EXPERTS, 2 * D_FF), f32),
        "w_down": nrm(ks[23], (L, N_EXPERTS, D_FF, D), f32) * D_FF ** -0.5,
        "b_down": 0.01 * nrm(ks[24], (L, N_EXPERTS, D), f32),
        "final_norm_w": 1.0 + 0.05 * nrm(ks[25], (D,), f32),
    }


def reference(x, mem, norm_mix_w, w_in, conv_w, conv_b, if_bias, m_head_norm_w, w_branch_m,
              w_branch_a, rel_bias, w_out, norm_x_w, norm_mem_w, wq_x, wkv_x, wo_x, norm_ffn_w,
              w_router, b_router, w_gu, b_gu, w_down, b_down, final_norm_w):
    B, S, D = x.shape
    offs = _in_proj_offsets()
    for l in range(DEPTH):
        h = rmsnorm(x, norm_mix_w[l])
        proj = h @ w_in[l]
        q_m, k_m, v_m, o_m, if_m, q_a, k_a, v_a, g_m, g_a = jnp.split(proj, offs, axis=-1)

        qk = jax.nn.silu(causal_dwconv(jnp.concatenate([q_m, k_m], axis=-1), conv_w[l], conv_b[l]))
        qm, km = jnp.split(qk, 2, axis=-1)
        to_heads_m = lambda t: t.reshape(B, S, M_HEADS, M_HEAD_DIM).transpose(0, 2, 1, 3)
        gates = (if_m + if_bias[l]).astype(jnp.float32)
        i_pre = gates[..., :M_HEADS].transpose(0, 2, 1)
        f_pre = gates[..., M_HEADS:].transpose(0, 2, 1)
        hm = mlstm_chunkwise(to_heads_m(qm), to_heads_m(km), to_heads_m(v_m), i_pre, f_pre)
        hm = hm.transpose(0, 2, 1, 3)
        hm = hm * lax.rsqrt(jnp.mean(hm * hm, axis=-1, keepdims=True) + EPS)
        hm = (hm * m_head_norm_w[l].astype(jnp.float32)).reshape(B, S, M_WIDTH).astype(x.dtype)
        y_m = (jax.nn.sigmoid(o_m) * hm) @ w_branch_m[l]

        to_heads_a = lambda t: t.reshape(B, S, A_HEADS, A_HEAD_DIM).transpose(0, 2, 1, 3)
        ya = chunked_band_attention(to_heads_a(q_a), to_heads_a(k_a), to_heads_a(v_a), rel_bias[l])
        y_a = ya @ w_branch_a[l]

        merged = jax.nn.sigmoid(g_m) * y_m + jax.nn.sigmoid(g_a) * y_a
        x = x + merged @ w_out[l]

        x = x + memory_cross_attention(rmsnorm(x, norm_x_w[l]), rmsnorm(mem, norm_mem_w[l]),
                                       wq_x[l], wkv_x[l], wo_x[l])

        x = x + moe_ffn(rmsnorm(x, norm_ffn_w[l]), w_router[l], b_router[l], w_gu[l], b_gu[l],
                        w_down[l], b_down[l])
    return rmsnorm(x, final_norm_w)
```

```python
import functools

import jax
import jax.numpy as jnp
from jax import lax
from jax.experimental import pallas as pl
from jax.experimental.pallas import tpu as pltpu

F32 = jnp.float32
BF16 = jnp.bfloat16
I32 = jnp.int32

D_MODEL = 2048
CHUNK = 64
EPS = 1e-6
M_HEADS = 4
M_HEAD_DIM = 256
M_WIDTH = M_HEADS * M_HEAD_DIM
CONV_WIDTH = 4
A_HEADS = 16
A_HEAD_DIM = 64
A_WIDTH = A_HEADS * A_HEAD_DIM
LEFT_CHUNKS = 8
MAX_REL = 128
X_HEADS = 4
X_HEAD_DIM = 128
X_WIDTH = X_HEADS * X_HEAD_DIM
N_EXPERTS = 32
TOP_K = 4
D_FF = D_MODEL
SWIGLU_LIMIT = 7.0
SWIGLU_ALPHA = 1.702

LANES_V7X = 128
SUBLANES_V7X = 8
VMEM_BYTES_V7X = 64 * 1024 * 1024
VMEM_LIMIT_CAP = 60000 * 1024

NEG_BIG = -1e30

COL_GM = 0
COL_GA = 2048
COL_QM = 4096
COL_KM = 5120
COL_VM = 6144
COL_OM = 7168
COL_QA = 8192
COL_KA = 9216
COL_VA = 10240
N_MAIN = 11264

IN_TM = 1024
IN_TN = 1024
ATT_TQ = 512
ATT_SUB = 128
ATT_WIN = ATT_SUB + LEFT_CHUNKS * CHUNK
MERGE_TM = 512
XATT_TM = 512
MOE_TM = 512
MOE_TF = 512
DISP_TT = 256
COMB_TT = 128


def _vmem_limit(nbytes):
    return int(min(VMEM_LIMIT_CAP, max(nbytes, 16 * 1024 * 1024)))


def _nt_dot(a, b):
    return lax.dot_general(a, b, (((1,), (1,)), ((), ())), preferred_element_type=F32)


def _tn_dot(a, b):
    return lax.dot_general(a, b, (((0,), (0,)), ((), ())), preferred_element_type=F32)


def _sigmoid(x):
    return 1.0 / (1.0 + jnp.exp(-x))


def _log_sigmoid(x):
    return jnp.minimum(x, 0.0) - jnp.log1p(jnp.exp(-jnp.abs(x)))


def _rms(x, w):
    return x * lax.rsqrt(jnp.mean(x * x, axis=-1, keepdims=True) + EPS) * w


def _inproj_kernel(x_ref, nw_ref, w_ref, wif_ref, o_ref, if_ref, h_scr):
    @pl.when(pl.program_id(1) == 0)
    def _():
        hb = _rms(x_ref[...], nw_ref[...]).astype(BF16)
        h_scr[...] = hb
        if_ref[...] = jnp.dot(hb, wif_ref[...], preferred_element_type=F32)

    o_ref[...] = jnp.dot(h_scr[...], w_ref[...], preferred_element_type=F32).astype(o_ref.dtype)


def _inproj(x2d, norm_w, w_main, w_if):
    T, D = x2d.shape
    tm = min(IN_TM, T)
    grid = (T // tm, N_MAIN // IN_TN)
    vmem = 2 * tm * D * 4 + tm * D * 2 + 2 * D * IN_TN * 2 + 2 * tm * IN_TN * 2 + 4 * tm * LANES_V7X * 4
    return pl.pallas_call(
        _inproj_kernel,
        out_shape=(jax.ShapeDtypeStruct((T, N_MAIN), BF16), jax.ShapeDtypeStruct((T, LANES_V7X), F32)),
        grid=grid,
        in_specs=[
            pl.BlockSpec((tm, D), lambda i, j: (i, 0)),
            pl.BlockSpec((1, D), lambda i, j: (0, 0)),
            pl.BlockSpec((D, IN_TN), lambda i, j: (0, j)),
            pl.BlockSpec((D, LANES_V7X), lambda i, j: (0, 0)),
        ],
        out_specs=(
            pl.BlockSpec((tm, IN_TN), lambda i, j: (i, j)),
            pl.BlockSpec((tm, LANES_V7X), lambda i, j: (i, 0)),
        ),
        scratch_shapes=[pltpu.VMEM((tm, D), BF16)],
        compiler_params=pltpu.CompilerParams(
            dimension_semantics=("parallel", "arbitrary"),
            vmem_limit_bytes=_vmem_limit(vmem + (8 << 20))),
        name="inproj",
    )(x2d, norm_w, w_main, w_if)


def _mlstm_kernel(q_ref, k_ref, v_ref, o_ref, gc_ref, gr_ref, cw_ref, cb_ref, bc_ref, br_ref, nw_ref,
                  out_ref, ubuf, c_st, n_st, m_st):
    nb = q_ref.shape[0]
    L = CHUNK
    dh = M_HEAD_DIM
    halo = SUBLANES_V7X

    @pl.when(pl.program_id(0) == 0)
    def _():
        ubuf[:, 0:halo, :] = jnp.zeros((nb, halo, 2 * M_WIDTH), F32)
        c_st[...] = jnp.zeros_like(c_st)
        n_st[...] = jnp.zeros_like(n_st)
        m_st[...] = jnp.zeros_like(m_st)

    row = lax.broadcasted_iota(I32, (L, L), 0)
    col = lax.broadcasted_iota(I32, (L, L), 1)
    lower = col <= row

    for b in range(nb):
        ubuf[b, halo:halo + L, 0:M_WIDTH] = q_ref[b].astype(F32)
        ubuf[b, halo:halo + L, M_WIDTH:2 * M_WIDTH] = k_ref[b].astype(F32)
        conv = cb_ref[...]
        for j in range(CONV_WIDTH):
            off = halo - (CONV_WIDTH - 1) + j
            conv = conv + ubuf[b, off:off + L, :] * cw_ref[j:j + 1, :]
        qk = conv * _sigmoid(conv)
        ubuf[b, 0:halo, :] = ubuf[b, L:L + halo, :]

        gcol = gc_ref[b] + bc_ref[...]
        grow = gr_ref[b, 0] + br_ref[...]

        for h in range(M_HEADS):
            s = b * M_HEADS + h
            lo, hi = h * dh, (h + 1) * dh
            q = qk[:, lo:hi]
            k = qk[:, M_WIDTH + lo:M_WIDTH + hi] * (dh ** -0.5)
            vb = v_ref[b, :, lo:hi]
            qb = q.astype(BF16)
            kb = k.astype(BF16)

            ig_c = gcol[:, h:h + 1]
            ig_r = grow[h:h + 1, :]
            ls_c = _log_sigmoid(gcol[:, M_HEADS + h:M_HEADS + h + 1])
            ls_r = _log_sigmoid(grow[M_HEADS + h:M_HEADS + h + 1, :])
            b_c = jnp.sum(jnp.where(lower, ls_r, 0.0), axis=1, keepdims=True)
            b_r = jnp.sum(jnp.where(row <= col, ls_c, 0.0), axis=0, keepdims=True)
            g = jnp.sum(ls_r, axis=1, keepdims=True)
            m_old = m_st[s][:, 0:1]

            w_end = g - b_c + ig_c
            a = jnp.max(w_end, axis=0, keepdims=True)
            e_end = jnp.exp(w_end - a)

            dlog = jnp.where(lower, b_c - b_r + ig_r, NEG_BIG)
            inter = b_c + m_old
            m_j = jnp.maximum(inter, jnp.max(dlog, axis=1, keepdims=True))
            sm = _nt_dot(qb, kb) * jnp.exp(dlog - m_j)
            w_int = jnp.exp(inter - m_j)

            c_old = c_st[s]
            n_old = n_st[s]
            num = (jnp.dot(sm.astype(BF16), vb, preferred_element_type=F32)
                   + w_int * _nt_dot(qb, c_old.astype(BF16)))
            den = (jnp.sum(sm, axis=1, keepdims=True)
                   + w_int * jnp.sum(q * n_old, axis=1, keepdims=True))
            hh = num / jnp.maximum(jnp.abs(den), jnp.exp(-m_j))

            hn = _rms(hh, nw_ref[:, lo:hi])
            og = _sigmoid(o_ref[b, :, lo:hi].astype(F32))
            out_ref[b, :, lo:hi] = (og * hn).astype(out_ref.dtype)

            m_new = jnp.maximum(g + m_old, a)
            s_prev = jnp.exp(g + m_old - m_new)
            s_loc = jnp.exp(a - m_new)
            ve = (vb.astype(F32) * e_end).astype(BF16)
            c_st[s] = s_prev * c_old + s_loc * _tn_dot(ve, kb)
            n_st[s] = s_prev * n_old + s_loc * jnp.sum(e_end * k, axis=0, keepdims=True)
            m_st[s] = jnp.broadcast_to(m_new, (1, LANES_V7X))


def _mlstm(p3, gcol3, grow4, conv_w, conv_b, bias_col, bias_row, head_w):
    B, S, _ = p3.shape
    nc = S // CHUNK
    ns = B * M_HEADS
    blk = lambda cb: pl.BlockSpec((B, CHUNK, M_WIDTH), lambda c, cb=cb: (0, c, cb))
    full = lambda shape: pl.BlockSpec(shape, lambda c: (0,) * len(shape))
    vmem = (2 * 5 * B * CHUNK * M_WIDTH * 2 + B * (CHUNK + 8) * 2 * M_WIDTH * 4
            + ns * M_HEAD_DIM * M_HEAD_DIM * 4 + (16 << 20))
    return pl.pallas_call(
        _mlstm_kernel,
        out_shape=jax.ShapeDtypeStruct((B, S, M_WIDTH), BF16),
        grid=(nc,),
        in_specs=[
            blk(COL_QM // M_WIDTH), blk(COL_KM // M_WIDTH), blk(COL_VM // M_WIDTH), blk(COL_OM // M_WIDTH),
            pl.BlockSpec((B, CHUNK, LANES_V7X), lambda c: (0, c, 0)),
            pl.BlockSpec((B, 1, 2 * M_HEADS, CHUNK), lambda c: (0, c, 0, 0)),
            full((CONV_WIDTH, 2 * M_WIDTH)), full((1, 2 * M_WIDTH)),
            full((1, LANES_V7X)), full((2 * M_HEADS, CHUNK)), full((1, M_WIDTH)),
        ],
        out_specs=pl.BlockSpec((B, CHUNK, M_WIDTH), lambda c: (0, c, 0)),
        scratch_shapes=[
            pltpu.VMEM((B, CHUNK + SUBLANES_V7X, 2 * M_WIDTH), F32),
            pltpu.VMEM((ns, M_HEAD_DIM, M_HEAD_DIM), F32),
            pltpu.VMEM((ns, 1, M_HEAD_DIM), F32),
            pltpu.VMEM((ns, 1, LANES_V7X), F32),
        ],
        compiler_params=pltpu.CompilerParams(
            dimension_semantics=("arbitrary",), vmem_limit_bytes=_vmem_limit(vmem)),
        name="mlstm",
    )(p3, p3, p3, p3, gcol3, grow4, conv_w, conv_b, bias_col, bias_row, head_w)


def _band_attn_kernel(q_ref, kp_ref, kc_ref, vp_ref, vc_ref, bias_ref, out_ref):
    qb_idx = pl.program_id(2)
    tq = q_ref.shape[1]
    q2 = q_ref[0]
    kcat = jnp.concatenate([kp_ref[0], kc_ref[0]], axis=0)
    vcat = jnp.concatenate([vp_ref[0], vc_ref[0]], axis=0)
    lane = lax.broadcasted_iota(I32, (1, LANES_V7X), 1)
    wcol = lax.broadcasted_iota(I32, (1, ATT_WIN), 1)
    first_pen = jnp.where(qb_idx == 0, NEG_BIG, 0.0)
    scale = A_HEAD_DIM ** -0.5

    for t in range(tq // ATT_SUB):
        r0 = t * ATT_SUB
        w0 = tq + r0 - LEFT_CHUNKS * CHUNK
        kw = kcat[w0:w0 + ATT_WIN]
        vw = vcat[w0:w0 + ATT_WIN]
        pen = jnp.where(wcol < (tq - w0), first_pen, 0.0)
        qs = q2[r0:r0 + ATT_SUB]
        outs = []
        for hh in range(2):
            head_lanes = (lane >= hh * A_HEAD_DIM) & (lane < (hh + 1) * A_HEAD_DIM)
            qh = jnp.where(head_lanes, qs, jnp.zeros_like(qs)) * scale
            sc = _nt_dot(qh.astype(BF16), kw) + bias_ref[hh] + pen
            mx = jnp.max(sc, axis=1, keepdims=True)
            p = jnp.exp(sc - mx)
            den = jnp.sum(p, axis=1, keepdims=True)
            o = jnp.dot(p.astype(BF16), vw, preferred_element_type=F32) / den
            outs.append(o)
        out_ref[0, r0:r0 + ATT_SUB, :] = jnp.where(lane < A_HEAD_DIM, outs[0], outs[1]).astype(out_ref.dtype)


def _band_attn(p3, bias):
    B, S, _ = p3.shape
    tq = ATT_TQ
    nq = S // tq
    nhp = A_HEADS // 2
    cq, ck, cv = COL_QA // LANES_V7X, COL_KA // LANES_V7X, COL_VA // LANES_V7X
    cur = lambda c0: pl.BlockSpec((1, tq, LANES_V7X), lambda b, hp, i, c0=c0: (b, i, c0 + hp))
    prev = lambda c0: pl.BlockSpec((1, tq, LANES_V7X), lambda b, hp, i, c0=c0: (b, jnp.maximum(i - 1, 0), c0 + hp))
    return pl.pallas_call(
        _band_attn_kernel,
        out_shape=jax.ShapeDtypeStruct((B, S, A_WIDTH), BF16),
        grid=(B, nhp, nq),
        in_specs=[cur(cq), prev(ck), cur(ck), prev(cv), cur(cv),
                  pl.BlockSpec((2, ATT_SUB, ATT_WIN), lambda b, hp, i: (hp, 0, 0))],
        out_specs=pl.BlockSpec((1, tq, LANES_V7X), lambda b, hp, i: (b, i, hp)),
        compiler_params=pltpu.CompilerParams(
            dimension_semantics=("parallel", "parallel", "arbitrary"),
            vmem_limit_bytes=_vmem_limit(32 << 20)),
        name="band_attn",
    )(p3, p3, p3, p3, p3, bias)


def _band_bias(rel_table):
    i = jnp.arange(ATT_SUB)[:, None]
    kp = jnp.arange(ATT_WIN)[None, :]
    rel = jnp.clip(LEFT_CHUNKS * CHUNK + i - kp, -MAX_REL, MAX_REL) + MAX_REL
    ci, kc = i // CHUNK, kp // CHUNK
    valid = (kc >= ci) & (kc <= ci + LEFT_CHUNKS)
    return jnp.where(valid[None], rel_table[:, rel].astype(F32), NEG_BIG)


def _merge_kernel(hm_ref, ya_ref, gm_ref, ga_ref, x_ref, wbm_ref, wba_ref, wo_ref, out_ref):
    ym = jnp.dot(hm_ref[...], wbm_ref[...], preferred_element_type=F32)
    ya = jnp.dot(ya_ref[...], wba_ref[...], preferred_element_type=F32)
    merged = _sigmoid(gm_ref[...].astype(F32)) * ym + _sigmoid(ga_ref[...].astype(F32)) * ya
    out_ref[...] = x_ref[...] + jnp.dot(merged.astype(BF16), wo_ref[...], preferred_element_type=F32)


def _const_spec(shape):
    return pl.BlockSpec(shape, lambda i: (0,) * len(shape), pipeline_mode=pl.Buffered(1))


def _merge(hm2d, ya2d, p2d, x2d, wbm, wba, wout):
    T, D = x2d.shape
    tm = min(MERGE_TM, T)
    vmem = ((2 * M_WIDTH * D + D * D) * 2 + 2 * (2 * tm * M_WIDTH * 2 + 2 * tm * D * 2 + 2 * tm * D * 4)
            + 6 * tm * D * 4)
    return pl.pallas_call(
        _merge_kernel,
        out_shape=jax.ShapeDtypeStruct((T, D), F32),
        grid=(T // tm,),
        in_specs=[
            pl.BlockSpec((tm, M_WIDTH), lambda i: (i, 0)),
            pl.BlockSpec((tm, A_WIDTH), lambda i: (i, 0)),
            pl.BlockSpec((tm, D), lambda i: (i, COL_GM // D)),
            pl.BlockSpec((tm, D), lambda i: (i, COL_GA // D)),
            pl.BlockSpec((tm, D), lambda i: (i, 0)),
            _const_spec((M_WIDTH, D)), _const_spec((A_WIDTH, D)), _const_spec((D, D)),
        ],
        out_specs=pl.BlockSpec((tm, D), lambda i: (i, 0)),
        compiler_params=pltpu.CompilerParams(
            dimension_semantics=("parallel",), vmem_limit_bytes=_vmem_limit(vmem)),
        name="merge_out",
    )(hm2d, ya2d, p2d, p2d, x2d, wbm, wba, wout)


def _memkv_kernel(mem_ref, nw_ref, w_ref, k_ref, v_ref):
    hb = _rms(mem_ref[0], nw_ref[...]).astype(BF16)
    kv = jnp.dot(hb, w_ref[...], preferred_element_type=F32)
    k_ref[0] = kv[:, :X_WIDTH].astype(k_ref.dtype)
    v_ref[0] = kv[:, X_WIDTH:].astype(v_ref.dtype)


def _memkv(mem, norm_w, wkv):
    B, N, D = mem.shape
    return pl.pallas_call(
        _memkv_kernel,
        out_shape=(jax.ShapeDtypeStruct((B, N, X_WIDTH), BF16), jax.ShapeDtypeStruct((B, N, X_WIDTH), BF16)),
        grid=(B,),
        in_specs=[pl.BlockSpec((1, N, D), lambda b: (b, 0, 0)),
                  pl.BlockSpec((1, D), lambda b: (0, 0)),
                  pl.BlockSpec((D, 2 * X_WIDTH), lambda b: (0, 0))],
        out_specs=(pl.BlockSpec((1, N, X_WIDTH), lambda b: (b, 0, 0)),
                   pl.BlockSpec((1, N, X_WIDTH), lambda b: (b, 0, 0))),
        compiler_params=pltpu.CompilerParams(
            dimension_semantics=("parallel",), vmem_limit_bytes=_vmem_limit(24 << 20)),
        name="mem_kv",
    )(mem, norm_w, wkv)


def _xattn_kernel(x_ref, nxw_ref, wq_ref, km_ref, vm_ref, wo_ref, nfw_ref, wrh_ref, wrl_ref, br_ref,
                  x2_ref, h3_ref, te_ref, tg_ref):
    x1 = x_ref[...]
    hb = _rms(x1, nxw_ref[...]).astype(BF16)
    q = jnp.dot(hb, wq_ref[...], preferred_element_type=F32).astype(BF16)
    km = km_ref[0]
    vm = vm_ref[0]
    outs = []
    for h in range(X_HEADS):
        lo, hi = h * X_HEAD_DIM, (h + 1) * X_HEAD_DIM
        sc = _nt_dot(q[:, lo:hi], km[:, lo:hi]) * (X_HEAD_DIM ** -0.5)
        mx = jnp.max(sc, axis=1, keepdims=True)
        p = jnp.exp(sc - mx)
        den = jnp.sum(p, axis=1, keepdims=True)
        outs.append(jnp.dot(p.astype(BF16), vm[:, lo:hi], preferred_element_type=F32) / den)
    o = jnp.concatenate(outs, axis=1).astype(BF16)
    x2 = x1 + jnp.dot(o, wo_ref[...], preferred_element_type=F32)
    x2_ref[...] = x2

    h3 = _rms(x2, nfw_ref[...])
    h3_ref[...] = h3
    h_hi = h3.astype(BF16)
    h_lo = (h3 - h_hi.astype(F32)).astype(BF16)
    logits = (jnp.dot(h_hi, wrh_ref[...], preferred_element_type=F32)
              + jnp.dot(h_lo, wrh_ref[...], preferred_element_type=F32)
              + jnp.dot(h_hi, wrl_ref[...], preferred_element_type=F32)
              + br_ref[...])

    colf = lax.broadcasted_iota(I32, logits.shape, 1).astype(F32)
    work = logits
    vals, idxs = [], []
    for _ in range(TOP_K):
        mx = jnp.max(work, axis=1, keepdims=True)
        ix = jnp.min(jnp.where(work == mx, colf, float(LANES_V7X)), axis=1, keepdims=True)
        vals.append(mx)
        idxs.append(ix)
        work = jnp.where(colf == ix, -jnp.inf, work)
    es = [jnp.exp(v - vals[0]) for v in vals]
    tot = es[0] + es[1] + es[2] + es[3]
    te = jnp.zeros_like(logits)
    tg = jnp.zeros_like(logits)
    for kk in range(TOP_K):
        te = jnp.where(colf == float(kk), idxs[kk], te)
        tg = jnp.where(colf == float(kk), es[kk] / tot, tg)
    te_ref[...] = te.astype(I32)
    tg_ref[...] = tg


def _xattn(x1, batch_rows, norm_x_w, wq, km, vm, wo, norm_ffn_w, wr_hi, wr_lo, br):
    T, D = x1.shape
    tm = min(XATT_TM, batch_rows)
    per_b = batch_rows // tm
    n_mem = km.shape[1]
    row = lambda w: pl.BlockSpec((tm, w), lambda i: (i, 0))
    vmem = (2 * D * X_WIDTH * 2 + 2 * D * LANES_V7X * 2 + 4 * n_mem * X_WIDTH * 2
            + 2 * 3 * tm * D * 4 + 8 * tm * D * 4)
    return pl.pallas_call(
        _xattn_kernel,
        out_shape=(jax.ShapeDtypeStruct((T, D), F32), jax.ShapeDtypeStruct((T, D), F32),
                   jax.ShapeDtypeStruct((T, LANES_V7X), I32), jax.ShapeDtypeStruct((T, LANES_V7X), F32)),
        grid=(T // tm,),
        in_specs=[
            row(D), _const_spec((1, D)), _const_spec((D, X_WIDTH)),
            pl.BlockSpec((1, n_mem, X_WIDTH), lambda i: (i // per_b, 0, 0)),
            pl.BlockSpec((1, n_mem, X_WIDTH), lambda i: (i // per_b, 0, 0)),
            _const_spec((X_WIDTH, D)), _const_spec((1, D)),
            _const_spec((D, LANES_V7X)), _const_spec((D, LANES_V7X)), _const_spec((1, LANES_V7X)),
        ],
        out_specs=(row(D), row(D), row(LANES_V7X), row(LANES_V7X)),
        compiler_params=pltpu.CompilerParams(
            dimension_semantics=("parallel",), vmem_limit_bytes=_vmem_limit(vmem)),
        name="xattn_router",
    )(x1, norm_x_w, wq, km, vm, wo, norm_ffn_w, wr_hi, wr_lo, br)


def _dispatch_kernel(dest_ref, h_ref, xs_in_ref, xs_ref, sem):
    del xs_in_ref
    tt = h_ref.shape[0]

    def row_copy(t, kk):
        return pltpu.make_async_copy(
            h_ref.at[pl.ds(t, 1)], xs_ref.at[pl.ds(dest_ref[0, t * TOP_K + kk], 1)], sem)

    def start(t, carry):
        for kk in range(TOP_K):
            row_copy(t, kk).start()
        return carry

    def wait(t, carry):
        for kk in range(TOP_K):
            row_copy(t, kk).wait()
        return carry

    lax.fori_loop(0, tt, start, 0)
    lax.fori_loop(0, tt, wait, 0)


def _dispatch(dest3, h3, xs_zero):
    T, D = h3.shape
    tt = min(DISP_TT, T)
    return pl.pallas_call(
        _dispatch_kernel,
        out_shape=jax.ShapeDtypeStruct(xs_zero.shape, xs_zero.dtype),
        grid=(T // tt,),
        in_specs=[
            pl.BlockSpec((None, 1, tt * TOP_K), lambda i: (i, 0, 0), memory_space=pltpu.SMEM),
            pl.BlockSpec((tt, D), lambda i: (i, 0)),
            pl.BlockSpec(memory_space=pl.ANY),
        ],
        out_specs=pl.BlockSpec(memory_space=pl.ANY),
        scratch_shapes=[pltpu.SemaphoreType.DMA(())],
        input_output_aliases={2: 0},
        compiler_params=pltpu.CompilerParams(
            dimension_semantics=("arbitrary",), has_side_effects=True,
            vmem_limit_bytes=_vmem_limit(4 * tt * D * 4 + (8 << 20))),
        name="moe_dispatch",
    )(dest3, h3, xs_zero)


def _experts_kernel(te_ref, nu_ref, x_ref, wg_ref, wu_ref, bg_ref, bu_ref, wd_ref, bd_ref, out_ref, xb_scr, acc_scr):
    i = pl.program_id(0)
    j = pl.program_id(1)
    last = pl.num_programs(1) - 1
    used = i < nu_ref[0]

    @pl.when(used)
    def _():
        @pl.when(j == 0)
        def _():
            xb_scr[...] = x_ref[...].astype(BF16)
            acc_scr[...] = jnp.zeros_like(acc_scr)

        xb = xb_scr[...]
        g = jnp.dot(xb, wg_ref[...], preferred_element_type=F32) + bg_ref[...]
        u = jnp.dot(xb, wu_ref[...], preferred_element_type=F32) + bu_ref[...]
        g = jnp.minimum(g, SWIGLU_LIMIT)
        u = jnp.clip(u, -SWIGLU_LIMIT, SWIGLU_LIMIT)
        act = (u + 1.0) * (g * _sigmoid(SWIGLU_ALPHA * g))
        acc_scr[...] += jnp.dot(act.astype(BF16), wd_ref[...], preferred_element_type=F32)

        @pl.when(j == last)
        def _():
            out_ref[...] = acc_scr[...] + bd_ref[...]

    @pl.when(jnp.logical_not(used) & (j == last))
    def _():
        out_ref[...] = jnp.zeros_like(out_ref)


def _experts(tile_e, n_used, xs, wgu, bgu, wdn, bdn):
    n_rows, D = xs.shape
    tm, tf = MOE_TM, MOE_TF
    nj = D_FF // tf
    grid_spec = pltpu.PrefetchScalarGridSpec(
        num_scalar_prefetch=2,
        grid=(n_rows // tm, nj),
        in_specs=[
            pl.BlockSpec((tm, D), lambda i, j, te, nu: (i, 0)),
            pl.BlockSpec((None, D, tf), lambda i, j, te, nu: (te[i], 0, j)),
            pl.BlockSpec((None, D, tf), lambda i, j, te, nu: (te[i], 0, nj + j)),
            pl.BlockSpec((None, 1, tf), lambda i, j, te, nu: (te[i], 0, j)),
            pl.BlockSpec((None, 1, tf), lambda i, j, te, nu: (te[i], 0, nj + j)),
            pl.BlockSpec((None, tf, D), lambda i, j, te, nu: (te[i], j, 0)),
            pl.BlockSpec((None, 1, D), lambda i, j, te, nu: (te[i], 0, 0)),
        ],
        out_specs=pl.BlockSpec((tm, D), lambda i, j, te, nu: (i, 0)),
        scratch_shapes=[pltpu.VMEM((tm, D), BF16), pltpu.VMEM((tm, D), F32)],
    )
    vmem = 2 * tm * D * 4 + tm * D * 2 + tm * D * 4 + 2 * 3 * D * tf * 2 + 2 * tm * D * 4 + 6 * tm * tf * 4
    return pl.pallas_call(
        _experts_kernel,
        out_shape=jax.ShapeDtypeStruct((n_rows, D), F32),
        grid_spec=grid_spec,
        compiler_params=pltpu.CompilerParams(
            dimension_semantics=("arbitrary", "arbitrary"), vmem_limit_bytes=_vmem_limit(vmem + (8 << 20))),
        name="moe_experts",
    )(tile_e, n_used, xs, wgu, wgu, bgu, bgu, wdn, bdn)


def _combine_kernel(dest_ref, gate_ref, x_ref, y_ref, nw_ref, out_ref, buf, sem):
    tt = x_ref.shape[0]

    def row_copy(t, kk):
        return pltpu.make_async_copy(
            y_ref.at[pl.ds(dest_ref[0, t * TOP_K + kk], 1)], buf.at[kk, pl.ds(t, 1)], sem)

    def start(t, carry):
        for kk in range(TOP_K):
            row_copy(t, kk).start()
        return carry

    def wait(t, carry):
        for kk in range(TOP_K):
            row_copy(t, kk).wait()
        return carry

    lax.fori_loop(0, tt, start, 0)
    lax.fori_loop(0, tt, wait, 0)
    gate = gate_ref[...]
    y = x_ref[...]
    for kk in range(TOP_K):
        y = y + gate[:, kk:kk + 1] * buf[kk]
    out_ref[...] = _rms(y, nw_ref[...])


def _combine(dest3, gates, x2, yrows, final_w):
    T, D = x2.shape
    tt = min(COMB_TT, T)
    return pl.pallas_call(
        _combine_kernel,
        out_shape=jax.ShapeDtypeStruct((T, D), F32),
        grid=(T // tt,),
        in_specs=[
            pl.BlockSpec((None, 1, tt * TOP_K), lambda i: (i, 0, 0), memory_space=pltpu.SMEM),
            pl.BlockSpec((tt, LANES_V7X), lambda i: (i, 0)),
            pl.BlockSpec((tt, D), lambda i: (i, 0)),
            pl.BlockSpec(memory_space=pl.ANY),
            pl.BlockSpec((1, D), lambda i: (0, 0)),
        ],
        out_specs=pl.BlockSpec((tt, D), lambda i: (i, 0)),
        scratch_shapes=[pltpu.VMEM((TOP_K, tt, D), F32), pltpu.SemaphoreType.DMA(())],
        compiler_params=pltpu.CompilerParams(
            dimension_semantics=("arbitrary",),
            vmem_limit_bytes=_vmem_limit(TOP_K * tt * D * 4 + 6 * tt * D * 4 + (8 << 20))),
        name="moe_combine",
    )(dest3, gates, x2, yrows, final_w)


def _route(top_e, n_rows):
    flat_e = top_e.reshape(-1)
    onehot = (flat_e[:, None] == jnp.arange(N_EXPERTS, dtype=I32)[None, :]).astype(I32)
    incl = jnp.cumsum(onehot, axis=0)
    counts = incl[-1]
    pos = jnp.sum((incl - onehot) * onehot, axis=1)
    padded = ((counts + MOE_TM - 1) // MOE_TM) * MOE_TM
    pend = jnp.cumsum(padded)
    pstart = pend - padded
    dest = pstart[flat_e] + pos
    n_tiles = n_rows // MOE_TM
    tile_e = jnp.searchsorted(pend, jnp.arange(n_tiles, dtype=I32) * MOE_TM, side="right")
    tile_e = jnp.minimum(tile_e, N_EXPERTS - 1).astype(I32)
    n_used = (pend[-1] // MOE_TM).astype(I32).reshape(1)
    return dest.astype(I32), tile_e, n_used


def kernel(x, mem, norm_mix_w, w_in, conv_w, conv_b, if_bias, m_head_norm_w, w_branch_m, w_branch_a, rel_bias, w_out, norm_x_w, norm_mem_w, wq_x, wkv_x, wo_x, norm_ffn_w, w_router, b_router, w_gu, b_gu, w_down, b_down, final_norm_w):
    B, S, D = x.shape
    T = B * S
    l = 0
    x2d = x.reshape(T, D)

    wi = w_in[l]
    o_qm, o_km, o_vm, o_om = 0, M_WIDTH, 2 * M_WIDTH, 3 * M_WIDTH
    o_if = 4 * M_WIDTH
    o_qa = o_if + 2 * M_HEADS
    o_ka, o_va = o_qa + A_WIDTH, o_qa + 2 * A_WIDTH
    o_gm = o_qa + 3 * A_WIDTH
    o_ga = o_gm + D
    seg = lambda o, w: wi[:, o:o + w]
    w_main = jnp.concatenate(
        [seg(o_gm, D), seg(o_ga, D), seg(o_qm, M_WIDTH), seg(o_km, M_WIDTH), seg(o_vm, M_WIDTH),
         seg(o_om, M_WIDTH), seg(o_qa, A_WIDTH), seg(o_ka, A_WIDTH), seg(o_va, A_WIDTH)], axis=1).astype(BF16)
    w_if = jnp.pad(seg(o_if, 2 * M_HEADS), ((0, 0), (0, LANES_V7X - 2 * M_HEADS))).astype(BF16)

    proj, if_rows = _inproj(x2d, norm_mix_w[l].reshape(1, D), w_main, w_if)
    p3 = proj.reshape(B, S, N_MAIN)

    nc = S // CHUNK
    gcol3 = if_rows.reshape(B, S, LANES_V7X)
    grow4 = if_rows[:, :2 * M_HEADS].reshape(B, nc, CHUNK, 2 * M_HEADS).transpose(0, 1, 3, 2)
    bias_col = jnp.pad(if_bias[l], (0, LANES_V7X - 2 * M_HEADS)).reshape(1, LANES_V7X)
    bias_row = jnp.broadcast_to(if_bias[l][:, None], (2 * M_HEADS, CHUNK))
    hm = _mlstm(p3, gcol3, grow4, conv_w[l], conv_b[l].reshape(1, 2 * M_WIDTH), bias_col, bias_row,
                m_head_norm_w[l].reshape(1, M_WIDTH))

    ya = _band_attn(p3, _band_bias(rel_bias[l]))

    x1 = _merge(hm.reshape(T, M_WIDTH), ya.reshape(T, A_WIDTH), proj, x2d,
                w_branch_m[l].astype(BF16), w_branch_a[l].astype(BF16), w_out[l].astype(BF16))

    km, vm = _memkv(mem, norm_mem_w[l].reshape(1, D), wkv_x[l].astype(BF16))
    wr = jnp.pad(w_router[l], ((0, 0), (0, LANES_V7X - N_EXPERTS)))
    wr_hi = wr.astype(BF16)
    wr_lo = (wr - wr_hi.astype(F32)).astype(BF16)
    br = jnp.concatenate([b_router[l], jnp.full((LANES_V7X - N_EXPERTS,), NEG_BIG, F32)]).reshape(1, LANES_V7X)
    x2, h3, te, tg = _xattn(x1, S, norm_x_w[l].reshape(1, D), wq_x[l].astype(BF16), km, vm,
                            wo_x[l].astype(BF16), norm_ffn_w[l].reshape(1, D), wr_hi, wr_lo, br)

    A = T * TOP_K
    n_rows = (A // MOE_TM + N_EXPERTS) * MOE_TM
    dest, tile_e, n_used = _route(te[:, :TOP_K], n_rows)
    xs = _dispatch(dest.reshape(T // min(DISP_TT, T), 1, -1), h3, jnp.zeros((n_rows, D), F32))
    yrows = _experts(tile_e, n_used, xs, w_gu[l].astype(BF16), b_gu[l].reshape(N_EXPERTS, 1, 2 * D_FF),
                     w_down[l].astype(BF16), b_down[l].reshape(N_EXPERTS, 1, D))
    out = _combine(dest.reshape(T // min(COMB_TT, T), 1, -1), tg, x2, yrows, final_norm_w.reshape(1, D))
    return out.reshape(B, S, D)
```

```python
import functools

import jax
import jax.numpy as jnp
from jax import lax
from jax.experimental import pallas as pl
from jax.experimental.pallas import tpu as pltpu

F32 = jnp.float32
BF16 = jnp.bfloat16
I32 = jnp.int32
U32 = jnp.uint32

D_MODEL = 2048
CHUNK = 64
EPS = 1e-6
M_HEADS = 4
M_HEAD_DIM = 256
M_WIDTH = M_HEADS * M_HEAD_DIM
CONV_WIDTH = 4
A_HEADS = 16
A_HEAD_DIM = 64
A_WIDTH = A_HEADS * A_HEAD_DIM
LEFT_CHUNKS = 8
MAX_REL = 128
X_HEADS = 4
X_HEAD_DIM = 128
X_WIDTH = X_HEADS * X_HEAD_DIM
N_EXPERTS = 32
TOP_K = 4
D_FF = D_MODEL
SWIGLU_LIMIT = 7.0
SWIGLU_ALPHA = 1.702

LANES_V7X = 128
SUBLANES_V7X = 8
VMEM_BYTES_V7X = 64 * 1024 * 1024
VMEM_LIMIT_CAP = 60000 * 1024

NEG_BIG = -1e30

COL_GM = 0
COL_GA = 2048
COL_QM = 4096
COL_KM = 5120
COL_VM = 6144
COL_OM = 7168
COL_QA = 8192
COL_KA = 9216
COL_VA = 10240
N_MAIN = 11264

IN_TM = 1024
IN_TN = 1024
ATT_TQ = 512
ATT_SUB = 128
ATT_WIN = ATT_SUB + LEFT_CHUNKS * CHUNK
MERGE_TM = 512
XATT_TM = 512
MOE_BLK = 256
MOE_G = 5
MOE_R = MOE_G * MOE_BLK
MOE_BIG = 4
MOE_TF = 256
DISP_TT = 256
COMB_TT = 128


def _vmem_limit(nbytes):
    return int(min(VMEM_LIMIT_CAP, max(nbytes, 16 * 1024 * 1024)))


def _nt_dot(a, b):
    return lax.dot_general(a, b, (((1,), (1,)), ((), ())), preferred_element_type=F32)


def _tn_dot(a, b):
    return lax.dot_general(a, b, (((0,), (0,)), ((), ())), preferred_element_type=F32)


def _sigmoid(x):
    return 1.0 / (1.0 + jnp.exp(-x))


def _log_sigmoid(x):
    return jnp.minimum(x, 0.0) - jnp.log1p(jnp.exp(-jnp.abs(x)))


def _rms(x, w):
    return x * lax.rsqrt(jnp.mean(x * x, axis=-1, keepdims=True) + EPS) * w


def _pack_bf16_pairs(xb):
    n = xb.shape[1] // 2
    hi = lax.bitcast_convert_type(xb[:, :n].astype(F32), U32)
    lo = lax.bitcast_convert_type(xb[:, n:].astype(F32), U32)
    return hi | (lo >> 16)


def _unpack_bf16_pairs(w):
    hi = lax.bitcast_convert_type(w & jnp.uint32(0xFFFF0000), F32)
    lo = lax.bitcast_convert_type(w << 16, F32)
    return hi, lo


def _inproj_kernel(x_ref, nw_ref, w_ref, wif_ref, o_ref, if_ref, h_scr):
    @pl.when(pl.program_id(1) == 0)
    def _():
        hb = _rms(x_ref[...], nw_ref[...]).astype(BF16)
        h_scr[...] = hb
        if_ref[...] = jnp.dot(hb, wif_ref[...], preferred_element_type=F32)

    o_ref[...] = jnp.dot(h_scr[...], w_ref[...], preferred_element_type=F32).astype(o_ref.dtype)


def _inproj(x2d, norm_w, w_main, w_if):
    T, D = x2d.shape
    tm = min(IN_TM, T)
    grid = (T // tm, N_MAIN // IN_TN)
    vmem = 2 * tm * D * 4 + tm * D * 2 + 2 * D * IN_TN * 2 + 2 * tm * IN_TN * 2 + 4 * tm * LANES_V7X * 4
    return pl.pallas_call(
        _inproj_kernel,
        out_shape=(jax.ShapeDtypeStruct((T, N_MAIN), BF16), jax.ShapeDtypeStruct((T, LANES_V7X), F32)),
        grid=grid,
        in_specs=[
            pl.BlockSpec((tm, D), lambda i, j: (i, 0)),
            pl.BlockSpec((1, D), lambda i, j: (0, 0)),
            pl.BlockSpec((D, IN_TN), lambda i, j: (0, j)),
            pl.BlockSpec((D, LANES_V7X), lambda i, j: (0, 0)),
        ],
        out_specs=(
            pl.BlockSpec((tm, IN_TN), lambda i, j: (i, j)),
            pl.BlockSpec((tm, LANES_V7X), lambda i, j: (i, 0)),
        ),
        scratch_shapes=[pltpu.VMEM((tm, D), BF16)],
        compiler_params=pltpu.CompilerParams(
            dimension_semantics=("parallel", "arbitrary"),
            vmem_limit_bytes=_vmem_limit(vmem + (8 << 20))),
        name="inproj",
    )(x2d, norm_w, w_main, w_if)


def _mlstm_kernel(q_ref, k_ref, v_ref, o_ref, gc_ref, gr_ref, cw_ref, cb_ref, bc_ref, br_ref, nw_ref,
                  out_ref, ubuf, c_st, n_st, m_st):
    nb = q_ref.shape[0]
    L = CHUNK
    dh = M_HEAD_DIM
    halo = SUBLANES_V7X

    @pl.when(pl.program_id(0) == 0)
    def _():
        ubuf[:, 0:halo, :] = jnp.zeros((nb, halo, 2 * M_WIDTH), F32)
        c_st[...] = jnp.zeros_like(c_st)
        n_st[...] = jnp.zeros_like(n_st)
        m_st[...] = jnp.zeros_like(m_st)

    row = lax.broadcasted_iota(I32, (L, L), 0)
    col = lax.broadcasted_iota(I32, (L, L), 1)
    lower = col <= row

    for b in range(nb):
        ubuf[b, halo:halo + L, 0:M_WIDTH] = q_ref[b].astype(F32)
        ubuf[b, halo:halo + L, M_WIDTH:2 * M_WIDTH] = k_ref[b].astype(F32)
        conv = cb_ref[...]
        for j in range(CONV_WIDTH):
            off = halo - (CONV_WIDTH - 1) + j
            conv = conv + ubuf[b, off:off + L, :] * cw_ref[j:j + 1, :]
        qk = conv * _sigmoid(conv)
        ubuf[b, 0:halo, :] = ubuf[b, L:L + halo, :]

        gcol = gc_ref[b] + bc_ref[...]
        grow = gr_ref[b, 0] + br_ref[...]

        for h in range(M_HEADS):
            s = b * M_HEADS + h
            lo, hi = h * dh, (h + 1) * dh
            q = qk[:, lo:hi]
            k = qk[:, M_WIDTH + lo:M_WIDTH + hi] * (dh ** -0.5)
            vb = v_ref[b, :, lo:hi]
            qb = q.astype(BF16)
            kb = k.astype(BF16)

            ig_c = gcol[:, h:h + 1]
            ig_r = grow[h:h + 1, :]
            ls_c = _log_sigmoid(gcol[:, M_HEADS + h:M_HEADS + h + 1])
            ls_r = _log_sigmoid(grow[M_HEADS + h:M_HEADS + h + 1, :])
            b_c = jnp.sum(jnp.where(lower, ls_r, 0.0), axis=1, keepdims=True)
            b_r = jnp.sum(jnp.where(row <= col, ls_c, 0.0), axis=0, keepdims=True)
            g = jnp.sum(ls_r, axis=1, keepdims=True)
            m_old = m_st[s][:, 0:1]

            w_end = g - b_c + ig_c
            a = jnp.max(w_end, axis=0, keepdims=True)
            e_end = jnp.exp(w_end - a)

            dlog = jnp.where(lower, b_c - b_r + ig_r, NEG_BIG)
            inter = b_c + m_old
            m_j = jnp.maximum(inter, jnp.max(dlog, axis=1, keepdims=True))
            sm = _nt_dot(qb, kb) * jnp.exp(dlog - m_j)
            w_int = jnp.exp(inter - m_j)

            c_old = c_st[s]
            n_old = n_st[s]
            num = (jnp.dot(sm.astype(BF16), vb, preferred_element_type=F32)
                   + w_int * _nt_dot(qb, c_old.astype(BF16)))
            den = (jnp.sum(sm, axis=1, keepdims=True)
                   + w_int * jnp.sum(q * n_old, axis=1, keepdims=True))
            hh = num / jnp.maximum(jnp.abs(den), jnp.exp(-m_j))

            hn = _rms(hh, nw_ref[:, lo:hi])
            og = _sigmoid(o_ref[b, :, lo:hi].astype(F32))
            out_ref[b, :, lo:hi] = (og * hn).astype(out_ref.dtype)

            m_new = jnp.maximum(g + m_old, a)
            s_prev = jnp.exp(g + m_old - m_new)
            s_loc = jnp.exp(a - m_new)
            ve = (vb.astype(F32) * e_end).astype(BF16)
            c_st[s] = s_prev * c_old + s_loc * _tn_dot(ve, kb)
            n_st[s] = s_prev * n_old + s_loc * jnp.sum(e_end * k, axis=0, keepdims=True)
            m_st[s] = jnp.broadcast_to(m_new, (1, LANES_V7X))


def _mlstm(p3, gcol3, grow4, conv_w, conv_b, bias_col, bias_row, head_w):
    B, S, _ = p3.shape
    nc = S // CHUNK
    ns = B * M_HEADS
    blk = lambda cb: pl.BlockSpec((B, CHUNK, M_WIDTH), lambda c, cb=cb: (0, c, cb))
    full = lambda shape: pl.BlockSpec(shape, lambda c: (0,) * len(shape))
    vmem = (2 * 5 * B * CHUNK * M_WIDTH * 2 + B * (CHUNK + 8) * 2 * M_WIDTH * 4
            + ns * M_HEAD_DIM * M_HEAD_DIM * 4 + (16 << 20))
    return pl.pallas_call(
        _mlstm_kernel,
        out_shape=jax.ShapeDtypeStruct((B, S, M_WIDTH), BF16),
        grid=(nc,),
        in_specs=[
            blk(COL_QM // M_WIDTH), blk(COL_KM // M_WIDTH), blk(COL_VM // M_WIDTH), blk(COL_OM // M_WIDTH),
            pl.BlockSpec((B, CHUNK, LANES_V7X), lambda c: (0, c, 0)),
            pl.BlockSpec((B, 1, 2 * M_HEADS, CHUNK), lambda c: (0, c, 0, 0)),
            full((CONV_WIDTH, 2 * M_WIDTH)), full((1, 2 * M_WIDTH)),
            full((1, LANES_V7X)), full((2 * M_HEADS, CHUNK)), full((1, M_WIDTH)),
        ],
        out_specs=pl.BlockSpec((B, CHUNK, M_WIDTH), lambda c: (0, c, 0)),
        scratch_shapes=[
            pltpu.VMEM((B, CHUNK + SUBLANES_V7X, 2 * M_WIDTH), F32),
            pltpu.VMEM((ns, M_HEAD_DIM, M_HEAD_DIM), F32),
            pltpu.VMEM((ns, 1, M_HEAD_DIM), F32),
            pltpu.VMEM((ns, 1, LANES_V7X), F32),
        ],
        compiler_params=pltpu.CompilerParams(
            dimension_semantics=("arbitrary",), vmem_limit_bytes=_vmem_limit(vmem)),
        name="mlstm",
    )(p3, p3, p3, p3, gcol3, grow4, conv_w, conv_b, bias_col, bias_row, head_w)


def _band_attn_kernel(q_ref, kp_ref, kc_ref, vp_ref, vc_ref, bias_ref, out_ref):
    qb_idx = pl.program_id(2)
    tq = q_ref.shape[1]
    q2 = q_ref[0]
    kcat = jnp.concatenate([kp_ref[0], kc_ref[0]], axis=0)
    vcat = jnp.concatenate([vp_ref[0], vc_ref[0]], axis=0)
    lane = lax.broadcasted_iota(I32, (1, LANES_V7X), 1)
    wcol = lax.broadcasted_iota(I32, (1, ATT_WIN), 1)
    first_pen = jnp.where(qb_idx == 0, NEG_BIG, 0.0)
    scale = A_HEAD_DIM ** -0.5
    bias = bias_ref[0]

    for t in range(tq // ATT_SUB):
        r0 = t * ATT_SUB
        w0 = tq + r0 - LEFT_CHUNKS * CHUNK
        kw = kcat[w0:w0 + ATT_WIN]
        vw = vcat[w0:w0 + ATT_WIN]
        pen = jnp.where(wcol < (tq - w0), first_pen, 0.0)
        qs = q2[r0:r0 + ATT_SUB] * scale
        zero = jnp.zeros_like(qs)
        qst = jnp.concatenate([jnp.where(lane < A_HEAD_DIM, qs, zero),
                               jnp.where(lane >= A_HEAD_DIM, qs, zero)], axis=0)
        sc = _nt_dot(qst, kw) + bias + pen
        mx = jnp.max(sc, axis=1, keepdims=True)
        p = jnp.exp(sc - mx)
        den = jnp.sum(p, axis=1, keepdims=True)
        o = jnp.dot(p.astype(BF16), vw, preferred_element_type=F32) / den
        out_ref[0, r0:r0 + ATT_SUB, :] = jnp.where(
            lane < A_HEAD_DIM, o[:ATT_SUB], o[ATT_SUB:]).astype(out_ref.dtype)


def _band_attn(p3, bias):
    B, S, _ = p3.shape
    tq = ATT_TQ
    nq = S // tq
    nhp = A_HEADS // 2
    cq, ck, cv = COL_QA // LANES_V7X, COL_KA // LANES_V7X, COL_VA // LANES_V7X
    cur = lambda c0: pl.BlockSpec((1, tq, LANES_V7X), lambda b, hp, i, c0=c0: (b, i, c0 + hp))
    prev = lambda c0: pl.BlockSpec((1, tq, LANES_V7X), lambda b, hp, i, c0=c0: (b, jnp.maximum(i - 1, 0), c0 + hp))
    return pl.pallas_call(
        _band_attn_kernel,
        out_shape=jax.ShapeDtypeStruct((B, S, A_WIDTH), BF16),
        grid=(B, nhp, nq),
        in_specs=[cur(cq), prev(ck), cur(ck), prev(cv), cur(cv),
                  pl.BlockSpec((1, 2 * ATT_SUB, ATT_WIN), lambda b, hp, i: (hp, 0, 0))],
        out_specs=pl.BlockSpec((1, tq, LANES_V7X), lambda b, hp, i: (b, i, hp)),
        compiler_params=pltpu.CompilerParams(
            dimension_semantics=("parallel", "parallel", "arbitrary"),
            vmem_limit_bytes=_vmem_limit(32 << 20)),
        name="band_attn",
    )(p3, p3, p3, p3, p3, bias)


def _band_bias(rel_table):
    n = 6 * LANES_V7X
    c = jnp.arange(n)
    d = jnp.where(c < ATT_WIN, c, c - n)
    rel = jnp.clip(LEFT_CHUNKS * CHUNK - d, -MAX_REL, MAX_REL) + MAX_REL
    vec = rel_table[:, rel].astype(F32)
    sheared = jnp.tile(vec, (1, ATT_SUB))[:, :ATT_SUB * (n - 1)].reshape(A_HEADS, ATT_SUB, n - 1)
    i = jnp.arange(ATT_SUB)[:, None]
    kp = jnp.arange(ATT_WIN)[None, :]
    ci, kc = i // CHUNK, kp // CHUNK
    valid = (kc >= ci) & (kc <= ci + LEFT_CHUNKS)
    bias = jnp.where(valid[None], sheared[:, :, :ATT_WIN], NEG_BIG)
    return bias.reshape(A_HEADS // 2, 2 * ATT_SUB, ATT_WIN)


def _merge_kernel(hm_ref, ya_ref, gm_ref, ga_ref, x_ref, wbm_ref, wba_ref, wo_ref, out_ref):
    ym = jnp.dot(hm_ref[...], wbm_ref[...], preferred_element_type=F32)
    ya = jnp.dot(ya_ref[...], wba_ref[...], preferred_element_type=F32)
    merged = _sigmoid(gm_ref[...].astype(F32)) * ym + _sigmoid(ga_ref[...].astype(F32)) * ya
    out_ref[...] = x_ref[...] + jnp.dot(merged.astype(BF16), wo_ref[...], preferred_element_type=F32)


def _const_spec(shape):
    return pl.BlockSpec(shape, lambda i: (0,) * len(shape), pipeline_mode=pl.Buffered(1))


def _merge(hm2d, ya2d, p2d, x2d, wbm, wba, wout):
    T, D = x2d.shape
    tm = min(MERGE_TM, T)
    vmem = ((2 * M_WIDTH * D + D * D) * 2 + 2 * (2 * tm * M_WIDTH * 2 + 2 * tm * D * 2 + 2 * tm * D * 4)
            + 6 * tm * D * 4)
    return pl.pallas_call(
        _merge_kernel,
        out_shape=jax.ShapeDtypeStruct((T, D), F32),
        grid=(T // tm,),
        in_specs=[
            pl.BlockSpec((tm, M_WIDTH), lambda i: (i, 0)),
            pl.BlockSpec((tm, A_WIDTH), lambda i: (i, 0)),
            pl.BlockSpec((tm, D), lambda i: (i, COL_GM // D)),
            pl.BlockSpec((tm, D), lambda i: (i, COL_GA // D)),
            pl.BlockSpec((tm, D), lambda i: (i, 0)),
            _const_spec((M_WIDTH, D)), _const_spec((A_WIDTH, D)), _const_spec((D, D)),
        ],
        out_specs=pl.BlockSpec((tm, D), lambda i: (i, 0)),
        compiler_params=pltpu.CompilerParams(
            dimension_semantics=("parallel",), vmem_limit_bytes=_vmem_limit(vmem)),
        name="merge_out",
    )(hm2d, ya2d, p2d, p2d, x2d, wbm, wba, wout)


def _memkv_kernel(mem_ref, nw_ref, w_ref, k_ref, v_ref):
    hb = _rms(mem_ref[0], nw_ref[...]).astype(BF16)
    kv = jnp.dot(hb, w_ref[...], preferred_element_type=F32)
    k_ref[0] = kv[:, :X_WIDTH].astype(k_ref.dtype)
    v_ref[0] = kv[:, X_WIDTH:].astype(v_ref.dtype)


def _memkv(mem, norm_w, wkv):
    B, N, D = mem.shape
    return pl.pallas_call(
        _memkv_kernel,
        out_shape=(jax.ShapeDtypeStruct((B, N, X_WIDTH), BF16), jax.ShapeDtypeStruct((B, N, X_WIDTH), BF16)),
        grid=(B,),
        in_specs=[pl.BlockSpec((1, N, D), lambda b: (b, 0, 0)),
                  pl.BlockSpec((1, D), lambda b: (0, 0)),
                  pl.BlockSpec((D, 2 * X_WIDTH), lambda b: (0, 0))],
        out_specs=(pl.BlockSpec((1, N, X_WIDTH), lambda b: (b, 0, 0)),
                   pl.BlockSpec((1, N, X_WIDTH), lambda b: (b, 0, 0))),
        compiler_params=pltpu.CompilerParams(
            dimension_semantics=("parallel",), vmem_limit_bytes=_vmem_limit(24 << 20)),
        name="mem_kv",
    )(mem, norm_w, wkv)


def _xattn_kernel(x_ref, nxw_ref, wq_ref, km_ref, vm_ref, wo_ref, nfw_ref, wrh_ref, wrl_ref, br_ref,
                  x2_ref, h3_ref, te_ref, tg_ref):
    x1 = x_ref[...]
    hb = _rms(x1, nxw_ref[...]).astype(BF16)
    q = jnp.dot(hb, wq_ref[...], preferred_element_type=F32).astype(BF16)
    km = km_ref[0]
    vm = vm_ref[0]
    outs = []
    for h in range(X_HEADS):
        lo, hi = h * X_HEAD_DIM, (h + 1) * X_HEAD_DIM
        sc = _nt_dot(q[:, lo:hi], km[:, lo:hi]) * (X_HEAD_DIM ** -0.5)
        mx = jnp.max(sc, axis=1, keepdims=True)
        p = jnp.exp(sc - mx)
        den = jnp.sum(p, axis=1, keepdims=True)
        outs.append(jnp.dot(p.astype(BF16), vm[:, lo:hi], preferred_element_type=F32) / den)
    o = jnp.concatenate(outs, axis=1).astype(BF16)
    x2 = x1 + jnp.dot(o, wo_ref[...], preferred_element_type=F32)
    x2_ref[...] = x2

    h3 = _rms(x2, nfw_ref[...])
    h_hi = h3.astype(BF16)
    h3_ref[...] = _pack_bf16_pairs(h_hi)
    h_lo = (h3 - h_hi.astype(F32)).astype(BF16)
    logits = (jnp.dot(h_hi, wrh_ref[...], preferred_element_type=F32)
              + jnp.dot(h_lo, wrh_ref[...], preferred_element_type=F32)
              + jnp.dot(h_hi, wrl_ref[...], preferred_element_type=F32)
              + br_ref[...])

    colf = lax.broadcasted_iota(I32, logits.shape, 1).astype(F32)
    work = logits
    vals, idxs = [], []
    for _ in range(TOP_K):
        mx = jnp.max(work, axis=1, keepdims=True)
        ix = jnp.min(jnp.where(work == mx, colf, float(LANES_V7X)), axis=1, keepdims=True)
        vals.append(mx)
        idxs.append(ix)
        work = jnp.where(colf == ix, -jnp.inf, work)
    es = [jnp.exp(v - vals[0]) for v in vals]
    tot = es[0] + es[1] + es[2] + es[3]
    te = jnp.zeros_like(logits)
    tg = jnp.zeros_like(logits)
    for kk in range(TOP_K):
        te = jnp.where(colf == float(kk), idxs[kk], te)
        tg = jnp.where(colf == float(kk), es[kk] / tot, tg)
    te_ref[...] = te.astype(I32)
    tg_ref[...] = tg


def _xattn(x1, batch_rows, norm_x_w, wq, km, vm, wo, norm_ffn_w, wr_hi, wr_lo, br):
    T, D = x1.shape
    tm = min(XATT_TM, batch_rows)
    per_b = batch_rows // tm
    n_mem = km.shape[1]
    row = lambda w: pl.BlockSpec((tm, w), lambda i: (i, 0))
    vmem = (2 * D * X_WIDTH * 2 + 2 * D * LANES_V7X * 2 + 4 * n_mem * X_WIDTH * 2
            + 2 * 3 * tm * D * 4 + 8 * tm * D * 4)
    return pl.pallas_call(
        _xattn_kernel,
        out_shape=(jax.ShapeDtypeStruct((T, D), F32), jax.ShapeDtypeStruct((T, D // 2), U32),
                   jax.ShapeDtypeStruct((T, LANES_V7X), I32), jax.ShapeDtypeStruct((T, LANES_V7X), F32)),
        grid=(T // tm,),
        in_specs=[
            row(D), _const_spec((1, D)), _const_spec((D, X_WIDTH)),
            pl.BlockSpec((1, n_mem, X_WIDTH), lambda i: (i // per_b, 0, 0)),
            pl.BlockSpec((1, n_mem, X_WIDTH), lambda i: (i // per_b, 0, 0)),
            _const_spec((X_WIDTH, D)), _const_spec((1, D)),
            _const_spec((D, LANES_V7X)), _const_spec((D, LANES_V7X)), _const_spec((1, LANES_V7X)),
        ],
        out_specs=(row(D), row(D // 2), row(LANES_V7X), row(LANES_V7X)),
        compiler_params=pltpu.CompilerParams(
            dimension_semantics=("parallel",), vmem_limit_bytes=_vmem_limit(vmem)),
        name="xattn_router",
    )(x1, norm_x_w, wq, km, vm, wo, norm_ffn_w, wr_hi, wr_lo, br)


def _dispatch_kernel(nb_ref, dest_ref, h_ref, xs_ref, zero_scr, sem, zsem):
    tt = h_ref.shape[0]
    nv = nb_ref.shape[0]

    @pl.when(pl.program_id(0) == 0)
    def _():
        zero_scr[...] = jnp.zeros_like(zero_scr)

        def zero_copy(v, b):
            return pltpu.make_async_copy(
                zero_scr, xs_ref.at[pl.ds(v * MOE_R + b * MOE_BLK, MOE_BLK)], zsem)

        def each_partial_block(v, fn):
            for b in range(MOE_G):
                @pl.when(b >= nb_ref[v] - 1)
                def _():
                    fn(zero_copy(v, b))

        def zstart(v, carry):
            each_partial_block(v, lambda c: c.start())
            return carry

        def zwait(v, carry):
            each_partial_block(v, lambda c: c.wait())
            return carry

        lax.fori_loop(0, nv, zstart, 0)
        lax.fori_loop(0, nv, zwait, 0)

    def row_copy(t, kk):
        return pltpu.make_async_copy(
            h_ref.at[pl.ds(t, 1)], xs_ref.at[pl.ds(dest_ref[0, t * TOP_K + kk], 1)], sem)

    def start(t, carry):
        for kk in range(TOP_K):
            row_copy(t, kk).start()
        return carry

    def wait(t, carry):
        for kk in range(TOP_K):
            row_copy(t, kk).wait()
        return carry

    lax.fori_loop(0, tt, start, 0)
    lax.fori_loop(0, tt, wait, 0)


def _dispatch(visit_nb, dest3, h3p, n_rows):
    T, W = h3p.shape
    tt = min(DISP_TT, T)
    grid_spec = pltpu.PrefetchScalarGridSpec(
        num_scalar_prefetch=1,
        grid=(T // tt,),
        in_specs=[
            pl.BlockSpec((None, 1, tt * TOP_K), lambda i, nb: (i, 0, 0), memory_space=pltpu.SMEM),
            pl.BlockSpec((tt, W), lambda i, nb: (i, 0)),
        ],
        out_specs=pl.BlockSpec(memory_space=pl.ANY),
        scratch_shapes=[pltpu.VMEM((MOE_BLK, W), U32), pltpu.SemaphoreType.DMA(()), pltpu.SemaphoreType.DMA(())],
    )
    return pl.pallas_call(
        _dispatch_kernel,
        out_shape=jax.ShapeDtypeStruct((n_rows, W), U32),
        grid_spec=grid_spec,
        compiler_params=pltpu.CompilerParams(
            dimension_semantics=("arbitrary",), has_side_effects=True,
            vmem_limit_bytes=_vmem_limit(4 * tt * W * 4 + MOE_BLK * W * 4 + (8 << 20))),
        name="moe_dispatch",
    )(visit_nb, dest3, h3p)


def _experts_kernel(ve_ref, nb_ref, nu_ref, x_ref, wg_ref, wu_ref, bg_ref, bu_ref, wd_ref, bd_ref, out_ref,
                    xb_scr, acc_scr, wgb_scr, wub_scr, wdb_scr):
    del ve_ref, nu_ref
    v = pl.program_id(0)
    j = pl.program_id(1)
    last = pl.num_programs(1) - 1
    nb = nb_ref[v]
    half = x_ref.shape[1]

    def block_rows(b):
        return pl.ds(pl.multiple_of(b * MOE_BLK, MOE_BLK), MOE_BLK)

    @pl.when(nb > 0)
    def _():
        @pl.when(j == 0)
        def _():
            def stage(b, carry):
                rows = block_rows(b)
                hi, lo = _unpack_bf16_pairs(x_ref[rows, :])
                xb_scr[rows, 0:half] = hi.astype(BF16)
                xb_scr[rows, half:2 * half] = lo.astype(BF16)
                acc_scr[rows, :] = jnp.zeros((MOE_BLK, acc_scr.shape[1]), F32)
                return carry
            lax.fori_loop(0, nb, stage, 0)

        def ffn_rows(rows, wg, wu, wd):
            xb = xb_scr[rows, :]
            g = jnp.dot(xb, wg, preferred_element_type=F32) + bg_ref[...]
            u = jnp.dot(xb, wu, preferred_element_type=F32) + bu_ref[...]
            g = jnp.minimum(g, SWIGLU_LIMIT)
            u = jnp.clip(u, -SWIGLU_LIMIT, SWIGLU_LIMIT)
            act = (u + 1.0) * (g * _sigmoid(SWIGLU_ALPHA * g))
            acc_scr[rows, :] += jnp.dot(act.astype(BF16), wd, preferred_element_type=F32)

        def cast_weights():
            wg, wu, wd = wg_ref[...].astype(BF16), wu_ref[...].astype(BF16), wd_ref[...].astype(BF16)
            wgb_scr[...] = wg
            wub_scr[...] = wu
            wdb_scr[...] = wd
            return wg, wu, wd

        big = nb >= MOE_BIG

        @pl.when(big)
        def _():
            ffn_rows(pl.ds(0, MOE_BIG * MOE_BLK), *cast_weights())

        @pl.when(jnp.logical_not(big))
        def _():
            cast_weights()

        def block(b, carry):
            ffn_rows(block_rows(b), wgb_scr[...], wub_scr[...], wdb_scr[...])
            return carry
        lax.fori_loop(jnp.where(big, MOE_BIG, 0), nb, block, 0)

        @pl.when(j == last)
        def _():
            def finish(b, carry):
                rows = block_rows(b)
                out_ref[rows, :] = _pack_bf16_pairs((acc_scr[rows, :] + bd_ref[...]).astype(BF16))
                return carry
            lax.fori_loop(0, nb, finish, 0)

            def blank(b, carry):
                out_ref[block_rows(b), :] = jnp.zeros((MOE_BLK, half), U32)
                return carry
            lax.fori_loop(nb, MOE_G, blank, 0)

    @pl.when((nb == 0) & (j == last))
    def _():
        out_ref[...] = jnp.zeros_like(out_ref)


def _experts(visit_e, visit_nb, n_used, xs, wgu, bgu, wdn, bdn):
    n_rows, half = xs.shape
    D = 2 * half
    tf = MOE_TF
    nj = D_FF // tf
    nv = n_rows // MOE_R

    def frozen(v, j, nu):
        return jnp.where(v < nu[0], j, nj - 1)

    def vrow(v, nu):
        return jnp.minimum(v, nu[0] - 1)

    grid_spec = pltpu.PrefetchScalarGridSpec(
        num_scalar_prefetch=3,
        grid=(nv, nj),
        in_specs=[
            pl.BlockSpec((MOE_R, half), lambda v, j, ve, nb, nu: (vrow(v, nu), 0)),
            pl.BlockSpec((None, D, tf), lambda v, j, ve, nb, nu: (ve[v], 0, frozen(v, j, nu))),
            pl.BlockSpec((None, D, tf), lambda v, j, ve, nb, nu: (ve[v], 0, nj + frozen(v, j, nu))),
            pl.BlockSpec((None, 1, tf), lambda v, j, ve, nb, nu: (ve[v], 0, frozen(v, j, nu))),
            pl.BlockSpec((None, 1, tf), lambda v, j, ve, nb, nu: (ve[v], 0, nj + frozen(v, j, nu))),
            pl.BlockSpec((None, tf, D), lambda v, j, ve, nb, nu: (ve[v], frozen(v, j, nu), 0)),
            pl.BlockSpec((None, 1, D), lambda v, j, ve, nb, nu: (ve[v], 0, 0)),
        ],
        out_specs=pl.BlockSpec((MOE_R, half), lambda v, j, ve, nb, nu: (v, 0)),
        scratch_shapes=[pltpu.VMEM((MOE_R, D), BF16), pltpu.VMEM((MOE_R, D), F32),
                        pltpu.VMEM((D, tf), BF16), pltpu.VMEM((D, tf), BF16), pltpu.VMEM((tf, D), BF16)],
    )
    vmem = (2 * MOE_R * half * 4 + MOE_R * D * 2 + MOE_R * D * 4 + 2 * MOE_R * half * 4
            + 2 * 3 * D * tf * 4 + 3 * D * tf * 2 + 8 * MOE_BLK * D * 4)
    return pl.pallas_call(
        _experts_kernel,
        out_shape=jax.ShapeDtypeStruct((n_rows, half), U32),
        grid_spec=grid_spec,
        compiler_params=pltpu.CompilerParams(
            dimension_semantics=("arbitrary", "arbitrary"), vmem_limit_bytes=_vmem_limit(vmem)),
        name="moe_experts",
    )(visit_e, visit_nb, n_used, xs, wgu, wgu, bgu, bgu, wdn, bdn)


def _combine_kernel(dcur_ref, dnext_ref, gate_ref, x_ref, y_ref, nw_ref, out_ref, buf, sems):
    i = pl.program_id(0)
    n = pl.num_programs(0)
    tt = x_ref.shape[0]
    half = y_ref.shape[1]
    slot = lax.rem(i, 2)

    def row_copy(dref, s, t, kk):
        return pltpu.make_async_copy(
            y_ref.at[pl.ds(dref[0, t * TOP_K + kk], 1)], buf.at[s, kk, pl.ds(t, 1)], sems.at[s])

    def issue(dref, s):
        def body(t, carry):
            for kk in range(TOP_K):
                row_copy(dref, s, t, kk).start()
            return carry
        lax.fori_loop(0, tt, body, 0)

    @pl.when(i == 0)
    def _():
        issue(dcur_ref, 0)

    @pl.when(i + 1 < n)
    def _():
        issue(dnext_ref, 1 - slot)

    def wait(t, carry):
        for kk in range(TOP_K):
            row_copy(dcur_ref, slot, t, kk).wait()
        return carry
    lax.fori_loop(0, tt, wait, 0)

    gate = gate_ref[...]
    x = x_ref[...]
    ya, yb = x[:, :half], x[:, half:]
    for kk in range(TOP_K):
        hi, lo = _unpack_bf16_pairs(buf[slot, kk])
        gk = gate[:, kk:kk + 1]
        ya = ya + gk * hi
        yb = yb + gk * lo
    out_ref[...] = _rms(jnp.concatenate([ya, yb], axis=1), nw_ref[...])


def _combine(dest3, gates, x2, yrows, final_w):
    T, D = x2.shape
    half = yrows.shape[1]
    tt = min(COMB_TT, T)
    n = T // tt
    dspec = lambda f: pl.BlockSpec((None, 1, tt * TOP_K), lambda i: (f(i), 0, 0), memory_space=pltpu.SMEM)
    return pl.pallas_call(
        _combine_kernel,
        out_shape=jax.ShapeDtypeStruct((T, D), F32),
        grid=(n,),
        in_specs=[
            dspec(lambda i: i), dspec(lambda i: jnp.minimum(i + 1, n - 1)),
            pl.BlockSpec((tt, LANES_V7X), lambda i: (i, 0)),
            pl.BlockSpec((tt, D), lambda i: (i, 0)),
            pl.BlockSpec(memory_space=pl.ANY),
            pl.BlockSpec((1, D), lambda i: (0, 0)),
        ],
        out_specs=pl.BlockSpec((tt, D), lambda i: (i, 0)),
        scratch_shapes=[pltpu.VMEM((2, TOP_K, tt, half), U32), pltpu.SemaphoreType.DMA((2,))],
        compiler_params=pltpu.CompilerParams(
            dimension_semantics=("arbitrary",),
            vmem_limit_bytes=_vmem_limit(2 * TOP_K * tt * half * 4 + 8 * tt * D * 4 + (8 << 20))),
        name="moe_combine",
    )(dest3, dest3, gates, x2, yrows, final_w)


def _route(top_e, n_vis):
    flat_e = top_e.reshape(-1)
    onehot = (flat_e[:, None] == jnp.arange(N_EXPERTS, dtype=I32)[None, :]).astype(I32)
    incl = jnp.cumsum(onehot, axis=0)
    counts = incl[-1]
    pos = jnp.sum((incl - onehot) * onehot, axis=1)
    nblk = (counts + MOE_BLK - 1) // MOE_BLK
    nvis = (nblk + MOE_G - 1) // MOE_G
    bpv = (nblk + jnp.maximum(nvis, 1) - 1) // jnp.maximum(nvis, 1)
    rpv = bpv * MOE_BLK
    vend = jnp.cumsum(nvis)
    vstart = vend - nvis
    rpv_a = jnp.maximum(jnp.sum(onehot * rpv[None, :], axis=1), 1)
    vs_a = jnp.sum(onehot * vstart[None, :], axis=1)
    dest = (vs_a + pos // rpv_a) * MOE_R + pos % rpv_a

    n_used = vend[-1]
    v = jnp.minimum(jnp.arange(n_vis, dtype=I32), n_used - 1)[:, None]
    sel = ((vstart[None, :] <= v) & (v < vend[None, :])).astype(I32)
    visit_e = jnp.sum(sel * jnp.arange(N_EXPERTS, dtype=I32)[None, :], axis=1)
    local = v[:, 0] - jnp.sum(sel * vstart[None, :], axis=1)
    v_nblk = jnp.sum(sel * nblk[None, :], axis=1)
    v_bpv = jnp.sum(sel * bpv[None, :], axis=1)
    visit_nb = jnp.clip(v_nblk - local * v_bpv, 0, v_bpv)
    visit_nb = jnp.where(jnp.arange(n_vis) < n_used, visit_nb, 0)
    return dest.astype(I32), visit_e.astype(I32), visit_nb.astype(I32), n_used.astype(I32).reshape(1)


def kernel(x, mem, norm_mix_w, w_in, conv_w, conv_b, if_bias, m_head_norm_w, w_branch_m, w_branch_a, rel_bias, w_out, norm_x_w, norm_mem_w, wq_x, wkv_x, wo_x, norm_ffn_w, w_router, b_router, w_gu, b_gu, w_down, b_down, final_norm_w):
    B, S, D = x.shape
    T = B * S
    l = 0
    x2d = x.reshape(T, D)

    wi = w_in[l]
    o_qm, o_km, o_vm, o_om = 0, M_WIDTH, 2 * M_WIDTH, 3 * M_WIDTH
    o_if = 4 * M_WIDTH
    o_qa = o_if + 2 * M_HEADS
    o_ka, o_va = o_qa + A_WIDTH, o_qa + 2 * A_WIDTH
    o_gm = o_qa + 3 * A_WIDTH
    o_ga = o_gm + D
    seg = lambda o, w: wi[:, o:o + w]
    w_main = jnp.concatenate(
        [seg(o_gm, D), seg(o_ga, D), seg(o_qm, M_WIDTH), seg(o_km, M_WIDTH), seg(o_vm, M_WIDTH),
         seg(o_om, M_WIDTH), seg(o_qa, A_WIDTH), seg(o_ka, A_WIDTH), seg(o_va, A_WIDTH)], axis=1).astype(BF16)
    w_if = jnp.pad(seg(o_if, 2 * M_HEADS), ((0, 0), (0, LANES_V7X - 2 * M_HEADS))).astype(BF16)

    proj, if_rows = _inproj(x2d, norm_mix_w[l].reshape(1, D), w_main, w_if)
    p3 = proj.reshape(B, S, N_MAIN)

    nc = S // CHUNK
    gcol3 = if_rows.reshape(B, S, LANES_V7X)
    grow4 = if_rows[:, :2 * M_HEADS].reshape(B, nc, CHUNK, 2 * M_HEADS).transpose(0, 1, 3, 2)
    bias_col = jnp.pad(if_bias[l], (0, LANES_V7X - 2 * M_HEADS)).reshape(1, LANES_V7X)
    bias_row = jnp.broadcast_to(if_bias[l][:, None], (2 * M_HEADS, CHUNK))
    hm = _mlstm(p3, gcol3, grow4, conv_w[l], conv_b[l].reshape(1, 2 * M_WIDTH), bias_col, bias_row,
                m_head_norm_w[l].reshape(1, M_WIDTH))

    ya = _band_attn(p3, _band_bias(rel_bias[l]))

    x1 = _merge(hm.reshape(T, M_WIDTH), ya.reshape(T, A_WIDTH), proj, x2d,
                w_branch_m[l].astype(BF16), w_branch_a[l].astype(BF16), w_out[l].astype(BF16))

    km, vm = _memkv(mem, norm_mem_w[l].reshape(1, D), wkv_x[l].astype(BF16))
    wr = jnp.pad(w_router[l], ((0, 0), (0, LANES_V7X - N_EXPERTS)))
    wr_hi = wr.astype(BF16)
    wr_lo = (wr - wr_hi.astype(F32)).astype(BF16)
    br = jnp.concatenate([b_router[l], jnp.full((LANES_V7X - N_EXPERTS,), NEG_BIG, F32)]).reshape(1, LANES_V7X)
    x2, h3p, te, tg = _xattn(x1, S, norm_x_w[l].reshape(1, D), wq_x[l].astype(BF16), km, vm,
                             wo_x[l].astype(BF16), norm_ffn_w[l].reshape(1, D), wr_hi, wr_lo, br)

    A = T * TOP_K
    n_vis = (A // MOE_BLK + N_EXPERTS + N_EXPERTS * (MOE_G - 1)) // MOE_G
    dest, visit_e, visit_nb, n_used = _route(te[:, :TOP_K], n_vis)
    xs = _dispatch(visit_nb, dest.reshape(T // min(DISP_TT, T), 1, -1), h3p, n_vis * MOE_R)
    yrows = _experts(visit_e, visit_nb, n_used, xs, w_gu[l], b_gu[l].reshape(N_EXPERTS, 1, 2 * D_FF),
                     w_down[l], b_down[l].reshape(N_EXPERTS, 1, D))
    out = _combine(dest.reshape(T // min(COMB_TT, T), 1, -1), tg, x2, yrows, final_norm_w.reshape(1, D))
    return out.reshape(B, S, D)
```

```python
import functools

import jax
import jax.numpy as jnp
from jax import lax
from jax.experimental import pallas as pl
from jax.experimental.pallas import tpu as pltpu

F32 = jnp.float32
BF16 = jnp.bfloat16
I32 = jnp.int32
U32 = jnp.uint32

D_MODEL = 2048
CHUNK = 64
EPS = 1e-6
M_HEADS = 4
M_HEAD_DIM = 256
M_WIDTH = M_HEADS * M_HEAD_DIM
CONV_WIDTH = 4
A_HEADS = 16
A_HEAD_DIM = 64
A_WIDTH = A_HEADS * A_HEAD_DIM
LEFT_CHUNKS = 8
MAX_REL = 128
X_HEADS = 4
X_HEAD_DIM = 128
X_WIDTH = X_HEADS * X_HEAD_DIM
N_EXPERTS = 32
TOP_K = 4
D_FF = D_MODEL
SWIGLU_LIMIT = 7.0
SWIGLU_ALPHA = 1.702

LANES_V7X = 128
SUBLANES_V7X = 8
VMEM_BYTES_V7X = 64 * 1024 * 1024
VMEM_LIMIT_CAP = 60000 * 1024

NEG_BIG = -1e30

COL_GM = 0
COL_GA = 2048
COL_QM = 4096
COL_KM = 5120
COL_VM = 6144
COL_OM = 7168
COL_QA = 8192
COL_KA = 9216
COL_VA = 10240
N_MAIN = 11264

IN_TM = 1024
IN_TN = 1024
ATT_TQ = 512
ATT_SUB = 128
ATT_WIN = ATT_SUB + LEFT_CHUNKS * CHUNK
MERGE_TM = 512
XATT_TM = 512
MOE_BLK = 256
MOE_G = 5
MOE_R = MOE_G * MOE_BLK
MOE_BIG = 4
MOE_TF = 256
MOE_TT = 256


def _vmem_limit(nbytes):
    return int(min(VMEM_LIMIT_CAP, max(nbytes, 16 * 1024 * 1024)))


def _nt_dot(a, b):
    return lax.dot_general(a, b, (((1,), (1,)), ((), ())), preferred_element_type=F32)


def _tn_dot(a, b):
    return lax.dot_general(a, b, (((0,), (0,)), ((), ())), preferred_element_type=F32)


def _sigmoid(x):
    return 1.0 / (1.0 + jnp.exp(-x))


def _log_sigmoid(x):
    return jnp.minimum(x, 0.0) - jnp.log1p(jnp.exp(-jnp.abs(x)))


def _rms(x, w):
    return x * lax.rsqrt(jnp.mean(x * x, axis=-1, keepdims=True) + EPS) * w


def _pack_bf16_pairs(xb):
    n = xb.shape[1] // 2
    hi = lax.bitcast_convert_type(xb[:, :n].astype(F32), U32)
    lo = lax.bitcast_convert_type(xb[:, n:].astype(F32), U32)
    return hi | (lo >> 16)


def _unpack_bf16_pairs(w):
    hi = lax.bitcast_convert_type(w & jnp.uint32(0xFFFF0000), F32)
    lo = lax.bitcast_convert_type(w << 16, F32)
    return hi, lo


TOK_SUB = SUBLANES_V7X
TOK_WORDS = TOK_SUB * LANES_V7X


def _store_token_tiles(ref, tok0, packed):
    n = packed.shape[0]
    for s in range(TOK_SUB):
        ref[pl.ds(tok0 * TOK_SUB + s, n, stride=TOK_SUB), :] = packed[:, s * LANES_V7X:(s + 1) * LANES_V7X]


def _load_token_slab(ref, tok0, n, s):
    return ref[pl.ds(tok0 * TOK_SUB + s, n, stride=TOK_SUB), :]


def _inproj_kernel(x_ref, nw_ref, w_ref, wif_ref, o_ref, if_ref, h_scr):
    @pl.when(pl.program_id(1) == 0)
    def _():
        hb = _rms(x_ref[...], nw_ref[...]).astype(BF16)
        h_scr[...] = hb
        if_ref[...] = jnp.dot(hb, wif_ref[...], preferred_element_type=F32)

    o_ref[...] = jnp.dot(h_scr[...], w_ref[...], preferred_element_type=F32).astype(o_ref.dtype)


def _inproj(x2d, norm_w, w_main, w_if):
    T, D = x2d.shape
    tm = min(IN_TM, T)
    grid = (T // tm, N_MAIN // IN_TN)
    vmem = 2 * tm * D * 4 + tm * D * 2 + 2 * D * IN_TN * 2 + 2 * tm * IN_TN * 2 + 4 * tm * LANES_V7X * 4
    return pl.pallas_call(
        _inproj_kernel,
        out_shape=(jax.ShapeDtypeStruct((T, N_MAIN), BF16), jax.ShapeDtypeStruct((T, LANES_V7X), F32)),
        grid=grid,
        in_specs=[
            pl.BlockSpec((tm, D), lambda i, j: (i, 0)),
            pl.BlockSpec((1, D), lambda i, j: (0, 0)),
            pl.BlockSpec((D, IN_TN), lambda i, j: (0, j)),
            pl.BlockSpec((D, LANES_V7X), lambda i, j: (0, 0)),
        ],
        out_specs=(
            pl.BlockSpec((tm, IN_TN), lambda i, j: (i, j)),
            pl.BlockSpec((tm, LANES_V7X), lambda i, j: (i, 0)),
        ),
        scratch_shapes=[pltpu.VMEM((tm, D), BF16)],
        compiler_params=pltpu.CompilerParams(
            dimension_semantics=("parallel", "arbitrary"),
            vmem_limit_bytes=_vmem_limit(vmem + (8 << 20))),
        name="inproj",
    )(x2d, norm_w, w_main, w_if)


def _mlstm_kernel(q_ref, k_ref, v_ref, o_ref, gc_ref, gr_ref, cw_ref, cb_ref, bc_ref, br_ref, nw_ref,
                  out_ref, tail, c_st, n_st, m_st):
    nb = q_ref.shape[0]
    L = CHUNK
    dh = M_HEAD_DIM
    halo = tail.shape[1]

    @pl.when(pl.program_id(0) == 0)
    def _():
        tail[...] = jnp.zeros_like(tail)
        c_st[...] = jnp.zeros_like(c_st)
        n_st[...] = jnp.zeros_like(n_st)
        m_st[...] = jnp.zeros_like(m_st)

    row = lax.broadcasted_iota(I32, (L, L), 0)
    col = lax.broadcasted_iota(I32, (L, L), 1)
    lower = col <= row

    srow = lax.broadcasted_iota(I32, (L, halo + L), 0)
    scol = lax.broadcasted_iota(I32, (L, halo + L), 1)
    shift_mat = jnp.concatenate(
        [jnp.where(scol == srow + (halo - (CONV_WIDTH - 1) + j), 1.0, 0.0) for j in range(CONV_WIDTH - 1)],
        axis=0).astype(BF16)

    for b in range(nb):
        qk = []
        for half, ref in enumerate((q_ref, k_ref)):
            cols = slice(half * M_WIDTH, (half + 1) * M_WIDTH)
            cur = ref[b]
            ext = jnp.concatenate([tail[b, :, cols], cur], axis=0)
            back = jnp.dot(shift_mat, ext, preferred_element_type=F32)
            conv = cb_ref[:, cols] + cur.astype(F32) * cw_ref[CONV_WIDTH - 1:CONV_WIDTH, cols]
            for j in range(CONV_WIDTH - 1):
                conv = conv + back[j * L:(j + 1) * L] * cw_ref[j:j + 1, cols]
            qk.append(conv * _sigmoid(conv))
            tail[b, :, cols] = cur[L - halo:L]

        gcol = gc_ref[b] + bc_ref[...]
        grow = gr_ref[b, 0] + br_ref[...]

        for h in range(M_HEADS):
            s = b * M_HEADS + h
            lo, hi = h * dh, (h + 1) * dh
            q = qk[0][:, lo:hi]
            k = qk[1][:, lo:hi] * (dh ** -0.5)
            vb = v_ref[b, :, lo:hi]
            qb = q.astype(BF16)
            kb = k.astype(BF16)

            ig_c = gcol[:, h:h + 1]
            ig_r = grow[h:h + 1, :]
            ls_c = _log_sigmoid(gcol[:, M_HEADS + h:M_HEADS + h + 1])
            ls_r = _log_sigmoid(grow[M_HEADS + h:M_HEADS + h + 1, :])
            b_c = jnp.sum(jnp.where(lower, ls_r, 0.0), axis=1, keepdims=True)
            b_r = jnp.sum(jnp.where(row <= col, ls_c, 0.0), axis=0, keepdims=True)
            g = jnp.sum(ls_r, axis=1, keepdims=True)
            m_old = m_st[s][:, 0:1]

            w_end = g - b_c + ig_c
            a = jnp.max(w_end, axis=0, keepdims=True)
            e_end = jnp.exp(w_end - a)

            dlog = jnp.where(lower, b_c - b_r + ig_r, NEG_BIG)
            inter = b_c + m_old
            m_j = jnp.maximum(inter, jnp.max(dlog, axis=1, keepdims=True))
            sm = _nt_dot(qb, kb) * jnp.exp(dlog - m_j)
            w_int = jnp.exp(inter - m_j)

            c_old = c_st[s]
            n_old = n_st[s]
            num = (jnp.dot(sm.astype(BF16), vb, preferred_element_type=F32)
                   + w_int * _nt_dot(qb, c_old.astype(BF16)))
            den = (jnp.sum(sm, axis=1, keepdims=True)
                   + w_int * jnp.sum(q * n_old, axis=1, keepdims=True))
            hh = num / jnp.maximum(jnp.abs(den), jnp.exp(-m_j))

            hn = _rms(hh, nw_ref[:, lo:hi])
            og = _sigmoid(o_ref[b, :, lo:hi].astype(F32))
            out_ref[b, :, lo:hi] = (og * hn).astype(out_ref.dtype)

            m_new = jnp.maximum(g + m_old, a)
            s_prev = jnp.exp(g + m_old - m_new)
            s_loc = jnp.exp(a - m_new)
            ve = (vb.astype(F32) * (e_end * s_loc)).astype(BF16)
            c_st[s] = s_prev * c_old + _tn_dot(ve, kb)
            n_st[s] = s_prev * n_old + s_loc * jnp.sum(e_end * k, axis=0, keepdims=True)
            m_st[s] = jnp.broadcast_to(m_new, (1, LANES_V7X))


def _mlstm(p3, gcol3, grow4, conv_w, conv_b, bias_col, bias_row, head_w):
    B, S, _ = p3.shape
    nc = S // CHUNK
    ns = B * M_HEADS
    blk = lambda cb: pl.BlockSpec((B, CHUNK, M_WIDTH), lambda c, cb=cb: (0, c, cb))
    full = lambda shape: pl.BlockSpec(shape, lambda c: (0,) * len(shape))
    vmem = (2 * 5 * B * CHUNK * M_WIDTH * 2 + B * (CHUNK + 8) * 2 * M_WIDTH * 4
            + ns * M_HEAD_DIM * M_HEAD_DIM * 4 + (16 << 20))
    return pl.pallas_call(
        _mlstm_kernel,
        out_shape=jax.ShapeDtypeStruct((B, S, M_WIDTH), BF16),
        grid=(nc,),
        in_specs=[
            blk(COL_QM // M_WIDTH), blk(COL_KM // M_WIDTH), blk(COL_VM // M_WIDTH), blk(COL_OM // M_WIDTH),
            pl.BlockSpec((B, CHUNK, LANES_V7X), lambda c: (0, c, 0)),
            pl.BlockSpec((B, 1, 2 * M_HEADS, CHUNK), lambda c: (0, c, 0, 0)),
            full((CONV_WIDTH, 2 * M_WIDTH)), full((1, 2 * M_WIDTH)),
            full((1, LANES_V7X)), full((2 * M_HEADS, CHUNK)), full((1, M_WIDTH)),
        ],
        out_specs=pl.BlockSpec((B, CHUNK, M_WIDTH), lambda c: (0, c, 0)),
        scratch_shapes=[
            pltpu.VMEM((B, 2 * SUBLANES_V7X, 2 * M_WIDTH), BF16),
            pltpu.VMEM((ns, M_HEAD_DIM, M_HEAD_DIM), F32),
            pltpu.VMEM((ns, 1, M_HEAD_DIM), F32),
            pltpu.VMEM((ns, 1, LANES_V7X), F32),
        ],
        compiler_params=pltpu.CompilerParams(
            dimension_semantics=("arbitrary",), vmem_limit_bytes=_vmem_limit(vmem)),
        name="mlstm",
    )(p3, p3, p3, p3, gcol3, grow4, conv_w, conv_b, bias_col, bias_row, head_w)


def _band_attn_kernel(q_ref, kp_ref, kc_ref, vp_ref, vc_ref, bias_ref, out_ref):
    qb_idx = pl.program_id(2)
    tq = q_ref.shape[1]
    q2 = q_ref[0]
    kcat = jnp.concatenate([kp_ref[0], kc_ref[0]], axis=0)
    vcat = jnp.concatenate([vp_ref[0], vc_ref[0]], axis=0)
    lane = lax.broadcasted_iota(I32, (1, LANES_V7X), 1)
    wcol = lax.broadcasted_iota(I32, (1, ATT_WIN), 1)
    first_pen = jnp.where(qb_idx == 0, NEG_BIG, 0.0)
    scale = A_HEAD_DIM ** -0.5
    bias = bias_ref[0]

    for t in range(tq // ATT_SUB):
        r0 = t * ATT_SUB
        w0 = tq + r0 - LEFT_CHUNKS * CHUNK
        kw = kcat[w0:w0 + ATT_WIN]
        vw = vcat[w0:w0 + ATT_WIN]
        pen = jnp.where(wcol < (tq - w0), first_pen, 0.0)
        qs = q2[r0:r0 + ATT_SUB] * scale
        zero = jnp.zeros_like(qs)
        qst = jnp.concatenate([jnp.where(lane < A_HEAD_DIM, qs, zero),
                               jnp.where(lane >= A_HEAD_DIM, qs, zero)], axis=0)
        sc = _nt_dot(qst, kw) + bias + pen
        mx = jnp.max(sc, axis=1, keepdims=True)
        p = jnp.exp(sc - mx)
        den = jnp.sum(p, axis=1, keepdims=True)
        o = jnp.dot(p.astype(BF16), vw, preferred_element_type=F32) / den
        out_ref[0, r0:r0 + ATT_SUB, :] = jnp.where(
            lane < A_HEAD_DIM, o[:ATT_SUB], o[ATT_SUB:]).astype(out_ref.dtype)


def _band_attn(p3, bias):
    B, S, _ = p3.shape
    tq = ATT_TQ
    nq = S // tq
    nhp = A_HEADS // 2
    cq, ck, cv = COL_QA // LANES_V7X, COL_KA // LANES_V7X, COL_VA // LANES_V7X
    cur = lambda c0: pl.BlockSpec((1, tq, LANES_V7X), lambda b, hp, i, c0=c0: (b, i, c0 + hp))
    prev = lambda c0: pl.BlockSpec((1, tq, LANES_V7X), lambda b, hp, i, c0=c0: (b, jnp.maximum(i - 1, 0), c0 + hp))
    return pl.pallas_call(
        _band_attn_kernel,
        out_shape=jax.ShapeDtypeStruct((B, S, A_WIDTH), BF16),
        grid=(B, nhp, nq),
        in_specs=[cur(cq), prev(ck), cur(ck), prev(cv), cur(cv),
                  pl.BlockSpec((1, 2 * ATT_SUB, ATT_WIN), lambda b, hp, i: (hp, 0, 0))],
        out_specs=pl.BlockSpec((1, tq, LANES_V7X), lambda b, hp, i: (b, i, hp)),
        compiler_params=pltpu.CompilerParams(
            dimension_semantics=("parallel", "parallel", "arbitrary"),
            vmem_limit_bytes=_vmem_limit(32 << 20)),
        name="band_attn",
    )(p3, p3, p3, p3, p3, bias)


def _band_bias(rel_table):
    n = 6 * LANES_V7X
    c = jnp.arange(n)
    d = jnp.where(c < ATT_WIN, c, c - n)
    rel = jnp.clip(LEFT_CHUNKS * CHUNK - d, -MAX_REL, MAX_REL) + MAX_REL
    vec = rel_table[:, rel].astype(F32)
    sheared = jnp.tile(vec, (1, ATT_SUB))[:, :ATT_SUB * (n - 1)].reshape(A_HEADS, ATT_SUB, n - 1)
    i = jnp.arange(ATT_SUB)[:, None]
    kp = jnp.arange(ATT_WIN)[None, :]
    ci, kc = i // CHUNK, kp // CHUNK
    valid = (kc >= ci) & (kc <= ci + LEFT_CHUNKS)
    bias = jnp.where(valid[None], sheared[:, :, :ATT_WIN], NEG_BIG)
    return bias.reshape(A_HEADS // 2, 2 * ATT_SUB, ATT_WIN)


def _merge_kernel(hm_ref, ya_ref, gm_ref, ga_ref, x_ref, wbm_ref, wba_ref, wo_ref, out_ref):
    ym = jnp.dot(hm_ref[...], wbm_ref[...], preferred_element_type=F32)
    ya = jnp.dot(ya_ref[...], wba_ref[...], preferred_element_type=F32)
    merged = _sigmoid(gm_ref[...].astype(F32)) * ym + _sigmoid(ga_ref[...].astype(F32)) * ya
    out_ref[...] = x_ref[...] + jnp.dot(merged.astype(BF16), wo_ref[...], preferred_element_type=F32)


def _const_spec(shape):
    return pl.BlockSpec(shape, lambda i: (0,) * len(shape), pipeline_mode=pl.Buffered(1))


def _merge(hm2d, ya2d, p2d, x2d, wbm, wba, wout):
    T, D = x2d.shape
    tm = min(MERGE_TM, T)
    vmem = ((2 * M_WIDTH * D + D * D) * 2 + 2 * (2 * tm * M_WIDTH * 2 + 2 * tm * D * 2 + 2 * tm * D * 4)
            + 6 * tm * D * 4)
    return pl.pallas_call(
        _merge_kernel,
        out_shape=jax.ShapeDtypeStruct((T, D), F32),
        grid=(T // tm,),
        in_specs=[
            pl.BlockSpec((tm, M_WIDTH), lambda i: (i, 0)),
            pl.BlockSpec((tm, A_WIDTH), lambda i: (i, 0)),
            pl.BlockSpec((tm, D), lambda i: (i, COL_GM // D)),
            pl.BlockSpec((tm, D), lambda i: (i, COL_GA // D)),
            pl.BlockSpec((tm, D), lambda i: (i, 0)),
            _const_spec((M_WIDTH, D)), _const_spec((A_WIDTH, D)), _const_spec((D, D)),
        ],
        out_specs=pl.BlockSpec((tm, D), lambda i: (i, 0)),
        compiler_params=pltpu.CompilerParams(
            dimension_semantics=("parallel",), vmem_limit_bytes=_vmem_limit(vmem)),
        name="merge_out",
    )(hm2d, ya2d, p2d, p2d, x2d, wbm, wba, wout)


def _memkv_kernel(mem_ref, nw_ref, w_ref, k_ref, v_ref):
    hb = _rms(mem_ref[0], nw_ref[...]).astype(BF16)
    kv = jnp.dot(hb, w_ref[...], preferred_element_type=F32)
    k_ref[0] = kv[:, :X_WIDTH].astype(k_ref.dtype)
    v_ref[0] = kv[:, X_WIDTH:].astype(v_ref.dtype)


def _memkv(mem, norm_w, wkv):
    B, N, D = mem.shape
    return pl.pallas_call(
        _memkv_kernel,
        out_shape=(jax.ShapeDtypeStruct((B, N, X_WIDTH), BF16), jax.ShapeDtypeStruct((B, N, X_WIDTH), BF16)),
        grid=(B,),
        in_specs=[pl.BlockSpec((1, N, D), lambda b: (b, 0, 0)),
                  pl.BlockSpec((1, D), lambda b: (0, 0)),
                  pl.BlockSpec((D, 2 * X_WIDTH), lambda b: (0, 0))],
        out_specs=(pl.BlockSpec((1, N, X_WIDTH), lambda b: (b, 0, 0)),
                   pl.BlockSpec((1, N, X_WIDTH), lambda b: (b, 0, 0))),
        compiler_params=pltpu.CompilerParams(
            dimension_semantics=("parallel",), vmem_limit_bytes=_vmem_limit(24 << 20)),
        name="mem_kv",
    )(mem, norm_w, wkv)


def _xattn_kernel(x_ref, nxw_ref, wq_ref, km_ref, vm_ref, wo_ref, nfw_ref, wrh_ref, wrl_ref, br_ref,
                  x2_ref, h3_ref, te_ref, tg_ref):
    x1 = x_ref[...]
    hb = _rms(x1, nxw_ref[...]).astype(BF16)
    q = jnp.dot(hb, wq_ref[...], preferred_element_type=F32).astype(BF16)
    km = km_ref[0]
    vm = vm_ref[0]
    outs = []
    for h in range(X_HEADS):
        lo, hi = h * X_HEAD_DIM, (h + 1) * X_HEAD_DIM
        sc = _nt_dot(q[:, lo:hi], km[:, lo:hi]) * (X_HEAD_DIM ** -0.5)
        mx = jnp.max(sc, axis=1, keepdims=True)
        p = jnp.exp(sc - mx)
        den = jnp.sum(p, axis=1, keepdims=True)
        outs.append(jnp.dot(p.astype(BF16), vm[:, lo:hi], preferred_element_type=F32) / den)
    o = jnp.concatenate(outs, axis=1).astype(BF16)
    x2 = x1 + jnp.dot(o, wo_ref[...], preferred_element_type=F32)
    x2_ref[...] = x2

    h3 = _rms(x2, nfw_ref[...])
    h_hi = h3.astype(BF16)
    _store_token_tiles(h3_ref, 0, _pack_bf16_pairs(h_hi))
    h_lo = (h3 - h_hi.astype(F32)).astype(BF16)
    logits = (jnp.dot(h_hi, wrh_ref[...], preferred_element_type=F32)
              + jnp.dot(h_lo, wrh_ref[...], preferred_element_type=F32)
              + jnp.dot(h_hi, wrl_ref[...], preferred_element_type=F32)
              + br_ref[...])

    colf = lax.broadcasted_iota(I32, logits.shape, 1).astype(F32)
    work = logits
    vals, idxs = [], []
    for _ in range(TOP_K):
        mx = jnp.max(work, axis=1, keepdims=True)
        ix = jnp.min(jnp.where(work == mx, colf, float(LANES_V7X)), axis=1, keepdims=True)
        vals.append(mx)
        idxs.append(ix)
        work = jnp.where(colf == ix, -jnp.inf, work)
    es = [jnp.exp(v - vals[0]) for v in vals]
    tot = es[0] + es[1] + es[2] + es[3]
    te = jnp.zeros_like(logits)
    tg = jnp.zeros_like(logits)
    for kk in range(TOP_K):
        te = jnp.where(colf == float(kk), idxs[kk], te)
        tg = jnp.where(colf == float(kk), es[kk] / tot, tg)
    te_ref[...] = te.astype(I32)
    tg_ref[...] = tg


def _xattn(x1, batch_rows, norm_x_w, wq, km, vm, wo, norm_ffn_w, wr_hi, wr_lo, br):
    T, D = x1.shape
    tm = min(XATT_TM, batch_rows)
    per_b = batch_rows // tm
    n_mem = km.shape[1]
    row = lambda w: pl.BlockSpec((tm, w), lambda i: (i, 0))
    vmem = (2 * D * X_WIDTH * 2 + 2 * D * LANES_V7X * 2 + 4 * n_mem * X_WIDTH * 2
            + 2 * 3 * tm * D * 4 + 8 * tm * D * 4)
    return pl.pallas_call(
        _xattn_kernel,
        out_shape=(jax.ShapeDtypeStruct((T, D), F32), jax.ShapeDtypeStruct((T * TOK_SUB, LANES_V7X), U32),
                   jax.ShapeDtypeStruct((T, LANES_V7X), I32), jax.ShapeDtypeStruct((T, LANES_V7X), F32)),
        grid=(T // tm,),
        in_specs=[
            row(D), _const_spec((1, D)), _const_spec((D, X_WIDTH)),
            pl.BlockSpec((1, n_mem, X_WIDTH), lambda i: (i // per_b, 0, 0)),
            pl.BlockSpec((1, n_mem, X_WIDTH), lambda i: (i // per_b, 0, 0)),
            _const_spec((X_WIDTH, D)), _const_spec((1, D)),
            _const_spec((D, LANES_V7X)), _const_spec((D, LANES_V7X)), _const_spec((1, LANES_V7X)),
        ],
        out_specs=(row(D), pl.BlockSpec((tm * TOK_SUB, LANES_V7X), lambda i: (i, 0)),
                   row(LANES_V7X), row(LANES_V7X)),
        compiler_params=pltpu.CompilerParams(
            dimension_semantics=("parallel",), vmem_limit_bytes=_vmem_limit(vmem)),
        name="xattn_router",
    )(x1, norm_x_w, wq, km, vm, wo, norm_ffn_w, wr_hi, wr_lo, br)


def _dispatch_kernel(nb_ref, dest_ref, h_ref, xs_ref, zero_scr, sem, zsem):
    tt = h_ref.shape[0] // TOK_SUB
    nv = nb_ref.shape[0]

    @pl.when(pl.program_id(0) == 0)
    def _():
        zero_scr[...] = jnp.zeros_like(zero_scr)

        def zero_copy(v, b):
            row0 = pl.multiple_of((v * MOE_R + b * MOE_BLK) * TOK_SUB, MOE_BLK * TOK_SUB)
            return pltpu.make_async_copy(zero_scr, xs_ref.at[pl.ds(row0, MOE_BLK * TOK_SUB)], zsem)

        def each_partial_block(v, fn):
            for b in range(MOE_G):
                @pl.when(b >= nb_ref[v] - 1)
                def _():
                    fn(zero_copy(v, b))

        def zstart(v, carry):
            each_partial_block(v, lambda c: c.start())
            return carry

        def zwait(v, carry):
            each_partial_block(v, lambda c: c.wait())
            return carry

        lax.fori_loop(0, nv, zstart, 0)
        lax.fori_loop(0, nv, zwait, 0)

    def tile_copy(t, kk):
        src = h_ref.at[pl.ds(pl.multiple_of(t * TOK_SUB, TOK_SUB), TOK_SUB)]
        dst = xs_ref.at[pl.ds(pl.multiple_of(dest_ref[0, t * TOP_K + kk], TOK_SUB), TOK_SUB)]
        return pltpu.make_async_copy(src, dst, sem)

    def start(t, carry):
        for kk in range(TOP_K):
            tile_copy(t, kk).start()
        return carry

    def wait(t, carry):
        for kk in range(TOP_K):
            tile_copy(t, kk).wait()
        return carry

    lax.fori_loop(0, tt, start, 0)
    lax.fori_loop(0, tt, wait, 0)


def _dispatch(visit_nb, dest3, h3p, n_rows):
    T = h3p.shape[0] // TOK_SUB
    tt = min(MOE_TT, T)
    grid_spec = pltpu.PrefetchScalarGridSpec(
        num_scalar_prefetch=1,
        grid=(T // tt,),
        in_specs=[
            pl.BlockSpec((None, 1, tt * TOP_K), lambda i, nb: (i, 0, 0), memory_space=pltpu.SMEM),
            pl.BlockSpec((tt * TOK_SUB, LANES_V7X), lambda i, nb: (i, 0)),
        ],
        out_specs=pl.BlockSpec(memory_space=pl.ANY),
        scratch_shapes=[pltpu.VMEM((MOE_BLK * TOK_SUB, LANES_V7X), U32),
                        pltpu.SemaphoreType.DMA(()), pltpu.SemaphoreType.DMA(())],
    )
    return pl.pallas_call(
        _dispatch_kernel,
        out_shape=jax.ShapeDtypeStruct((n_rows * TOK_SUB, LANES_V7X), U32),
        grid_spec=grid_spec,
        compiler_params=pltpu.CompilerParams(
            dimension_semantics=("arbitrary",), has_side_effects=True,
            vmem_limit_bytes=_vmem_limit(4 * tt * TOK_WORDS * 4 + MOE_BLK * TOK_WORDS * 4 + (8 << 20))),
        name="moe_dispatch",
    )(visit_nb, dest3, h3p)


def _experts_kernel(ve_ref, nb_ref, nu_ref, x_ref, wg_ref, wu_ref, bg_ref, bu_ref, wd_ref, bd_ref, out_ref,
                    xb_scr, acc_scr, wgb_scr, wub_scr, wdb_scr):
    del ve_ref, nu_ref
    v = pl.program_id(0)
    j = pl.program_id(1)
    last = pl.num_programs(1) - 1
    nb = nb_ref[v]
    half = TOK_WORDS

    def block_rows(b):
        return pl.ds(pl.multiple_of(b * MOE_BLK, MOE_BLK), MOE_BLK)

    @pl.when(nb > 0)
    def _():
        @pl.when(j == 0)
        def _():
            def stage(b, carry):
                rows = block_rows(b)
                for s in range(TOK_SUB):
                    hi, lo = _unpack_bf16_pairs(_load_token_slab(x_ref, b * MOE_BLK, MOE_BLK, s))
                    xb_scr[rows, s * LANES_V7X:(s + 1) * LANES_V7X] = hi.astype(BF16)
                    xb_scr[rows, half + s * LANES_V7X:half + (s + 1) * LANES_V7X] = lo.astype(BF16)
                acc_scr[rows, :] = jnp.zeros((MOE_BLK, acc_scr.shape[1]), F32)
                return carry
            lax.fori_loop(0, nb, stage, 0)

        def ffn_rows(rows, wg, wu, wd):
            xb = xb_scr[rows, :]
            g = jnp.dot(xb, wg, preferred_element_type=F32) + bg_ref[...]
            u = jnp.dot(xb, wu, preferred_element_type=F32) + bu_ref[...]
            g = jnp.minimum(g, SWIGLU_LIMIT)
            u = jnp.clip(u, -SWIGLU_LIMIT, SWIGLU_LIMIT)
            act = (u + 1.0) * (g * _sigmoid(SWIGLU_ALPHA * g))
            acc_scr[rows, :] += jnp.dot(act.astype(BF16), wd, preferred_element_type=F32)

        def cast_weights():
            wg, wu, wd = wg_ref[...].astype(BF16), wu_ref[...].astype(BF16), wd_ref[...].astype(BF16)
            wgb_scr[...] = wg
            wub_scr[...] = wu
            wdb_scr[...] = wd
            return wg, wu, wd

        big = nb >= MOE_BIG

        @pl.when(big)
        def _():
            ffn_rows(pl.ds(0, MOE_BIG * MOE_BLK), *cast_weights())

        @pl.when(jnp.logical_not(big))
        def _():
            cast_weights()

        def block(b, carry):
            ffn_rows(block_rows(b), wgb_scr[...], wub_scr[...], wdb_scr[...])
            return carry
        lax.fori_loop(jnp.where(big, MOE_BIG, 0), nb, block, 0)

        @pl.when(j == last)
        def _():
            def finish(b, carry):
                y = (acc_scr[block_rows(b), :] + bd_ref[...]).astype(BF16)
                _store_token_tiles(out_ref, b * MOE_BLK, _pack_bf16_pairs(y))
                return carry
            lax.fori_loop(0, nb, finish, 0)

            def blank(b, carry):
                rows = pl.ds(pl.multiple_of(b * MOE_BLK * TOK_SUB, MOE_BLK * TOK_SUB), MOE_BLK * TOK_SUB)
                out_ref[rows, :] = jnp.zeros((MOE_BLK * TOK_SUB, LANES_V7X), U32)
                return carry
            lax.fori_loop(nb, MOE_G, blank, 0)

    @pl.when((nb == 0) & (j == last))
    def _():
        out_ref[...] = jnp.zeros_like(out_ref)


def _experts(visit_e, visit_nb, n_used, xs, wgu, bgu, wdn, bdn):
    n_rows = xs.shape[0] // TOK_SUB
    half = TOK_WORDS
    D = 2 * half
    tf = MOE_TF
    nj = D_FF // tf
    nv = n_rows // MOE_R

    def frozen(v, j, nu):
        return jnp.where(v < nu[0], j, nj - 1)

    def vrow(v, nu):
        return jnp.minimum(v, nu[0] - 1)

    grid_spec = pltpu.PrefetchScalarGridSpec(
        num_scalar_prefetch=3,
        grid=(nv, nj),
        in_specs=[
            pl.BlockSpec((MOE_R * TOK_SUB, LANES_V7X), lambda v, j, ve, nb, nu: (vrow(v, nu), 0)),
            pl.BlockSpec((None, D, tf), lambda v, j, ve, nb, nu: (ve[v], 0, frozen(v, j, nu))),
            pl.BlockSpec((None, D, tf), lambda v, j, ve, nb, nu: (ve[v], 0, nj + frozen(v, j, nu))),
            pl.BlockSpec((None, 1, tf), lambda v, j, ve, nb, nu: (ve[v], 0, frozen(v, j, nu))),
            pl.BlockSpec((None, 1, tf), lambda v, j, ve, nb, nu: (ve[v], 0, nj + frozen(v, j, nu))),
            pl.BlockSpec((None, tf, D), lambda v, j, ve, nb, nu: (ve[v], frozen(v, j, nu), 0)),
            pl.BlockSpec((None, 1, D), lambda v, j, ve, nb, nu: (ve[v], 0, 0)),
        ],
        out_specs=pl.BlockSpec((MOE_R * TOK_SUB, LANES_V7X), lambda v, j, ve, nb, nu: (v, 0)),
        scratch_shapes=[pltpu.VMEM((MOE_R, D), BF16), pltpu.VMEM((MOE_R, D), F32),
                        pltpu.VMEM((D, tf), BF16), pltpu.VMEM((D, tf), BF16), pltpu.VMEM((tf, D), BF16)],
    )
    vmem = (2 * MOE_R * half * 4 + MOE_R * D * 2 + MOE_R * D * 4 + 2 * MOE_R * half * 4
            + 2 * 3 * D * tf * 4 + 3 * D * tf * 2 + 8 * MOE_BLK * D * 4)
    return pl.pallas_call(
        _experts_kernel,
        out_shape=jax.ShapeDtypeStruct((n_rows * TOK_SUB, LANES_V7X), U32),
        grid_spec=grid_spec,
        compiler_params=pltpu.CompilerParams(
            dimension_semantics=("arbitrary", "arbitrary"), vmem_limit_bytes=_vmem_limit(vmem)),
        name="moe_experts",
    )(visit_e, visit_nb, n_used, xs, wgu, wgu, bgu, bgu, wdn, bdn)


def _combine_kernel(dcur_ref, dnext_ref, gate_ref, x_ref, y_ref, nw_ref, out_ref, buf, y_scr, sems):
    i = pl.program_id(0)
    n = pl.num_programs(0)
    tt = x_ref.shape[0]
    half = TOK_WORDS
    slot = lax.rem(i, 2)

    def row_copy(dref, s, t, kk):
        src = y_ref.at[pl.ds(pl.multiple_of(dref[0, t * TOP_K + kk], TOK_SUB), TOK_SUB)]
        dst = buf.at[s, kk, pl.ds(pl.multiple_of(t * TOK_SUB, TOK_SUB), TOK_SUB)]
        return pltpu.make_async_copy(src, dst, sems.at[s])

    def issue(dref, s):
        def body(t, carry):
            for kk in range(TOP_K):
                row_copy(dref, s, t, kk).start()
            return carry
        lax.fori_loop(0, tt, body, 0)

    @pl.when(i == 0)
    def _():
        issue(dcur_ref, 0)

    @pl.when(i + 1 < n)
    def _():
        issue(dnext_ref, 1 - slot)

    def wait(t, carry):
        for kk in range(TOP_K):
            row_copy(dcur_ref, slot, t, kk).wait()
        return carry
    lax.fori_loop(0, tt, wait, 0)

    gate = gate_ref[...]
    gks = [jnp.broadcast_to(gate[:, kk:kk + 1], (tt, LANES_V7X)) for kk in range(TOP_K)]
    for s in range(TOK_SUB):
        ca = pl.ds(s * LANES_V7X, LANES_V7X)
        cb = pl.ds(half + s * LANES_V7X, LANES_V7X)
        ya = x_ref[:, ca]
        yb = x_ref[:, cb]
        for kk in range(TOP_K):
            hi, lo = _unpack_bf16_pairs(_load_token_slab(buf.at[slot, kk], 0, tt, s))
            ya = ya + gks[kk] * hi
            yb = yb + gks[kk] * lo
        y_scr[:, ca] = ya
        y_scr[:, cb] = yb
    out_ref[...] = _rms(y_scr[...], nw_ref[...])


def _combine(dest3, gates, x2, yrows, final_w):
    T, D = x2.shape
    half = TOK_WORDS
    tt = min(MOE_TT, T)
    n = T // tt
    dspec = lambda f: pl.BlockSpec((None, 1, tt * TOP_K), lambda i: (f(i), 0, 0), memory_space=pltpu.SMEM)
    return pl.pallas_call(
        _combine_kernel,
        out_shape=jax.ShapeDtypeStruct((T, D), F32),
        grid=(n,),
        in_specs=[
            dspec(lambda i: i), dspec(lambda i: jnp.minimum(i + 1, n - 1)),
            pl.BlockSpec((tt, LANES_V7X), lambda i: (i, 0)),
            pl.BlockSpec((tt, D), lambda i: (i, 0)),
            pl.BlockSpec(memory_space=pl.ANY),
            pl.BlockSpec((1, D), lambda i: (0, 0)),
        ],
        out_specs=pl.BlockSpec((tt, D), lambda i: (i, 0)),
        scratch_shapes=[pltpu.VMEM((2, TOP_K, tt * TOK_SUB, LANES_V7X), U32), pltpu.VMEM((tt, D), F32),
                        pltpu.SemaphoreType.DMA((2,))],
        compiler_params=pltpu.CompilerParams(
            dimension_semantics=("arbitrary",),
            vmem_limit_bytes=_vmem_limit(2 * TOP_K * tt * half * 4 + 8 * tt * D * 4 + (8 << 20))),
        name="moe_combine",
    )(dest3, dest3, gates, x2, yrows, final_w)


def _route(top_e, n_vis):
    flat_e = top_e.reshape(-1)
    onehot = (flat_e[:, None] == jnp.arange(N_EXPERTS, dtype=I32)[None, :]).astype(I32)
    incl = jnp.cumsum(onehot, axis=0)
    counts = incl[-1]
    pos = jnp.sum((incl - onehot) * onehot, axis=1)
    nblk = (counts + MOE_BLK - 1) // MOE_BLK
    nvis = (nblk + MOE_G - 1) // MOE_G
    bpv = (nblk + jnp.maximum(nvis, 1) - 1) // jnp.maximum(nvis, 1)
    rpv = bpv * MOE_BLK
    vend = jnp.cumsum(nvis)
    vstart = vend - nvis
    rpv_a = jnp.maximum(jnp.sum(onehot * rpv[None, :], axis=1), 1)
    vs_a = jnp.sum(onehot * vstart[None, :], axis=1)
    q = jnp.floor((pos.astype(F32) + 0.5) / rpv_a.astype(F32)).astype(I32)
    dest = (vs_a + q) * MOE_R + (pos - q * rpv_a)

    n_used = vend[-1]
    v = jnp.minimum(jnp.arange(n_vis, dtype=I32), n_used - 1)[:, None]
    sel = ((vstart[None, :] <= v) & (v < vend[None, :])).astype(I32)
    visit_e = jnp.sum(sel * jnp.arange(N_EXPERTS, dtype=I32)[None, :], axis=1)
    local = v[:, 0] - jnp.sum(sel * vstart[None, :], axis=1)
    v_nblk = jnp.sum(sel * nblk[None, :], axis=1)
    v_bpv = jnp.sum(sel * bpv[None, :], axis=1)
    visit_nb = jnp.clip(v_nblk - local * v_bpv, 0, v_bpv)
    visit_nb = jnp.where(jnp.arange(n_vis) < n_used, visit_nb, 0)
    return dest.astype(I32), visit_e.astype(I32), visit_nb.astype(I32), n_used.astype(I32).reshape(1)


def kernel(x, mem, norm_mix_w, w_in, conv_w, conv_b, if_bias, m_head_norm_w, w_branch_m, w_branch_a, rel_bias, w_out, norm_x_w, norm_mem_w, wq_x, wkv_x, wo_x, norm_ffn_w, w_router, b_router, w_gu, b_gu, w_down, b_down, final_norm_w):
    B, S, D = x.shape
    T = B * S
    l = 0
    x2d = x.reshape(T, D)

    wi = w_in[l]
    o_qm, o_km, o_vm, o_om = 0, M_WIDTH, 2 * M_WIDTH, 3 * M_WIDTH
    o_if = 4 * M_WIDTH
    o_qa = o_if + 2 * M_HEADS
    o_ka, o_va = o_qa + A_WIDTH, o_qa + 2 * A_WIDTH
    o_gm = o_qa + 3 * A_WIDTH
    o_ga = o_gm + D
    seg = lambda o, w: wi[:, o:o + w]
    w_main = jnp.concatenate(
        [seg(o_gm, D), seg(o_ga, D), seg(o_qm, M_WIDTH), seg(o_km, M_WIDTH), seg(o_vm, M_WIDTH),
         seg(o_om, M_WIDTH), seg(o_qa, A_WIDTH), seg(o_ka, A_WIDTH), seg(o_va, A_WIDTH)], axis=1).astype(BF16)
    w_if = jnp.pad(seg(o_if, 2 * M_HEADS), ((0, 0), (0, LANES_V7X - 2 * M_HEADS))).astype(BF16)

    proj, if_rows = _inproj(x2d, norm_mix_w[l].reshape(1, D), w_main, w_if)
    p3 = proj.reshape(B, S, N_MAIN)

    nc = S // CHUNK
    gcol3 = if_rows.reshape(B, S, LANES_V7X)
    grow4 = if_rows[:, :2 * M_HEADS].reshape(B, nc, CHUNK, 2 * M_HEADS).transpose(0, 1, 3, 2)
    bias_col = jnp.pad(if_bias[l], (0, LANES_V7X - 2 * M_HEADS)).reshape(1, LANES_V7X)
    bias_row = jnp.broadcast_to(if_bias[l][:, None], (2 * M_HEADS, CHUNK))
    hm = _mlstm(p3, gcol3, grow4, conv_w[l], conv_b[l].reshape(1, 2 * M_WIDTH), bias_col, bias_row,
                m_head_norm_w[l].reshape(1, M_WIDTH))

    ya = _band_attn(p3, _band_bias(rel_bias[l]))

    x1 = _merge(hm.reshape(T, M_WIDTH), ya.reshape(T, A_WIDTH), proj, x2d,
                w_branch_m[l].astype(BF16), w_branch_a[l].astype(BF16), w_out[l].astype(BF16))

    km, vm = _memkv(mem, norm_mem_w[l].reshape(1, D), wkv_x[l].astype(BF16))
    wr = jnp.pad(w_router[l], ((0, 0), (0, LANES_V7X - N_EXPERTS)))
    wr_hi = wr.astype(BF16)
    wr_lo = (wr - wr_hi.astype(F32)).astype(BF16)
    br = jnp.concatenate([b_router[l], jnp.full((LANES_V7X - N_EXPERTS,), NEG_BIG, F32)]).reshape(1, LANES_V7X)
    x2, h3p, te, tg = _xattn(x1, S, norm_x_w[l].reshape(1, D), wq_x[l].astype(BF16), km, vm,
                             wo_x[l].astype(BF16), norm_ffn_w[l].reshape(1, D), wr_hi, wr_lo, br)

    A = T * TOP_K
    n_vis = (A // MOE_BLK + N_EXPERTS + N_EXPERTS * (MOE_G - 1)) // MOE_G
    dest, visit_e, visit_nb, n_used = _route(te[:, :TOP_K], n_vis)
    dest3 = (dest * TOK_SUB).reshape(T // min(MOE_TT, T), 1, -1)
    xs = _dispatch(visit_nb, dest3, h3p, n_vis * MOE_R)
    yrows = _experts(visit_e, visit_nb, n_used, xs, w_gu[l], b_gu[l].reshape(N_EXPERTS, 1, 2 * D_FF),
                     w_down[l], b_down[l].reshape(N_EXPERTS, 1, D))
    out = _combine(dest3, tg, x2, yrows, final_norm_w.reshape(1, D))
    return out.reshape(B, S, D)
```

```python
import functools

import jax
import jax.numpy as jnp
from jax import lax
from jax.experimental import pallas as pl
from jax.experimental.pallas import tpu as pltpu

F32 = jnp.float32
BF16 = jnp.bfloat16
I32 = jnp.int32
U32 = jnp.uint32

D_MODEL = 2048
CHUNK = 64
EPS = 1e-6
M_HEADS = 4
M_HEAD_DIM = 256
M_WIDTH = M_HEADS * M_HEAD_DIM
CONV_WIDTH = 4
A_HEADS = 16
A_HEAD_DIM = 64
A_WIDTH = A_HEADS * A_HEAD_DIM
LEFT_CHUNKS = 8
MAX_REL = 128
X_HEADS = 4
X_HEAD_DIM = 128
X_WIDTH = X_HEADS * X_HEAD_DIM
N_EXPERTS = 32
TOP_K = 4
D_FF = D_MODEL
SWIGLU_LIMIT = 7.0
SWIGLU_ALPHA = 1.702

LANES_V7X = 128
SUBLANES_V7X = 8
VMEM_BYTES_V7X = 64 * 1024 * 1024
VMEM_LIMIT_CAP = 60000 * 1024
DMA_PRIORITIES = 2

NEG_BIG = -1e30

COL_GM = 0
COL_GA = 2048
COL_QM = 4096
COL_KM = 5120
COL_VM = 6144
COL_OM = 7168
COL_QA = 8192
COL_KA = 9216
COL_VA = 10240
N_MAIN = 11264

IN_TM = 1024
IN_TN = 1024
ATT_TQ = 512
ATT_SUB = 128
ATT_WIN = ATT_SUB + LEFT_CHUNKS * CHUNK
MERGE_TM = 512
XATT_TM = 512
MOE_BLK = 256
MOE_G = 5
MOE_R = MOE_G * MOE_BLK
MOE_BIG = 4
MOE_TF = 256
MOE_TT = 256


def _vmem_limit(nbytes):
    return int(min(VMEM_LIMIT_CAP, max(nbytes, 16 * 1024 * 1024)))


def _nt_dot(a, b):
    return lax.dot_general(a, b, (((1,), (1,)), ((), ())), preferred_element_type=F32)


def _tn_dot(a, b):
    return lax.dot_general(a, b, (((0,), (0,)), ((), ())), preferred_element_type=F32)


def _sigmoid(x):
    return 1.0 / (1.0 + jnp.exp(-x))


def _log_sigmoid(x):
    return jnp.minimum(x, 0.0) - jnp.log1p(jnp.exp(-jnp.abs(x)))


def _rms(x, w):
    return x * lax.rsqrt(jnp.mean(x * x, axis=-1, keepdims=True) + EPS) * w


def _pack_bf16_pairs(xb):
    n = xb.shape[1] // 2
    hi = lax.bitcast_convert_type(xb[:, :n].astype(F32), U32)
    lo = lax.bitcast_convert_type(xb[:, n:].astype(F32), U32)
    return hi | (lo >> 16)


def _unpack_bf16_pairs(w):
    hi = lax.bitcast_convert_type(w & jnp.uint32(0xFFFF0000), F32)
    lo = lax.bitcast_convert_type(w << 16, F32)
    return hi, lo


TOK_SUB = SUBLANES_V7X
TOK_WORDS = TOK_SUB * LANES_V7X


def _store_token_tiles(ref, tok0, packed):
    n = packed.shape[0]
    for s in range(TOK_SUB):
        ref[pl.ds(tok0 * TOK_SUB + s, n, stride=TOK_SUB), :] = packed[:, s * LANES_V7X:(s + 1) * LANES_V7X]


def _load_token_slab(ref, tok0, n, s):
    return ref[pl.ds(tok0 * TOK_SUB + s, n, stride=TOK_SUB), :]


def _inproj_kernel(x_ref, nw_ref, w_ref, wif_ref, o_ref, if_ref, h_scr):
    @pl.when(pl.program_id(1) == 0)
    def _():
        hb = _rms(x_ref[...], nw_ref[...]).astype(BF16)
        h_scr[...] = hb
        if_ref[...] = jnp.dot(hb, wif_ref[...], preferred_element_type=F32)

    o_ref[...] = jnp.dot(h_scr[...], w_ref[...], preferred_element_type=F32).astype(o_ref.dtype)


def _inproj(x2d, norm_w, w_main, w_if):
    T, D = x2d.shape
    tm = min(IN_TM, T)
    grid = (T // tm, N_MAIN // IN_TN)
    vmem = 2 * tm * D * 4 + tm * D * 2 + 2 * D * IN_TN * 2 + 2 * tm * IN_TN * 2 + 4 * tm * LANES_V7X * 4
    return pl.pallas_call(
        _inproj_kernel,
        out_shape=(jax.ShapeDtypeStruct((T, N_MAIN), BF16), jax.ShapeDtypeStruct((T, LANES_V7X), F32)),
        grid=grid,
        in_specs=[
            pl.BlockSpec((tm, D), lambda i, j: (i, 0)),
            pl.BlockSpec((1, D), lambda i, j: (0, 0)),
            pl.BlockSpec((D, IN_TN), lambda i, j: (0, j)),
            pl.BlockSpec((D, LANES_V7X), lambda i, j: (0, 0)),
        ],
        out_specs=(
            pl.BlockSpec((tm, IN_TN), lambda i, j: (i, j)),
            pl.BlockSpec((tm, LANES_V7X), lambda i, j: (i, 0)),
        ),
        scratch_shapes=[pltpu.VMEM((tm, D), BF16)],
        compiler_params=pltpu.CompilerParams(
            dimension_semantics=("parallel", "arbitrary"),
            vmem_limit_bytes=_vmem_limit(vmem + (8 << 20))),
        name="inproj",
    )(x2d, norm_w, w_main, w_if)


def _mlstm_kernel(q_ref, k_ref, v_ref, o_ref, gc_ref, gr_ref, cw_ref, cb_ref, bc_ref, br_ref, nw_ref,
                  out_ref, tail, c_st, n_st, m_st):
    nb = q_ref.shape[0]
    L = CHUNK
    dh = M_HEAD_DIM
    halo = tail.shape[1]

    @pl.when(pl.program_id(0) == 0)
    def _():
        tail[...] = jnp.zeros_like(tail)
        c_st[...] = jnp.zeros_like(c_st)
        n_st[...] = jnp.zeros_like(n_st)
        m_st[...] = jnp.zeros_like(m_st)

    row = lax.broadcasted_iota(I32, (L, L), 0)
    col = lax.broadcasted_iota(I32, (L, L), 1)
    lower = col <= row

    srow = lax.broadcasted_iota(I32, (L, halo + L), 0)
    scol = lax.broadcasted_iota(I32, (L, halo + L), 1)
    shift_mat = jnp.concatenate(
        [jnp.where(scol == srow + (halo - (CONV_WIDTH - 1) + j), 1.0, 0.0) for j in range(CONV_WIDTH - 1)],
        axis=0).astype(BF16)

    erow = lax.broadcasted_iota(I32, (dh, dh), 0)
    ecol = lax.broadcasted_iota(I32, (dh, dh), 1)
    eye = jnp.where(erow == ecol, 1.0, 0.0).astype(BF16)

    streams = [(b, h) for b in range(nb) for h in range(M_HEADS)]

    qk = []
    for b in range(nb):
        halves = []
        for half, ref in enumerate((q_ref, k_ref)):
            cols = slice(half * M_WIDTH, (half + 1) * M_WIDTH)
            cur = ref[b]
            ext = jnp.concatenate([tail[b, :, cols], cur], axis=0)
            back = jnp.dot(shift_mat, ext, preferred_element_type=F32)
            conv = cb_ref[:, cols] + cur.astype(F32) * cw_ref[CONV_WIDTH - 1:CONV_WIDTH, cols]
            for j in range(CONV_WIDTH - 1):
                conv = conv + back[j * L:(j + 1) * L] * cw_ref[j:j + 1, cols]
            halves.append(conv * _sigmoid(conv))
            tail[b, :, cols] = cur[L - halo:L]
        qk.append(halves)

    st = []
    for b, h in streams:
        s = b * M_HEADS + h
        lo, hi = h * dh, (h + 1) * dh
        gcol = gc_ref[b] + bc_ref[...]
        grow = gr_ref[b, 0] + br_ref[...]
        q = qk[b][0][:, lo:hi]
        k = qk[b][1][:, lo:hi] * (dh ** -0.5)
        ig_c = gcol[:, h:h + 1]
        ig_r = grow[h:h + 1, :]
        ls_c = _log_sigmoid(gcol[:, M_HEADS + h:M_HEADS + h + 1])
        ls_r = _log_sigmoid(grow[M_HEADS + h:M_HEADS + h + 1, :])
        b_c = jnp.sum(jnp.where(lower, ls_r, 0.0), axis=1, keepdims=True)
        b_r = jnp.sum(jnp.where(row <= col, ls_c, 0.0), axis=0, keepdims=True)
        g = jnp.sum(ls_r, axis=1, keepdims=True)
        m_old = m_st[s][:, 0:1]
        w_end = g - b_c + ig_c
        a = jnp.max(w_end, axis=0, keepdims=True)
        dlog = jnp.where(lower, b_c - b_r + ig_r, NEG_BIG)
        inter = b_c + m_old
        m_j = jnp.maximum(inter, jnp.max(dlog, axis=1, keepdims=True))
        st.append(dict(s=s, lo=lo, hi=hi, b=b, q=q, k=k, qb=q.astype(BF16), kb=k.astype(BF16),
                       vb=v_ref[b, :, lo:hi], g=g, a=a, m_old=m_old, m_j=m_j,
                       e_end=jnp.exp(w_end - a), decay=jnp.exp(dlog - m_j), w_int=jnp.exp(inter - m_j),
                       c_old=c_st[s], n_old=n_st[s]))

    for d in st:
        d["sm"] = _nt_dot(d["qb"], d["kb"]) * d["decay"]
    for d in st:
        d["num"] = (jnp.dot(d["sm"].astype(BF16), d["vb"], preferred_element_type=F32)
                    + d["w_int"] * _nt_dot(d["qb"], d["c_old"].astype(BF16)))
    for d in st:
        den = (jnp.sum(d["sm"], axis=1, keepdims=True)
               + d["w_int"] * jnp.sum(d["q"] * d["n_old"], axis=1, keepdims=True))
        hh = d["num"] / jnp.maximum(jnp.abs(den), jnp.exp(-d["m_j"]))
        hn = _rms(hh, nw_ref[:, d["lo"]:d["hi"]])
        og = _sigmoid(o_ref[d["b"], :, d["lo"]:d["hi"]].astype(F32))
        out_ref[d["b"], :, d["lo"]:d["hi"]] = (og * hn).astype(out_ref.dtype)
    for d in st:
        s = d["s"]
        m_new = jnp.maximum(d["g"] + d["m_old"], d["a"])
        s_prev = jnp.exp(d["g"] + d["m_old"] - m_new)
        s_loc = jnp.exp(d["a"] - m_new)
        ve = (d["vb"].astype(F32) * (d["e_end"] * s_loc)).astype(BF16)
        ve_t = _nt_dot(eye, ve).astype(BF16)
        c_st[s] = s_prev * d["c_old"] + jnp.dot(ve_t, d["kb"], preferred_element_type=F32)
        n_st[s] = s_prev * d["n_old"] + s_loc * jnp.sum(d["e_end"] * d["k"], axis=0, keepdims=True)
        m_st[s] = jnp.broadcast_to(m_new, (1, LANES_V7X))


def _mlstm(p3, gcol3, grow4, conv_w, conv_b, bias_col, bias_row, head_w):
    B, S, _ = p3.shape
    nc = S // CHUNK
    ns = B * M_HEADS
    blk = lambda cb: pl.BlockSpec((B, CHUNK, M_WIDTH), lambda c, cb=cb: (0, c, cb))
    full = lambda shape: pl.BlockSpec(shape, lambda c: (0,) * len(shape))
    vmem = (2 * 5 * B * CHUNK * M_WIDTH * 2 + B * (CHUNK + 8) * 2 * M_WIDTH * 4
            + ns * M_HEAD_DIM * M_HEAD_DIM * 4 + (16 << 20))
    return pl.pallas_call(
        _mlstm_kernel,
        out_shape=jax.ShapeDtypeStruct((B, S, M_WIDTH), BF16),
        grid=(nc,),
        in_specs=[
            blk(COL_QM // M_WIDTH), blk(COL_KM // M_WIDTH), blk(COL_VM // M_WIDTH), blk(COL_OM // M_WIDTH),
            pl.BlockSpec((B, CHUNK, LANES_V7X), lambda c: (0, c, 0)),
            pl.BlockSpec((B, 1, 2 * M_HEADS, CHUNK), lambda c: (0, c, 0, 0)),
            full((CONV_WIDTH, 2 * M_WIDTH)), full((1, 2 * M_WIDTH)),
            full((1, LANES_V7X)), full((2 * M_HEADS, CHUNK)), full((1, M_WIDTH)),
        ],
        out_specs=pl.BlockSpec((B, CHUNK, M_WIDTH), lambda c: (0, c, 0)),
        scratch_shapes=[
            pltpu.VMEM((B, 2 * SUBLANES_V7X, 2 * M_WIDTH), BF16),
            pltpu.VMEM((ns, M_HEAD_DIM, M_HEAD_DIM), F32),
            pltpu.VMEM((ns, 1, M_HEAD_DIM), F32),
            pltpu.VMEM((ns, 1, LANES_V7X), F32),
        ],
        compiler_params=pltpu.CompilerParams(
            dimension_semantics=("arbitrary",), vmem_limit_bytes=_vmem_limit(vmem)),
        name="mlstm",
    )(p3, p3, p3, p3, gcol3, grow4, conv_w, conv_b, bias_col, bias_row, head_w)


def _band_attn_kernel(q_ref, kp_ref, kc_ref, vp_ref, vc_ref, bias_ref, out_ref):
    qb_idx = pl.program_id(2)
    tq = q_ref.shape[1]
    q2 = q_ref[0]
    kcat = jnp.concatenate([kp_ref[0], kc_ref[0]], axis=0)
    vcat = jnp.concatenate([vp_ref[0], vc_ref[0]], axis=0)
    lane = lax.broadcasted_iota(I32, (1, LANES_V7X), 1)
    wcol = lax.broadcasted_iota(I32, (1, ATT_WIN), 1)
    first_pen = jnp.where(qb_idx == 0, NEG_BIG, 0.0)
    scale = A_HEAD_DIM ** -0.5
    bias = bias_ref[0]

    subs = []
    for t in range(tq // ATT_SUB):
        r0 = t * ATT_SUB
        w0 = tq + r0 - LEFT_CHUNKS * CHUNK
        pen = jnp.where(wcol < (tq - w0), first_pen, 0.0)
        qs = q2[r0:r0 + ATT_SUB] * scale
        zero = jnp.zeros_like(qs)
        qst = jnp.concatenate([jnp.where(lane < A_HEAD_DIM, qs, zero),
                               jnp.where(lane >= A_HEAD_DIM, qs, zero)], axis=0)
        subs.append(dict(r0=r0, w0=w0, pen=pen, qst=qst))
    for d in subs:
        d["sc"] = _nt_dot(d["qst"], kcat[d["w0"]:d["w0"] + ATT_WIN]) + bias + d["pen"]
    for d in subs:
        p = jnp.exp(d["sc"] - jnp.max(d["sc"], axis=1, keepdims=True))
        d["den"] = jnp.sum(p, axis=1, keepdims=True)
        d["p"] = p.astype(BF16)
    for d in subs:
        d["o"] = jnp.dot(d["p"], vcat[d["w0"]:d["w0"] + ATT_WIN], preferred_element_type=F32) / d["den"]
    for d in subs:
        o, r0 = d["o"], d["r0"]
        out_ref[0, r0:r0 + ATT_SUB, :] = jnp.where(
            lane < A_HEAD_DIM, o[:ATT_SUB], o[ATT_SUB:]).astype(out_ref.dtype)


def _band_attn(p3, bias):
    B, S, _ = p3.shape
    tq = ATT_TQ
    nq = S // tq
    nhp = A_HEADS // 2
    cq, ck, cv = COL_QA // LANES_V7X, COL_KA // LANES_V7X, COL_VA // LANES_V7X
    cur = lambda c0: pl.BlockSpec((1, tq, LANES_V7X), lambda b, hp, i, c0=c0: (b, i, c0 + hp))
    prev = lambda c0: pl.BlockSpec((1, tq, LANES_V7X), lambda b, hp, i, c0=c0: (b, jnp.maximum(i - 1, 0), c0 + hp))
    return pl.pallas_call(
        _band_attn_kernel,
        out_shape=jax.ShapeDtypeStruct((B, S, A_WIDTH), BF16),
        grid=(B, nhp, nq),
        in_specs=[cur(cq), prev(ck), cur(ck), prev(cv), cur(cv),
                  pl.BlockSpec((1, 2 * ATT_SUB, ATT_WIN), lambda b, hp, i: (hp, 0, 0))],
        out_specs=pl.BlockSpec((1, tq, LANES_V7X), lambda b, hp, i: (b, i, hp)),
        compiler_params=pltpu.CompilerParams(
            dimension_semantics=("parallel", "parallel", "arbitrary"),
            vmem_limit_bytes=_vmem_limit(32 << 20)),
        name="band_attn",
    )(p3, p3, p3, p3, p3, bias)


def _band_bias(rel_table):
    n = 6 * LANES_V7X
    c = jnp.arange(n)
    d = jnp.where(c < ATT_WIN, c, c - n)
    rel = jnp.clip(LEFT_CHUNKS * CHUNK - d, -MAX_REL, MAX_REL) + MAX_REL
    vec = rel_table[:, rel].astype(F32)
    sheared = jnp.tile(vec, (1, ATT_SUB))[:, :ATT_SUB * (n - 1)].reshape(A_HEADS, ATT_SUB, n - 1)
    i = jnp.arange(ATT_SUB)[:, None]
    kp = jnp.arange(ATT_WIN)[None, :]
    ci, kc = i // CHUNK, kp // CHUNK
    valid = (kc >= ci) & (kc <= ci + LEFT_CHUNKS)
    bias = jnp.where(valid[None], sheared[:, :, :ATT_WIN], NEG_BIG)
    return bias.reshape(A_HEADS // 2, 2 * ATT_SUB, ATT_WIN)


def _merge_kernel(hm_ref, ya_ref, gm_ref, ga_ref, x_ref, wbm_ref, wba_ref, wo_ref, out_ref):
    ym = jnp.dot(hm_ref[...], wbm_ref[...], preferred_element_type=F32)
    ya = jnp.dot(ya_ref[...], wba_ref[...], preferred_element_type=F32)
    merged = _sigmoid(gm_ref[...].astype(F32)) * ym + _sigmoid(ga_ref[...].astype(F32)) * ya
    out_ref[...] = x_ref[...] + jnp.dot(merged.astype(BF16), wo_ref[...], preferred_element_type=F32)


def _const_spec(shape):
    return pl.BlockSpec(shape, lambda i: (0,) * len(shape), pipeline_mode=pl.Buffered(1))


def _merge(hm2d, ya2d, p2d, x2d, wbm, wba, wout):
    T, D = x2d.shape
    tm = min(MERGE_TM, T)
    vmem = ((2 * M_WIDTH * D + D * D) * 2 + 2 * (2 * tm * M_WIDTH * 2 + 2 * tm * D * 2 + 2 * tm * D * 4)
            + 6 * tm * D * 4)
    return pl.pallas_call(
        _merge_kernel,
        out_shape=jax.ShapeDtypeStruct((T, D), F32),
        grid=(T // tm,),
        in_specs=[
            pl.BlockSpec((tm, M_WIDTH), lambda i: (i, 0)),
            pl.BlockSpec((tm, A_WIDTH), lambda i: (i, 0)),
            pl.BlockSpec((tm, D), lambda i: (i, COL_GM // D)),
            pl.BlockSpec((tm, D), lambda i: (i, COL_GA // D)),
            pl.BlockSpec((tm, D), lambda i: (i, 0)),
            _const_spec((M_WIDTH, D)), _const_spec((A_WIDTH, D)), _const_spec((D, D)),
        ],
        out_specs=pl.BlockSpec((tm, D), lambda i: (i, 0)),
        compiler_params=pltpu.CompilerParams(
            dimension_semantics=("parallel",), vmem_limit_bytes=_vmem_limit(vmem)),
        name="merge_out",
    )(hm2d, ya2d, p2d, p2d, x2d, wbm, wba, wout)


def _memkv_kernel(mem_ref, nw_ref, w_ref, k_ref, v_ref):
    hb = _rms(mem_ref[0], nw_ref[...]).astype(BF16)
    kv = jnp.dot(hb, w_ref[...], preferred_element_type=F32)
    k_ref[0] = kv[:, :X_WIDTH].astype(k_ref.dtype)
    v_ref[0] = kv[:, X_WIDTH:].astype(v_ref.dtype)


def _memkv(mem, norm_w, wkv):
    B, N, D = mem.shape
    return pl.pallas_call(
        _memkv_kernel,
        out_shape=(jax.ShapeDtypeStruct((B, N, X_WIDTH), BF16), jax.ShapeDtypeStruct((B, N, X_WIDTH), BF16)),
        grid=(B,),
        in_specs=[pl.BlockSpec((1, N, D), lambda b: (b, 0, 0)),
                  pl.BlockSpec((1, D), lambda b: (0, 0)),
                  pl.BlockSpec((D, 2 * X_WIDTH), lambda b: (0, 0))],
        out_specs=(pl.BlockSpec((1, N, X_WIDTH), lambda b: (b, 0, 0)),
                   pl.BlockSpec((1, N, X_WIDTH), lambda b: (b, 0, 0))),
        compiler_params=pltpu.CompilerParams(
            dimension_semantics=("parallel",), vmem_limit_bytes=_vmem_limit(24 << 20)),
        name="mem_kv",
    )(mem, norm_w, wkv)


def _xattn_kernel(x_ref, nxw_ref, wq_ref, km_ref, vm_ref, wo_ref, nfw_ref, wrh_ref, wrl_ref, br_ref,
                  x2_ref, h3_ref, te_ref, tg_ref):
    x1 = x_ref[...]
    hb = _rms(x1, nxw_ref[...]).astype(BF16)
    q = jnp.dot(hb, wq_ref[...], preferred_element_type=F32).astype(BF16)
    km = km_ref[0]
    vm = vm_ref[0]
    outs = []
    for h in range(X_HEADS):
        lo, hi = h * X_HEAD_DIM, (h + 1) * X_HEAD_DIM
        sc = _nt_dot(q[:, lo:hi], km[:, lo:hi]) * (X_HEAD_DIM ** -0.5)
        mx = jnp.max(sc, axis=1, keepdims=True)
        p = jnp.exp(sc - mx)
        den = jnp.sum(p, axis=1, keepdims=True)
        outs.append(jnp.dot(p.astype(BF16), vm[:, lo:hi], preferred_element_type=F32) / den)
    o = jnp.concatenate(outs, axis=1).astype(BF16)
    x2 = x1 + jnp.dot(o, wo_ref[...], preferred_element_type=F32)
    x2_ref[...] = x2

    h3 = _rms(x2, nfw_ref[...])
    h_hi = h3.astype(BF16)
    _store_token_tiles(h3_ref, 0, _pack_bf16_pairs(h_hi))
    h_lo = (h3 - h_hi.astype(F32)).astype(BF16)
    logits = (jnp.dot(h_hi, wrh_ref[...], preferred_element_type=F32)
              + jnp.dot(h_lo, wrh_ref[...], preferred_element_type=F32)
              + jnp.dot(h_hi, wrl_ref[...], preferred_element_type=F32)
              + br_ref[...])

    colf = lax.broadcasted_iota(I32, logits.shape, 1).astype(F32)
    work = logits
    vals, idxs = [], []
    for _ in range(TOP_K):
        mx = jnp.max(work, axis=1, keepdims=True)
        ix = jnp.min(jnp.where(work == mx, colf, float(LANES_V7X)), axis=1, keepdims=True)
        vals.append(mx)
        idxs.append(ix)
        work = jnp.where(colf == ix, -jnp.inf, work)
    es = [jnp.exp(v - vals[0]) for v in vals]
    tot = es[0] + es[1] + es[2] + es[3]
    te = jnp.zeros_like(logits)
    tg = jnp.zeros_like(logits)
    for kk in range(TOP_K):
        te = jnp.where(colf == float(kk), idxs[kk], te)
        tg = jnp.where(colf == float(kk), es[kk] / tot, tg)
    te_ref[...] = te.astype(I32)
    tg_ref[...] = tg


def _xattn(x1, batch_rows, norm_x_w, wq, km, vm, wo, norm_ffn_w, wr_hi, wr_lo, br):
    T, D = x1.shape
    tm = min(XATT_TM, batch_rows)
    per_b = batch_rows // tm
    n_mem = km.shape[1]
    row = lambda w: pl.BlockSpec((tm, w), lambda i: (i, 0))
    vmem = (2 * D * X_WIDTH * 2 + 2 * D * LANES_V7X * 2 + 4 * n_mem * X_WIDTH * 2
            + 2 * 3 * tm * D * 4 + 8 * tm * D * 4)
    return pl.pallas_call(
        _xattn_kernel,
        out_shape=(jax.ShapeDtypeStruct((T, D), F32), jax.ShapeDtypeStruct((T * TOK_SUB, LANES_V7X), U32),
                   jax.ShapeDtypeStruct((T, LANES_V7X), I32), jax.ShapeDtypeStruct((T, LANES_V7X), F32)),
        grid=(T // tm,),
        in_specs=[
            row(D), _const_spec((1, D)), _const_spec((D, X_WIDTH)),
            pl.BlockSpec((1, n_mem, X_WIDTH), lambda i: (i // per_b, 0, 0)),
            pl.BlockSpec((1, n_mem, X_WIDTH), lambda i: (i // per_b, 0, 0)),
            _const_spec((X_WIDTH, D)), _const_spec((1, D)),
            _const_spec((D, LANES_V7X)), _const_spec((D, LANES_V7X)), _const_spec((1, LANES_V7X)),
        ],
        out_specs=(row(D), pl.BlockSpec((tm * TOK_SUB, LANES_V7X), lambda i: (i, 0)),
                   row(LANES_V7X), row(LANES_V7X)),
        compiler_params=pltpu.CompilerParams(
            dimension_semantics=("parallel",), vmem_limit_bytes=_vmem_limit(vmem)),
        name="xattn_router",
    )(x1, norm_x_w, wq, km, vm, wo, norm_ffn_w, wr_hi, wr_lo, br)


def _dispatch_kernel(nb_ref, dest_ref, h_ref, xs_ref, zero_scr, sem, zsem):
    tt = h_ref.shape[0] // TOK_SUB
    nv = nb_ref.shape[0]

    @pl.when(pl.program_id(0) == 0)
    def _():
        zero_scr[...] = jnp.zeros_like(zero_scr)

        def zero_copy(v, b):
            row0 = pl.multiple_of((v * MOE_R + b * MOE_BLK) * TOK_SUB, MOE_BLK * TOK_SUB)
            return pltpu.make_async_copy(zero_scr, xs_ref.at[pl.ds(row0, MOE_BLK * TOK_SUB)], zsem)

        def each_partial_block(v, fn):
            for b in range(MOE_G):
                @pl.when(b >= nb_ref[v] - 1)
                def _():
                    fn(zero_copy(v, b))

        def zstart(v, carry):
            each_partial_block(v, lambda c: c.start())
            return carry

        def zwait(v, carry):
            each_partial_block(v, lambda c: c.wait())
            return carry

        lax.fori_loop(0, nv, zstart, 0)
        lax.fori_loop(0, nv, zwait, 0)

    def tile_copy(t, kk):
        src = h_ref.at[pl.ds(pl.multiple_of(t * TOK_SUB, TOK_SUB), TOK_SUB)]
        dst = xs_ref.at[pl.ds(pl.multiple_of(dest_ref[0, t * TOP_K + kk], TOK_SUB), TOK_SUB)]
        return pltpu.make_async_copy(src, dst, sem)

    def start(t, carry):
        for kk in range(TOP_K):
            tile_copy(t, kk).start(priority=kk % DMA_PRIORITIES)
        return carry

    def wait(t, carry):
        for kk in range(TOP_K):
            tile_copy(t, kk).wait()
        return carry

    lax.fori_loop(0, tt, start, 0)
    lax.fori_loop(0, tt, wait, 0)


def _dispatch(visit_nb, dest3, h3p, n_rows):
    T = h3p.shape[0] // TOK_SUB
    tt = min(MOE_TT, T)
    grid_spec = pltpu.PrefetchScalarGridSpec(
        num_scalar_prefetch=1,
        grid=(T // tt,),
        in_specs=[
            pl.BlockSpec((None, 1, tt * TOP_K), lambda i, nb: (i, 0, 0), memory_space=pltpu.SMEM),
            pl.BlockSpec((tt * TOK_SUB, LANES_V7X), lambda i, nb: (i, 0)),
        ],
        out_specs=pl.BlockSpec(memory_space=pl.ANY),
        scratch_shapes=[pltpu.VMEM((MOE_BLK * TOK_SUB, LANES_V7X), U32),
                        pltpu.SemaphoreType.DMA(()), pltpu.SemaphoreType.DMA(())],
    )
    return pl.pallas_call(
        _dispatch_kernel,
        out_shape=jax.ShapeDtypeStruct((n_rows * TOK_SUB, LANES_V7X), U32),
        grid_spec=grid_spec,
        compiler_params=pltpu.CompilerParams(
            dimension_semantics=("arbitrary",), has_side_effects=True,
            vmem_limit_bytes=_vmem_limit(4 * tt * TOK_WORDS * 4 + MOE_BLK * TOK_WORDS * 4 + (8 << 20))),
        name="moe_dispatch",
    )(visit_nb, dest3, h3p)


def _experts_kernel(ve_ref, nb_ref, nu_ref, x_ref, wg_ref, wu_ref, bg_ref, bu_ref, wd_ref, bd_ref, out_ref,
                    xb_scr, acc_scr, wgb_scr, wub_scr, wdb_scr):
    del ve_ref, nu_ref
    v = pl.program_id(0)
    j = pl.program_id(1)
    last = pl.num_programs(1) - 1
    nb = nb_ref[v]
    half = TOK_WORDS

    def block_rows(b):
        return pl.ds(pl.multiple_of(b * MOE_BLK, MOE_BLK), MOE_BLK)

    @pl.when(nb > 0)
    def _():
        @pl.when(j == 0)
        def _():
            def stage(b, carry):
                rows = block_rows(b)
                for s in range(TOK_SUB):
                    hi, lo = _unpack_bf16_pairs(_load_token_slab(x_ref, b * MOE_BLK, MOE_BLK, s))
                    xb_scr[rows, s * LANES_V7X:(s + 1) * LANES_V7X] = hi.astype(BF16)
                    xb_scr[rows, half + s * LANES_V7X:half + (s + 1) * LANES_V7X] = lo.astype(BF16)
                acc_scr[rows, :] = jnp.zeros((MOE_BLK, acc_scr.shape[1]), F32)
                return carry
            lax.fori_loop(0, nb, stage, 0)

        def ffn_rows(rows, wg, wu, wd):
            xb = xb_scr[rows, :]
            g = jnp.dot(xb, wg, preferred_element_type=F32) + bg_ref[...]
            u = jnp.dot(xb, wu, preferred_element_type=F32) + bu_ref[...]
            g = jnp.minimum(g, SWIGLU_LIMIT)
            u = jnp.clip(u, -SWIGLU_LIMIT, SWIGLU_LIMIT)
            act = (u + 1.0) * (g * _sigmoid(SWIGLU_ALPHA * g))
            acc_scr[rows, :] += jnp.dot(act.astype(BF16), wd, preferred_element_type=F32)

        def cast_weights():
            wg, wu, wd = wg_ref[...].astype(BF16), wu_ref[...].astype(BF16), wd_ref[...].astype(BF16)
            wgb_scr[...] = wg
            wub_scr[...] = wu
            wdb_scr[...] = wd
            return wg, wu, wd

        full = nb == MOE_G
        big = jnp.logical_and(nb >= MOE_BIG, jnp.logical_not(full))

        @pl.when(full)
        def _():
            ffn_rows(pl.ds(0, MOE_R), *cast_weights())

        @pl.when(big)
        def _():
            ffn_rows(pl.ds(0, MOE_BIG * MOE_BLK), *cast_weights())

        @pl.when(nb < MOE_BIG)
        def _():
            cast_weights()

        def block(b, carry):
            ffn_rows(block_rows(b), wgb_scr[...], wub_scr[...], wdb_scr[...])
            return carry
        lax.fori_loop(jnp.where(full, MOE_G, jnp.where(big, MOE_BIG, 0)), nb, block, 0)

        @pl.when(j == last)
        def _():
            def finish(b, carry):
                y = (acc_scr[block_rows(b), :] + bd_ref[...]).astype(BF16)
                _store_token_tiles(out_ref, b * MOE_BLK, _pack_bf16_pairs(y))
                return carry
            lax.fori_loop(0, nb, finish, 0)

            def blank(b, carry):
                rows = pl.ds(pl.multiple_of(b * MOE_BLK * TOK_SUB, MOE_BLK * TOK_SUB), MOE_BLK * TOK_SUB)
                out_ref[rows, :] = jnp.zeros((MOE_BLK * TOK_SUB, LANES_V7X), U32)
                return carry
            lax.fori_loop(nb, MOE_G, blank, 0)

    @pl.when((nb == 0) & (j == last))
    def _():
        out_ref[...] = jnp.zeros_like(out_ref)


def _experts(visit_e, visit_nb, n_used, xs, wgu, bgu, wdn, bdn):
    n_rows = xs.shape[0] // TOK_SUB
    half = TOK_WORDS
    D = 2 * half
    tf = MOE_TF
    nj = D_FF // tf
    nv = n_rows // MOE_R

    def frozen(v, j, nu):
        return jnp.where(v < nu[0], j, nj - 1)

    def vrow(v, nu):
        return jnp.minimum(v, nu[0] - 1)

    grid_spec = pltpu.PrefetchScalarGridSpec(
        num_scalar_prefetch=3,
        grid=(nv, nj),
        in_specs=[
            pl.BlockSpec((MOE_R * TOK_SUB, LANES_V7X), lambda v, j, ve, nb, nu: (vrow(v, nu), 0)),
            pl.BlockSpec((None, D, tf), lambda v, j, ve, nb, nu: (ve[v], 0, frozen(v, j, nu))),
            pl.BlockSpec((None, D, tf), lambda v, j, ve, nb, nu: (ve[v], 0, nj + frozen(v, j, nu))),
            pl.BlockSpec((None, 1, tf), lambda v, j, ve, nb, nu: (ve[v], 0, frozen(v, j, nu))),
            pl.BlockSpec((None, 1, tf), lambda v, j, ve, nb, nu: (ve[v], 0, nj + frozen(v, j, nu))),
            pl.BlockSpec((None, tf, D), lambda v, j, ve, nb, nu: (ve[v], frozen(v, j, nu), 0)),
            pl.BlockSpec((None, 1, D), lambda v, j, ve, nb, nu: (ve[v], 0, 0)),
        ],
        out_specs=pl.BlockSpec((MOE_R * TOK_SUB, LANES_V7X), lambda v, j, ve, nb, nu: (v, 0)),
        scratch_shapes=[pltpu.VMEM((MOE_R, D), BF16), pltpu.VMEM((MOE_R, D), F32),
                        pltpu.VMEM((D, tf), BF16), pltpu.VMEM((D, tf), BF16), pltpu.VMEM((tf, D), BF16)],
    )
    vmem = (2 * MOE_R * half * 4 + MOE_R * D * 2 + MOE_R * D * 4 + 2 * MOE_R * half * 4
            + 2 * 3 * D * tf * 4 + 3 * D * tf * 2 + 8 * MOE_BLK * D * 4)
    return pl.pallas_call(
        _experts_kernel,
        out_shape=jax.ShapeDtypeStruct((n_rows * TOK_SUB, LANES_V7X), U32),
        grid_spec=grid_spec,
        compiler_params=pltpu.CompilerParams(
            dimension_semantics=("arbitrary", "arbitrary"), vmem_limit_bytes=_vmem_limit(vmem)),
        name="moe_experts",
    )(visit_e, visit_nb, n_used, xs, wgu, wgu, bgu, bgu, wdn, bdn)


def _combine_kernel(dcur_ref, dnext_ref, gate_ref, x_ref, y_ref, nw_ref, out_ref, buf, y_scr, sems):
    i = pl.program_id(0)
    n = pl.num_programs(0)
    tt = x_ref.shape[0]
    half = TOK_WORDS
    slot = lax.rem(i, 2)

    def row_copy(dref, s, t, kk):
        src = y_ref.at[pl.ds(pl.multiple_of(dref[0, t * TOP_K + kk], TOK_SUB), TOK_SUB)]
        dst = buf.at[s, kk, pl.ds(pl.multiple_of(t * TOK_SUB, TOK_SUB), TOK_SUB)]
        return pltpu.make_async_copy(src, dst, sems.at[s])

    def issue(dref, s):
        def body(t, carry):
            for kk in range(TOP_K):
                row_copy(dref, s, t, kk).start(priority=kk % DMA_PRIORITIES)
            return carry
        lax.fori_loop(0, tt, body, 0)

    @pl.when(i == 0)
    def _():
        issue(dcur_ref, 0)

    @pl.when(i + 1 < n)
    def _():
        issue(dnext_ref, 1 - slot)

    def wait(t, carry):
        for kk in range(TOP_K):
            row_copy(dcur_ref, slot, t, kk).wait()
        return carry
    lax.fori_loop(0, tt, wait, 0)

    gate = gate_ref[...]
    gks = [jnp.broadcast_to(gate[:, kk:kk + 1], (tt, LANES_V7X)) for kk in range(TOP_K)]
    for s in range(TOK_SUB):
        ca = pl.ds(s * LANES_V7X, LANES_V7X)
        cb = pl.ds(half + s * LANES_V7X, LANES_V7X)
        ya = x_ref[:, ca]
        yb = x_ref[:, cb]
        for kk in range(TOP_K):
            hi, lo = _unpack_bf16_pairs(_load_token_slab(buf.at[slot, kk], 0, tt, s))
            ya = ya + gks[kk] * hi
            yb = yb + gks[kk] * lo
        y_scr[:, ca] = ya
        y_scr[:, cb] = yb
    out_ref[...] = _rms(y_scr[...], nw_ref[...])


def _combine(dest3, gates, x2, yrows, final_w):
    T, D = x2.shape
    half = TOK_WORDS
    tt = min(MOE_TT, T)
    n = T // tt
    dspec = lambda f: pl.BlockSpec((None, 1, tt * TOP_K), lambda i: (f(i), 0, 0), memory_space=pltpu.SMEM)
    return pl.pallas_call(
        _combine_kernel,
        out_shape=jax.ShapeDtypeStruct((T, D), F32),
        grid=(n,),
        in_specs=[
            dspec(lambda i: i), dspec(lambda i: jnp.minimum(i + 1, n - 1)),
            pl.BlockSpec((tt, LANES_V7X), lambda i: (i, 0)),
            pl.BlockSpec((tt, D), lambda i: (i, 0)),
            pl.BlockSpec(memory_space=pl.ANY),
            pl.BlockSpec((1, D), lambda i: (0, 0)),
        ],
        out_specs=pl.BlockSpec((tt, D), lambda i: (i, 0)),
        scratch_shapes=[pltpu.VMEM((2, TOP_K, tt * TOK_SUB, LANES_V7X), U32), pltpu.VMEM((tt, D), F32),
                        pltpu.SemaphoreType.DMA((2,))],
        compiler_params=pltpu.CompilerParams(
            dimension_semantics=("arbitrary",),
            vmem_limit_bytes=_vmem_limit(2 * TOP_K * tt * half * 4 + 8 * tt * D * 4 + (8 << 20))),
        name="moe_combine",
    )(dest3, dest3, gates, x2, yrows, final_w)


def _route(top_e, n_vis):
    flat_e = top_e.reshape(-1)
    onehot = (flat_e[:, None] == jnp.arange(N_EXPERTS, dtype=I32)[None, :]).astype(I32)
    incl = jnp.cumsum(onehot, axis=0)
    counts = incl[-1]
    pos = jnp.sum((incl - onehot) * onehot, axis=1)
    nblk = (counts + MOE_BLK - 1) // MOE_BLK
    nvis = (nblk + MOE_G - 1) // MOE_G
    bpv = (nblk + jnp.maximum(nvis, 1) - 1) // jnp.maximum(nvis, 1)
    rpv = bpv * MOE_BLK
    vend = jnp.cumsum(nvis)
    vstart = vend - nvis
    rpv_a = jnp.maximum(jnp.sum(onehot * rpv[None, :], axis=1), 1)
    vs_a = jnp.sum(onehot * vstart[None, :], axis=1)
    q = jnp.floor((pos.astype(F32) + 0.5) / rpv_a.astype(F32)).astype(I32)
    dest = (vs_a + q) * MOE_R + (pos - q * rpv_a)

    n_used = vend[-1]
    v = jnp.minimum(jnp.arange(n_vis, dtype=I32), n_used - 1)[:, None]
    sel = ((vstart[None, :] <= v) & (v < vend[None, :])).astype(I32)
    visit_e = jnp.sum(sel * jnp.arange(N_EXPERTS, dtype=I32)[None, :], axis=1)
    local = v[:, 0] - jnp.sum(sel * vstart[None, :], axis=1)
    v_nblk = jnp.sum(sel * nblk[None, :], axis=1)
    v_bpv = jnp.sum(sel * bpv[None, :], axis=1)
    visit_nb = jnp.clip(v_nblk - local * v_bpv, 0, v_bpv)
    visit_nb = jnp.where(jnp.arange(n_vis) < n_used, visit_nb, 0)
    return dest.astype(I32), visit_e.astype(I32), visit_nb.astype(I32), n_used.astype(I32).reshape(1)


def kernel(x, mem, norm_mix_w, w_in, conv_w, conv_b, if_bias, m_head_norm_w, w_branch_m, w_branch_a, rel_bias, w_out, norm_x_w, norm_mem_w, wq_x, wkv_x, wo_x, norm_ffn_w, w_router, b_router, w_gu, b_gu, w_down, b_down, final_norm_w):
    B, S, D = x.shape
    T = B * S
    l = 0
    x2d = x.reshape(T, D)

    wi = w_in[l]
    o_qm, o_km, o_vm, o_om = 0, M_WIDTH, 2 * M_WIDTH, 3 * M_WIDTH
    o_if = 4 * M_WIDTH
    o_qa = o_if + 2 * M_HEADS
    o_ka, o_va = o_qa + A_WIDTH, o_qa + 2 * A_WIDTH
    o_gm = o_qa + 3 * A_WIDTH
    o_ga = o_gm + D
    seg = lambda o, w: wi[:, o:o + w]
    w_main = jnp.concatenate(
        [seg(o_gm, D), seg(o_ga, D), seg(o_qm, M_WIDTH), seg(o_km, M_WIDTH), seg(o_vm, M_WIDTH),
         seg(o_om, M_WIDTH), seg(o_qa, A_WIDTH), seg(o_ka, A_WIDTH), seg(o_va, A_WIDTH)], axis=1).astype(BF16)
    w_if = jnp.pad(seg(o_if, 2 * M_HEADS), ((0, 0), (0, LANES_V7X - 2 * M_HEADS))).astype(BF16)

    proj, if_rows = _inproj(x2d, norm_mix_w[l].reshape(1, D), w_main, w_if)
    p3 = proj.reshape(B, S, N_MAIN)

    nc = S // CHUNK
    gcol3 = if_rows.reshape(B, S, LANES_V7X)
    grow4 = if_rows[:, :2 * M_HEADS].reshape(B, nc, CHUNK, 2 * M_HEADS).transpose(0, 1, 3, 2)
    bias_col = jnp.pad(if_bias[l], (0, LANES_V7X - 2 * M_HEADS)).reshape(1, LANES_V7X)
    bias_row = jnp.broadcast_to(if_bias[l][:, None], (2 * M_HEADS, CHUNK))
    hm = _mlstm(p3, gcol3, grow4, conv_w[l], conv_b[l].reshape(1, 2 * M_WIDTH), bias_col, bias_row,
                m_head_norm_w[l].reshape(1, M_WIDTH))

    ya = _band_attn(p3, _band_bias(rel_bias[l]))

    x1 = _merge(hm.reshape(T, M_WIDTH), ya.reshape(T, A_WIDTH), proj, x2d,
                w_branch_m[l].astype(BF16), w_branch_a[l].astype(BF16), w_out[l].astype(BF16))

    km, vm = _memkv(mem, norm_mem_w[l].reshape(1, D), wkv_x[l].astype(BF16))
    wr = jnp.pad(w_router[l], ((0, 0), (0, LANES_V7X - N_EXPERTS)))
    wr_hi = wr.astype(BF16)
    wr_lo = (wr - wr_hi.astype(F32)).astype(BF16)
    br = jnp.concatenate([b_router[l], jnp.full((LANES_V7X - N_EXPERTS,), NEG_BIG, F32)]).reshape(1, LANES_V7X)
    x2, h3p, te, tg = _xattn(x1, S, norm_x_w[l].reshape(1, D), wq_x[l].astype(BF16), km, vm,
                             wo_x[l].astype(BF16), norm_ffn_w[l].reshape(1, D), wr_hi, wr_lo, br)

    A = T * TOP_K
    n_vis = (A // MOE_BLK + N_EXPERTS + N_EXPERTS * (MOE_G - 1)) // MOE_G
    dest, visit_e, visit_nb, n_used = _route(te[:, :TOP_K], n_vis)
    dest3 = (dest * TOK_SUB).reshape(T // min(MOE_TT, T), 1, -1)
    xs = _dispatch(visit_nb, dest3, h3p, n_vis * MOE_R)
    yrows = _experts(visit_e, visit_nb, n_used, xs, w_gu[l], b_gu[l].reshape(N_EXPERTS, 1, 2 * D_FF),
                     w_down[l], b_down[l].reshape(N_EXPERTS, 1, D))
    out = _combine(dest3, tg, x2, yrows, final_norm_w.reshape(1, D))
    return out.reshape(B, S, D)
```

```python
import functools

import jax
import jax.numpy as jnp
from jax import lax
from jax.experimental import pallas as pl
from jax.experimental.pallas import tpu as pltpu

F32 = jnp.float32
BF16 = jnp.bfloat16
I32 = jnp.int32
U32 = jnp.uint32

D_MODEL = 2048
CHUNK = 64
EPS = 1e-6
M_HEADS = 4
M_HEAD_DIM = 256
M_WIDTH = M_HEADS * M_HEAD_DIM
CONV_WIDTH = 4
A_HEADS = 16
A_HEAD_DIM = 64
A_WIDTH = A_HEADS * A_HEAD_DIM
LEFT_CHUNKS = 8
MAX_REL = 128
X_HEADS = 4
X_HEAD_DIM = 128
X_WIDTH = X_HEADS * X_HEAD_DIM
N_EXPERTS = 32
TOP_K = 4
D_FF = D_MODEL
SWIGLU_LIMIT = 7.0
SWIGLU_ALPHA = 1.702

LANES_V7X = 128
SUBLANES_V7X = 8
VMEM_BYTES_V7X = 64 * 1024 * 1024
VMEM_LIMIT_CAP = 60000 * 1024
DMA_PRIORITIES = 2

NEG_BIG = -1e30

COL_GM = 0
COL_GA = 2048
COL_QM = 4096
COL_KM = 5120
COL_VM = 6144
COL_OM = 7168
COL_QA = 8192
COL_KA = 9216
COL_VA = 10240
N_MAIN = 11264

IN_TM = 1024
IN_TN = 1024
ATT_TQ = 512
ATT_SUB = 128
ATT_WIN = ATT_SUB + LEFT_CHUNKS * CHUNK
MERGE_TM = 512
XATT_TM = 512
MOE_BLK = 256
MOE_G = 5
MOE_R = MOE_G * MOE_BLK
MOE_BIG = 4
MOE_TF = 256
MOE_TT = 256


def _vmem_limit(nbytes):
    return int(min(VMEM_LIMIT_CAP, max(nbytes, 16 * 1024 * 1024)))


def _nt_dot(a, b):
    return lax.dot_general(a, b, (((1,), (1,)), ((), ())), preferred_element_type=F32)


def _tn_dot(a, b):
    return lax.dot_general(a, b, (((0,), (0,)), ((), ())), preferred_element_type=F32)


def _sigmoid(x):
    return 1.0 / (1.0 + jnp.exp(-x))


def _log_sigmoid(x):
    return jnp.minimum(x, 0.0) - jnp.log1p(jnp.exp(-jnp.abs(x)))


def _rms(x, w):
    return x * lax.rsqrt(jnp.mean(x * x, axis=-1, keepdims=True) + EPS) * w


def _pack_bf16_pairs(xb):
    n = xb.shape[1] // 2
    hi = lax.bitcast_convert_type(xb[:, :n].astype(F32), U32)
    lo = lax.bitcast_convert_type(xb[:, n:].astype(F32), U32)
    return hi | (lo >> 16)


def _unpack_bf16_pairs(w):
    hi = lax.bitcast_convert_type(w & jnp.uint32(0xFFFF0000), F32)
    lo = lax.bitcast_convert_type(w << 16, F32)
    return hi, lo


TOK_SUB = SUBLANES_V7X
TOK_WORDS = TOK_SUB * LANES_V7X


def _store_token_tiles(ref, tok0, packed):
    n = packed.shape[0]
    for s in range(TOK_SUB):
        ref[pl.ds(tok0 * TOK_SUB + s, n, stride=TOK_SUB), :] = packed[:, s * LANES_V7X:(s + 1) * LANES_V7X]


def _load_token_slab(ref, tok0, n, s):
    return ref[pl.ds(tok0 * TOK_SUB + s, n, stride=TOK_SUB), :]


def _inproj_kernel(x_ref, nw_ref, w_ref, wif_ref, o_ref, if_ref, h_scr):
    @pl.when(pl.program_id(1) == 0)
    def _():
        hb = _rms(x_ref[...], nw_ref[...]).astype(BF16)
        h_scr[...] = hb
        if_ref[...] = jnp.dot(hb, wif_ref[...], preferred_element_type=F32)

    o_ref[...] = jnp.dot(h_scr[...], w_ref[...], preferred_element_type=F32).astype(o_ref.dtype)


def _inproj(x2d, norm_w, w_main, w_if):
    T, D = x2d.shape
    tm = min(IN_TM, T)
    grid = (T // tm, N_MAIN // IN_TN)
    vmem = 2 * tm * D * 4 + tm * D * 2 + 2 * D * IN_TN * 2 + 2 * tm * IN_TN * 2 + 4 * tm * LANES_V7X * 4
    return pl.pallas_call(
        _inproj_kernel,
        out_shape=(jax.ShapeDtypeStruct((T, N_MAIN), BF16), jax.ShapeDtypeStruct((T, LANES_V7X), F32)),
        grid=grid,
        in_specs=[
            pl.BlockSpec((tm, D), lambda i, j: (i, 0)),
            pl.BlockSpec((1, D), lambda i, j: (0, 0)),
            pl.BlockSpec((D, IN_TN), lambda i, j: (0, j)),
            pl.BlockSpec((D, LANES_V7X), lambda i, j: (0, 0)),
        ],
        out_specs=(
            pl.BlockSpec((tm, IN_TN), lambda i, j: (i, j)),
            pl.BlockSpec((tm, LANES_V7X), lambda i, j: (i, 0)),
        ),
        scratch_shapes=[pltpu.VMEM((tm, D), BF16)],
        compiler_params=pltpu.CompilerParams(
            dimension_semantics=("parallel", "arbitrary"),
            vmem_limit_bytes=_vmem_limit(vmem + (8 << 20))),
        name="inproj",
    )(x2d, norm_w, w_main, w_if)


def _mlstm_kernel(q_ref, k_ref, v_ref, o_ref, gc_ref, gr_ref, cw_ref, cb_ref, bc_ref, br_ref, nw_ref,
                  out_ref, tail, c_st, n_st, m_st):
    nb = q_ref.shape[0]
    L = CHUNK
    dh = M_HEAD_DIM
    halo = tail.shape[1]

    @pl.when(pl.program_id(0) == 0)
    def _():
        tail[...] = jnp.zeros_like(tail)
        c_st[...] = jnp.zeros_like(c_st)
        n_st[...] = jnp.zeros_like(n_st)
        m_st[...] = jnp.zeros_like(m_st)

    row = lax.broadcasted_iota(I32, (L, L), 0)
    col = lax.broadcasted_iota(I32, (L, L), 1)
    lower = col <= row

    srow = lax.broadcasted_iota(I32, (L, halo + L), 0)
    scol = lax.broadcasted_iota(I32, (L, halo + L), 1)
    shift_mat = jnp.concatenate(
        [jnp.where(scol == srow + (halo - (CONV_WIDTH - 1) + j), 1.0, 0.0) for j in range(CONV_WIDTH - 1)],
        axis=0).astype(BF16)

    erow = lax.broadcasted_iota(I32, (dh, dh), 0)
    ecol = lax.broadcasted_iota(I32, (dh, dh), 1)
    eye = jnp.where(erow == ecol, 1.0, 0.0).astype(BF16)

    streams = [(b, h) for b in range(nb) for h in range(M_HEADS)]

    qk = []
    for b in range(nb):
        halves = []
        for half, ref in enumerate((q_ref, k_ref)):
            cols = slice(half * M_WIDTH, (half + 1) * M_WIDTH)
            cur = ref[b]
            ext = jnp.concatenate([tail[b, :, cols], cur], axis=0)
            back = jnp.dot(shift_mat, ext, preferred_element_type=F32)
            conv = cb_ref[:, cols] + cur.astype(F32) * cw_ref[CONV_WIDTH - 1:CONV_WIDTH, cols]
            for j in range(CONV_WIDTH - 1):
                conv = conv + back[j * L:(j + 1) * L] * cw_ref[j:j + 1, cols]
            halves.append(conv * _sigmoid(conv))
            tail[b, :, cols] = cur[L - halo:L]
        qk.append(halves)

    st = []
    for b, h in streams:
        s = b * M_HEADS + h
        lo, hi = h * dh, (h + 1) * dh
        gcol = gc_ref[b] + bc_ref[...]
        grow = gr_ref[b, 0] + br_ref[...]
        q = qk[b][0][:, lo:hi]
        k = qk[b][1][:, lo:hi] * (dh ** -0.5)
        ig_c = gcol[:, h:h + 1]
        ig_r = grow[h:h + 1, :]
        ls_c = _log_sigmoid(gcol[:, M_HEADS + h:M_HEADS + h + 1])
        ls_r = _log_sigmoid(grow[M_HEADS + h:M_HEADS + h + 1, :])
        b_c = jnp.sum(jnp.where(lower, ls_r, 0.0), axis=1, keepdims=True)
        b_r = jnp.sum(jnp.where(row <= col, ls_c, 0.0), axis=0, keepdims=True)
        g = jnp.sum(ls_r, axis=1, keepdims=True)
        m_old = m_st[s][:, 0:1]
        w_end = g - b_c + ig_c
        a = jnp.max(w_end, axis=0, keepdims=True)
        dlog = jnp.where(lower, b_c - b_r + ig_r, NEG_BIG)
        inter = b_c + m_old
        m_j = jnp.maximum(inter, jnp.max(dlog, axis=1, keepdims=True))
        st.append(dict(s=s, lo=lo, hi=hi, b=b, q=q, k=k, qb=q.astype(BF16), kb=k.astype(BF16),
                       vb=v_ref[b, :, lo:hi], g=g, a=a, m_old=m_old, m_j=m_j,
                       e_end=jnp.exp(w_end - a), decay=jnp.exp(dlog - m_j), w_int=jnp.exp(inter - m_j),
                       c_old=c_st[s], n_old=n_st[s]))

    for d in st:
        d["sm"] = _nt_dot(d["qb"], d["kb"]) * d["decay"]
    for d in st:
        d["num"] = (jnp.dot(d["sm"].astype(BF16), d["vb"], preferred_element_type=F32)
                    + d["w_int"] * _nt_dot(d["qb"], d["c_old"].astype(BF16)))
    for d in st:
        den = (jnp.sum(d["sm"], axis=1, keepdims=True)
               + d["w_int"] * jnp.sum(d["q"] * d["n_old"], axis=1, keepdims=True))
        hh = d["num"] / jnp.maximum(jnp.abs(den), jnp.exp(-d["m_j"]))
        hn = _rms(hh, nw_ref[:, d["lo"]:d["hi"]])
        og = _sigmoid(o_ref[d["b"], :, d["lo"]:d["hi"]].astype(F32))
        out_ref[d["b"], :, d["lo"]:d["hi"]] = (og * hn).astype(out_ref.dtype)
    for d in st:
        s = d["s"]
        m_new = jnp.maximum(d["g"] + d["m_old"], d["a"])
        s_prev = jnp.exp(d["g"] + d["m_old"] - m_new)
        s_loc = jnp.exp(d["a"] - m_new)
        ve = (d["vb"].astype(F32) * (d["e_end"] * s_loc)).astype(BF16)
        ve_t = _nt_dot(eye, ve).astype(BF16)
        c_st[s] = s_prev * d["c_old"] + jnp.dot(ve_t, d["kb"], preferred_element_type=F32)
        n_st[s] = s_prev * d["n_old"] + s_loc * jnp.sum(d["e_end"] * d["k"], axis=0, keepdims=True)
        m_st[s] = jnp.broadcast_to(m_new, (1, LANES_V7X))


def _mlstm(p3, gcol3, grow4, conv_w, conv_b, bias_col, bias_row, head_w):
    B, S, _ = p3.shape
    nc = S // CHUNK
    ns = B * M_HEADS
    blk = lambda cb: pl.BlockSpec((B, CHUNK, M_WIDTH), lambda c, cb=cb: (0, c, cb))
    full = lambda shape: pl.BlockSpec(shape, lambda c: (0,) * len(shape))
    vmem = (2 * 5 * B * CHUNK * M_WIDTH * 2 + B * (CHUNK + 8) * 2 * M_WIDTH * 4
            + ns * M_HEAD_DIM * M_HEAD_DIM * 4 + (16 << 20))
    return pl.pallas_call(
        _mlstm_kernel,
        out_shape=jax.ShapeDtypeStruct((B, S, M_WIDTH), BF16),
        grid=(nc,),
        in_specs=[
            blk(COL_QM // M_WIDTH), blk(COL_KM // M_WIDTH), blk(COL_VM // M_WIDTH), blk(COL_OM // M_WIDTH),
            pl.BlockSpec((B, CHUNK, LANES_V7X), lambda c: (0, c, 0)),
            pl.BlockSpec((B, 1, 2 * M_HEADS, CHUNK), lambda c: (0, c, 0, 0)),
            full((CONV_WIDTH, 2 * M_WIDTH)), full((1, 2 * M_WIDTH)),
            full((1, LANES_V7X)), full((2 * M_HEADS, CHUNK)), full((1, M_WIDTH)),
        ],
        out_specs=pl.BlockSpec((B, CHUNK, M_WIDTH), lambda c: (0, c, 0)),
        scratch_shapes=[
            pltpu.VMEM((B, 2 * SUBLANES_V7X, 2 * M_WIDTH), BF16),
            pltpu.VMEM((ns, M_HEAD_DIM, M_HEAD_DIM), F32),
            pltpu.VMEM((ns, 1, M_HEAD_DIM), F32),
            pltpu.VMEM((ns, 1, LANES_V7X), F32),
        ],
        compiler_params=pltpu.CompilerParams(
            dimension_semantics=("arbitrary",), vmem_limit_bytes=_vmem_limit(vmem)),
        name="mlstm",
    )(p3, p3, p3, p3, gcol3, grow4, conv_w, conv_b, bias_col, bias_row, head_w)


def _band_attn_kernel(q_ref, kp_ref, kc_ref, vp_ref, vc_ref, bias_ref, out_ref):
    qb_idx = pl.program_id(2)
    tq = q_ref.shape[1]
    q2 = q_ref[0]
    kcat = jnp.concatenate([kp_ref[0], kc_ref[0]], axis=0)
    vcat = jnp.concatenate([vp_ref[0], vc_ref[0]], axis=0)
    lane = lax.broadcasted_iota(I32, (1, LANES_V7X), 1)
    wcol = lax.broadcasted_iota(I32, (1, ATT_WIN), 1)
    first_pen = jnp.where(qb_idx == 0, NEG_BIG, 0.0)
    scale = A_HEAD_DIM ** -0.5
    bias = bias_ref[0]

    subs = []
    for t in range(tq // ATT_SUB):
        r0 = t * ATT_SUB
        w0 = tq + r0 - LEFT_CHUNKS * CHUNK
        pen = jnp.where(wcol < (tq - w0), first_pen, 0.0)
        qs = q2[r0:r0 + ATT_SUB] * scale
        zero = jnp.zeros_like(qs)
        qst = jnp.concatenate([jnp.where(lane < A_HEAD_DIM, qs, zero),
                               jnp.where(lane >= A_HEAD_DIM, qs, zero)], axis=0)
        subs.append(dict(r0=r0, w0=w0, pen=pen, qst=qst))
    for d in subs:
        d["sc"] = _nt_dot(d["qst"], kcat[d["w0"]:d["w0"] + ATT_WIN]) + bias + d["pen"]
    for d in subs:
        p = jnp.exp(d["sc"] - jnp.max(d["sc"], axis=1, keepdims=True))
        d["den"] = jnp.sum(p, axis=1, keepdims=True)
        d["p"] = p.astype(BF16)
    for d in subs:
        d["o"] = jnp.dot(d["p"], vcat[d["w0"]:d["w0"] + ATT_WIN], preferred_element_type=F32) / d["den"]
    for d in subs:
        o, r0 = d["o"], d["r0"]
        out_ref[0, r0:r0 + ATT_SUB, :] = jnp.where(
            lane < A_HEAD_DIM, o[:ATT_SUB], o[ATT_SUB:]).astype(out_ref.dtype)


def _band_attn(p3, bias):
    B, S, _ = p3.shape
    tq = ATT_TQ
    nq = S // tq
    nhp = A_HEADS // 2
    cq, ck, cv = COL_QA // LANES_V7X, COL_KA // LANES_V7X, COL_VA // LANES_V7X
    cur = lambda c0: pl.BlockSpec((1, tq, LANES_V7X), lambda b, hp, i, c0=c0: (b, i, c0 + hp))
    prev = lambda c0: pl.BlockSpec((1, tq, LANES_V7X), lambda b, hp, i, c0=c0: (b, jnp.maximum(i - 1, 0), c0 + hp))
    return pl.pallas_call(
        _band_attn_kernel,
        out_shape=jax.ShapeDtypeStruct((B, S, A_WIDTH), BF16),
        grid=(B, nhp, nq),
        in_specs=[cur(cq), prev(ck), cur(ck), prev(cv), cur(cv),
                  pl.BlockSpec((1, 2 * ATT_SUB, ATT_WIN), lambda b, hp, i: (hp, 0, 0))],
        out_specs=pl.BlockSpec((1, tq, LANES_V7X), lambda b, hp, i: (b, i, hp)),
        compiler_params=pltpu.CompilerParams(
            dimension_semantics=("parallel", "parallel", "arbitrary"),
            vmem_limit_bytes=_vmem_limit(32 << 20)),
        name="band_attn",
    )(p3, p3, p3, p3, p3, bias)


def _band_bias(rel_table):
    n = 6 * LANES_V7X
    c = jnp.arange(n)
    d = jnp.where(c < ATT_WIN, c, c - n)
    rel = jnp.clip(LEFT_CHUNKS * CHUNK - d, -MAX_REL, MAX_REL) + MAX_REL
    vec = rel_table[:, rel].astype(F32)
    sheared = jnp.tile(vec, (1, ATT_SUB))[:, :ATT_SUB * (n - 1)].reshape(A_HEADS, ATT_SUB, n - 1)
    i = jnp.arange(ATT_SUB)[:, None]
    kp = jnp.arange(ATT_WIN)[None, :]
    ci, kc = i // CHUNK, kp // CHUNK
    valid = (kc >= ci) & (kc <= ci + LEFT_CHUNKS)
    bias = jnp.where(valid[None], sheared[:, :, :ATT_WIN], NEG_BIG)
    return bias.reshape(A_HEADS // 2, 2 * ATT_SUB, ATT_WIN)


def _merge_kernel(hm_ref, ya_ref, gm_ref, ga_ref, x_ref, wbm_ref, wba_ref, wo_ref, out_ref):
    ym = jnp.dot(hm_ref[...], wbm_ref[...], preferred_element_type=F32)
    ya = jnp.dot(ya_ref[...], wba_ref[...], preferred_element_type=F32)
    merged = _sigmoid(gm_ref[...].astype(F32)) * ym + _sigmoid(ga_ref[...].astype(F32)) * ya
    out_ref[...] = x_ref[...] + jnp.dot(merged.astype(BF16), wo_ref[...], preferred_element_type=F32)


def _const_spec(shape):
    return pl.BlockSpec(shape, lambda i: (0,) * len(shape), pipeline_mode=pl.Buffered(1))


def _merge(hm2d, ya2d, p2d, x2d, wbm, wba, wout):
    T, D = x2d.shape
    tm = min(MERGE_TM, T)
    vmem = ((2 * M_WIDTH * D + D * D) * 2 + 2 * (2 * tm * M_WIDTH * 2 + 2 * tm * D * 2 + 2 * tm * D * 4)
            + 6 * tm * D * 4)
    return pl.pallas_call(
        _merge_kernel,
        out_shape=jax.ShapeDtypeStruct((T, D), F32),
        grid=(T // tm,),
        in_specs=[
            pl.BlockSpec((tm, M_WIDTH), lambda i: (i, 0)),
            pl.BlockSpec((tm, A_WIDTH), lambda i: (i, 0)),
            pl.BlockSpec((tm, D), lambda i: (i, COL_GM // D)),
            pl.BlockSpec((tm, D), lambda i: (i, COL_GA // D)),
            pl.BlockSpec((tm, D), lambda i: (i, 0)),
            _const_spec((M_WIDTH, D)), _const_spec((A_WIDTH, D)), _const_spec((D, D)),
        ],
        out_specs=pl.BlockSpec((tm, D), lambda i: (i, 0)),
        compiler_params=pltpu.CompilerParams(
            dimension_semantics=("parallel",), vmem_limit_bytes=_vmem_limit(vmem)),
        name="merge_out",
    )(hm2d, ya2d, p2d, p2d, x2d, wbm, wba, wout)


def _memkv_kernel(mem_ref, nw_ref, w_ref, k_ref, v_ref):
    hb = _rms(mem_ref[0], nw_ref[...]).astype(BF16)
    kv = jnp.dot(hb, w_ref[...], preferred_element_type=F32)
    k_ref[0] = kv[:, :X_WIDTH].astype(k_ref.dtype)
    v_ref[0] = kv[:, X_WIDTH:].astype(v_ref.dtype)


def _memkv(mem, norm_w, wkv):
    B, N, D = mem.shape
    return pl.pallas_call(
        _memkv_kernel,
        out_shape=(jax.ShapeDtypeStruct((B, N, X_WIDTH), BF16), jax.ShapeDtypeStruct((B, N, X_WIDTH), BF16)),
        grid=(B,),
        in_specs=[pl.BlockSpec((1, N, D), lambda b: (b, 0, 0)),
                  pl.BlockSpec((1, D), lambda b: (0, 0)),
                  pl.BlockSpec((D, 2 * X_WIDTH), lambda b: (0, 0))],
        out_specs=(pl.BlockSpec((1, N, X_WIDTH), lambda b: (b, 0, 0)),
                   pl.BlockSpec((1, N, X_WIDTH), lambda b: (b, 0, 0))),
        compiler_params=pltpu.CompilerParams(
            dimension_semantics=("parallel",), vmem_limit_bytes=_vmem_limit(24 << 20)),
        name="mem_kv",
    )(mem, norm_w, wkv)


def _xattn_kernel(x_ref, nxw_ref, wq_ref, km_ref, vm_ref, wo_ref, nfw_ref, wrh_ref, wrl_ref, br_ref,
                  x2_ref, h3_ref, te_ref, tg_ref):
    x1 = x_ref[...]
    hb = _rms(x1, nxw_ref[...]).astype(BF16)
    q = jnp.dot(hb, wq_ref[...], preferred_element_type=F32).astype(BF16)
    km = km_ref[0]
    vm = vm_ref[0]
    outs = []
    for h in range(X_HEADS):
        lo, hi = h * X_HEAD_DIM, (h + 1) * X_HEAD_DIM
        sc = _nt_dot(q[:, lo:hi], km[:, lo:hi]) * (X_HEAD_DIM ** -0.5)
        mx = jnp.max(sc, axis=1, keepdims=True)
        p = jnp.exp(sc - mx)
        den = jnp.sum(p, axis=1, keepdims=True)
        outs.append(jnp.dot(p.astype(BF16), vm[:, lo:hi], preferred_element_type=F32) / den)
    o = jnp.concatenate(outs, axis=1).astype(BF16)
    x2 = x1 + jnp.dot(o, wo_ref[...], preferred_element_type=F32)
    x2_ref[...] = x2

    h3 = _rms(x2, nfw_ref[...])
    h_hi = h3.astype(BF16)
    _store_token_tiles(h3_ref, 0, _pack_bf16_pairs(h_hi))
    h_lo = (h3 - h_hi.astype(F32)).astype(BF16)
    logits = (jnp.dot(h_hi, wrh_ref[...], preferred_element_type=F32)
              + jnp.dot(h_lo, wrh_ref[...], preferred_element_type=F32)
              + jnp.dot(h_hi, wrl_ref[...], preferred_element_type=F32)
              + br_ref[...])

    colf = lax.broadcasted_iota(I32, logits.shape, 1).astype(F32)
    work = logits
    vals, idxs = [], []
    for _ in range(TOP_K):
        mx = jnp.max(work, axis=1, keepdims=True)
        ix = jnp.min(jnp.where(work == mx, colf, float(LANES_V7X)), axis=1, keepdims=True)
        vals.append(mx)
        idxs.append(ix)
        work = jnp.where(colf == ix, -jnp.inf, work)
    es = [jnp.exp(v - vals[0]) for v in vals]
    tot = es[0] + es[1] + es[2] + es[3]
    te = jnp.zeros_like(logits)
    tg = jnp.zeros_like(logits)
    for kk in range(TOP_K):
        te = jnp.where(colf == float(kk), idxs[kk], te)
        tg = jnp.where(colf == float(kk), es[kk] / tot, tg)
    te_ref[...] = te.astype(I32)
    tg_ref[...] = tg


def _xattn(x1, batch_rows, norm_x_w, wq, km, vm, wo, norm_ffn_w, wr_hi, wr_lo, br):
    T, D = x1.shape
    tm = min(XATT_TM, batch_rows)
    per_b = batch_rows // tm
    n_mem = km.shape[1]
    row = lambda w: pl.BlockSpec((tm, w), lambda i: (i, 0))
    vmem = (2 * D * X_WIDTH * 2 + 2 * D * LANES_V7X * 2 + 4 * n_mem * X_WIDTH * 2
            + 2 * 3 * tm * D * 4 + 8 * tm * D * 4)
    return pl.pallas_call(
        _xattn_kernel,
        out_shape=(jax.ShapeDtypeStruct((T, D), F32), jax.ShapeDtypeStruct((T * TOK_SUB, LANES_V7X), U32),
                   jax.ShapeDtypeStruct((T, LANES_V7X), I32), jax.ShapeDtypeStruct((T, LANES_V7X), F32)),
        grid=(T // tm,),
        in_specs=[
            row(D), _const_spec((1, D)), _const_spec((D, X_WIDTH)),
            pl.BlockSpec((1, n_mem, X_WIDTH), lambda i: (i // per_b, 0, 0)),
            pl.BlockSpec((1, n_mem, X_WIDTH), lambda i: (i // per_b, 0, 0)),
            _const_spec((X_WIDTH, D)), _const_spec((1, D)),
            _const_spec((D, LANES_V7X)), _const_spec((D, LANES_V7X)), _const_spec((1, LANES_V7X)),
        ],
        out_specs=(row(D), pl.BlockSpec((tm * TOK_SUB, LANES_V7X), lambda i: (i, 0)),
                   row(LANES_V7X), row(LANES_V7X)),
        compiler_params=pltpu.CompilerParams(
            dimension_semantics=("parallel",), vmem_limit_bytes=_vmem_limit(vmem)),
        name="xattn_router",
    )(x1, norm_x_w, wq, km, vm, wo, norm_ffn_w, wr_hi, wr_lo, br)


def _dispatch_kernel(nb_ref, dest_ref, h_ref, xs_ref, zero_scr, sem, zsem):
    tt = h_ref.shape[0] // TOK_SUB
    nv = nb_ref.shape[0]

    @pl.when(pl.program_id(0) == 0)
    def _():
        zero_scr[...] = jnp.zeros_like(zero_scr)

        def zero_copy(v, b):
            row0 = pl.multiple_of((v * MOE_R + b * MOE_BLK) * TOK_SUB, MOE_BLK * TOK_SUB)
            return pltpu.make_async_copy(zero_scr, xs_ref.at[pl.ds(row0, MOE_BLK * TOK_SUB)], zsem)

        def each_partial_block(v, fn):
            for b in range(MOE_G):
                @pl.when(b >= nb_ref[v] - 1)
                def _():
                    fn(zero_copy(v, b))

        def zstart(v, carry):
            each_partial_block(v, lambda c: c.start())
            return carry

        def zwait(v, carry):
            each_partial_block(v, lambda c: c.wait())
            return carry

        lax.fori_loop(0, nv, zstart, 0)
        lax.fori_loop(0, nv, zwait, 0)

    def tile_copy(t, kk):
        src = h_ref.at[pl.ds(pl.multiple_of(t * TOK_SUB, TOK_SUB), TOK_SUB)]
        dst = xs_ref.at[pl.ds(pl.multiple_of(dest_ref[0, t * TOP_K + kk], TOK_SUB), TOK_SUB)]
        return pltpu.make_async_copy(src, dst, sem)

    def start(t, carry):
        for kk in range(TOP_K):
            tile_copy(t, kk).start(priority=kk % DMA_PRIORITIES)
        return carry

    def wait(t, carry):
        for kk in range(TOP_K):
            tile_copy(t, kk).wait()
        return carry

    lax.fori_loop(0, tt, start, 0)
    lax.fori_loop(0, tt, wait, 0)


def _dispatch(visit_nb, dest3, h3p, n_rows):
    T = h3p.shape[0] // TOK_SUB
    tt = min(MOE_TT, T)
    grid_spec = pltpu.PrefetchScalarGridSpec(
        num_scalar_prefetch=1,
        grid=(T // tt,),
        in_specs=[
            pl.BlockSpec((None, 1, tt * TOP_K), lambda i, nb: (i, 0, 0), memory_space=pltpu.SMEM),
            pl.BlockSpec((tt * TOK_SUB, LANES_V7X), lambda i, nb: (i, 0)),
        ],
        out_specs=pl.BlockSpec(memory_space=pl.ANY),
        scratch_shapes=[pltpu.VMEM((MOE_BLK * TOK_SUB, LANES_V7X), U32),
                        pltpu.SemaphoreType.DMA(()), pltpu.SemaphoreType.DMA(())],
    )
    return pl.pallas_call(
        _dispatch_kernel,
        out_shape=jax.ShapeDtypeStruct((n_rows * TOK_SUB, LANES_V7X), U32),
        grid_spec=grid_spec,
        compiler_params=pltpu.CompilerParams(
            dimension_semantics=("arbitrary",), has_side_effects=True,
            vmem_limit_bytes=_vmem_limit(4 * tt * TOK_WORDS * 4 + MOE_BLK * TOK_WORDS * 4 + (8 << 20))),
        name="moe_dispatch",
    )(visit_nb, dest3, h3p)


def _experts_kernel(ve_ref, nb_ref, nu_ref, x_ref, bgu_ref, bd_ref, wgu_hbm, wdn_hbm, out_ref,
                    xb_scr, acc_scr, wg_buf, wu_buf, wd_buf, wgb_scr, wub_scr, wdb_scr, sems):
    v = pl.program_id(0)
    nb = nb_ref[v]
    n_used = nu_ref[0]
    half = TOK_WORDS
    tf = wg_buf.shape[2]
    nj = D_FF // tf

    def block_rows(b):
        return pl.ds(pl.multiple_of(b * MOE_BLK, MOE_BLK), MOE_BLK)

    def tile_copies(e, j, slot):
        cols = pl.ds(pl.multiple_of(j * tf, tf), tf)
        ucols = pl.ds(pl.multiple_of(D_FF + j * tf, tf), tf)
        return (pltpu.make_async_copy(wgu_hbm.at[e, :, cols], wg_buf.at[slot], sems.at[slot, 0]),
                pltpu.make_async_copy(wgu_hbm.at[e, :, ucols], wu_buf.at[slot], sems.at[slot, 1]),
                pltpu.make_async_copy(wdn_hbm.at[e, cols, :], wd_buf.at[slot], sems.at[slot, 2]))

    def fetch(e, j, slot):
        for c in tile_copies(e, j, slot):
            c.start()

    @pl.when(nb > 0)
    def _():
        e = ve_ref[v]

        @pl.when(v == 0)
        def _():
            fetch(e, 0, 0)

        def stage(b, carry):
            rows = block_rows(b)
            for s in range(TOK_SUB):
                hi, lo = _unpack_bf16_pairs(_load_token_slab(x_ref, b * MOE_BLK, MOE_BLK, s))
                xb_scr[rows, s * LANES_V7X:(s + 1) * LANES_V7X] = hi.astype(BF16)
                xb_scr[rows, half + s * LANES_V7X:half + (s + 1) * LANES_V7X] = lo.astype(BF16)
            acc_scr[rows, :] = jnp.zeros((MOE_BLK, acc_scr.shape[1]), F32)
            return carry
        lax.fori_loop(0, nb, stage, 0)

        full = nb == MOE_G
        big = jnp.logical_and(nb >= MOE_BIG, jnp.logical_not(full))

        def tile_step(j, carry):
            slot = lax.rem(j, 2)

            @pl.when(j + 1 < nj)
            def _():
                fetch(e, j + 1, 1 - slot)

            @pl.when(jnp.logical_and(j + 1 == nj, v + 1 < n_used))
            def _():
                fetch(ve_ref[v + 1], 0, 1 - slot)

            for c in tile_copies(e, j, slot):
                c.wait()

            def ffn_rows(rows, wg, wu, wd):
                xb = xb_scr[rows, :]
                g = jnp.dot(xb, wg, preferred_element_type=F32) + bgu_ref[pl.ds(j, 1), :]
                u = jnp.dot(xb, wu, preferred_element_type=F32) + bgu_ref[pl.ds(nj + j, 1), :]
                g = jnp.minimum(g, SWIGLU_LIMIT)
                u = jnp.clip(u, -SWIGLU_LIMIT, SWIGLU_LIMIT)
                act = (u + 1.0) * (g * _sigmoid(SWIGLU_ALPHA * g))
                acc_scr[rows, :] += jnp.dot(act.astype(BF16), wd, preferred_element_type=F32)

            def cast_weights():
                wg = wg_buf[slot].astype(BF16)
                wu = wu_buf[slot].astype(BF16)
                wd = wd_buf[slot].astype(BF16)
                wgb_scr[...] = wg
                wub_scr[...] = wu
                wdb_scr[...] = wd
                return wg, wu, wd

            @pl.when(full)
            def _():
                ffn_rows(pl.ds(0, MOE_R), *cast_weights())

            @pl.when(big)
            def _():
                ffn_rows(pl.ds(0, MOE_BIG * MOE_BLK), *cast_weights())

            @pl.when(nb < MOE_BIG)
            def _():
                cast_weights()

            def block(b, c2):
                ffn_rows(block_rows(b), wgb_scr[...], wub_scr[...], wdb_scr[...])
                return c2
            lax.fori_loop(jnp.where(full, MOE_G, jnp.where(big, MOE_BIG, 0)), nb, block, 0)
            return carry
        lax.fori_loop(0, nj, tile_step, 0)

        def finish(b, carry):
            y = (acc_scr[block_rows(b), :] + bd_ref[...]).astype(BF16)
            _store_token_tiles(out_ref, b * MOE_BLK, _pack_bf16_pairs(y))
            return carry
        lax.fori_loop(0, nb, finish, 0)

        def blank(b, carry):
            rows = pl.ds(pl.multiple_of(b * MOE_BLK * TOK_SUB, MOE_BLK * TOK_SUB), MOE_BLK * TOK_SUB)
            out_ref[rows, :] = jnp.zeros((MOE_BLK * TOK_SUB, LANES_V7X), U32)
            return carry
        lax.fori_loop(nb, MOE_G, blank, 0)

    @pl.when(nb == 0)
    def _():
        out_ref[...] = jnp.zeros_like(out_ref)


def _experts(visit_e, visit_nb, n_used, xs, wgu, bgu, wdn, bdn):
    n_rows = xs.shape[0] // TOK_SUB
    half = TOK_WORDS
    D = 2 * half
    tf = MOE_TF
    nj = D_FF // tf
    nv = n_rows // MOE_R
    assert nj % 2 == 0

    def vrow(v, nu):
        return jnp.minimum(v, nu[0] - 1)

    grid_spec = pltpu.PrefetchScalarGridSpec(
        num_scalar_prefetch=3,
        grid=(nv,),
        in_specs=[
            pl.BlockSpec((MOE_R * TOK_SUB, LANES_V7X), lambda v, ve, nb, nu: (vrow(v, nu), 0)),
            pl.BlockSpec((None, 2 * nj, tf), lambda v, ve, nb, nu: (ve[v], 0, 0)),
            pl.BlockSpec((None, 1, D), lambda v, ve, nb, nu: (ve[v], 0, 0)),
            pl.BlockSpec(memory_space=pl.ANY),
            pl.BlockSpec(memory_space=pl.ANY),
        ],
        out_specs=pl.BlockSpec((MOE_R * TOK_SUB, LANES_V7X), lambda v, ve, nb, nu: (v, 0)),
        scratch_shapes=[pltpu.VMEM((MOE_R, D), BF16), pltpu.VMEM((MOE_R, D), F32),
                        pltpu.VMEM((2, D, tf), F32), pltpu.VMEM((2, D, tf), F32), pltpu.VMEM((2, tf, D), F32),
                        pltpu.VMEM((D, tf), BF16), pltpu.VMEM((D, tf), BF16), pltpu.VMEM((tf, D), BF16),
                        pltpu.SemaphoreType.DMA((2, 3))],
    )
    vmem = (2 * MOE_R * half * 4 + MOE_R * D * 2 + MOE_R * D * 4 + 2 * MOE_R * half * 4
            + 2 * 3 * D * tf * 4 + 3 * D * tf * 2 + 8 * MOE_BLK * D * 4)
    return pl.pallas_call(
        _experts_kernel,
        out_shape=jax.ShapeDtypeStruct((n_rows * TOK_SUB, LANES_V7X), U32),
        grid_spec=grid_spec,
        compiler_params=pltpu.CompilerParams(
            dimension_semantics=("arbitrary",), vmem_limit_bytes=_vmem_limit(vmem)),
        name="moe_experts",
    )(visit_e, visit_nb, n_used, xs, bgu, bdn, wgu, wdn)


def _combine_kernel(dcur_ref, dnext_ref, gate_ref, x_ref, y_ref, nw_ref, out_ref, buf, y_scr, sems):
    i = pl.program_id(0)
    n = pl.num_programs(0)
    tt = x_ref.shape[0]
    half = TOK_WORDS
    slot = lax.rem(i, 2)

    def row_copy(dref, s, t, kk):
        src = y_ref.at[pl.ds(pl.multiple_of(dref[0, t * TOP_K + kk], TOK_SUB), TOK_SUB)]
        dst = buf.at[s, kk, pl.ds(pl.multiple_of(t * TOK_SUB, TOK_SUB), TOK_SUB)]
        return pltpu.make_async_copy(src, dst, sems.at[s])

    def issue(dref, s):
        def body(t, carry):
            for kk in range(TOP_K):
                row_copy(dref, s, t, kk).start(priority=kk % DMA_PRIORITIES)
            return carry
        lax.fori_loop(0, tt, body, 0)

    @pl.when(i == 0)
    def _():
        issue(dcur_ref, 0)

    @pl.when(i + 1 < n)
    def _():
        issue(dnext_ref, 1 - slot)

    def wait(t, carry):
        for kk in range(TOP_K):
            row_copy(dcur_ref, slot, t, kk).wait()
        return carry
    lax.fori_loop(0, tt, wait, 0)

    gate = gate_ref[...]
    gks = [jnp.broadcast_to(gate[:, kk:kk + 1], (tt, LANES_V7X)) for kk in range(TOP_K)]
    for s in range(TOK_SUB):
        ca = pl.ds(s * LANES_V7X, LANES_V7X)
        cb = pl.ds(half + s * LANES_V7X, LANES_V7X)
        ya = x_ref[:, ca]
        yb = x_ref[:, cb]
        for kk in range(TOP_K):
            hi, lo = _unpack_bf16_pairs(_load_token_slab(buf.at[slot, kk], 0, tt, s))
            ya = ya + gks[kk] * hi
            yb = yb + gks[kk] * lo
        y_scr[:, ca] = ya
        y_scr[:, cb] = yb
    out_ref[...] = _rms(y_scr[...], nw_ref[...])


def _combine(dest3, gates, x2, yrows, final_w):
    T, D = x2.shape
    half = TOK_WORDS
    tt = min(MOE_TT, T)
    n = T // tt
    dspec = lambda f: pl.BlockSpec((None, 1, tt * TOP_K), lambda i: (f(i), 0, 0), memory_space=pltpu.SMEM)
    return pl.pallas_call(
        _combine_kernel,
        out_shape=jax.ShapeDtypeStruct((T, D), F32),
        grid=(n,),
        in_specs=[
            dspec(lambda i: i), dspec(lambda i: jnp.minimum(i + 1, n - 1)),
            pl.BlockSpec((tt, LANES_V7X), lambda i: (i, 0)),
            pl.BlockSpec((tt, D), lambda i: (i, 0)),
            pl.BlockSpec(memory_space=pl.ANY),
            pl.BlockSpec((1, D), lambda i: (0, 0)),
        ],
        out_specs=pl.BlockSpec((tt, D), lambda i: (i, 0)),
        scratch_shapes=[pltpu.VMEM((2, TOP_K, tt * TOK_SUB, LANES_V7X), U32), pltpu.VMEM((tt, D), F32),
                        pltpu.SemaphoreType.DMA((2,))],
        compiler_params=pltpu.CompilerParams(
            dimension_semantics=("arbitrary",),
            vmem_limit_bytes=_vmem_limit(2 * TOP_K * tt * half * 4 + 8 * tt * D * 4 + (8 << 20))),
        name="moe_combine",
    )(dest3, dest3, gates, x2, yrows, final_w)


def _route(top_e, n_vis):
    flat_e = top_e.reshape(-1)
    onehot = (flat_e[:, None] == jnp.arange(N_EXPERTS, dtype=I32)[None, :]).astype(I32)
    incl = jnp.cumsum(onehot, axis=0)
    counts = incl[-1]
    pos = jnp.sum((incl - onehot) * onehot, axis=1)
    nblk = (counts + MOE_BLK - 1) // MOE_BLK
    nvis = (nblk + MOE_G - 1) // MOE_G
    bpv = (nblk + jnp.maximum(nvis, 1) - 1) // jnp.maximum(nvis, 1)
    rpv = bpv * MOE_BLK
    vend = jnp.cumsum(nvis)
    vstart = vend - nvis
    rpv_a = jnp.maximum(jnp.sum(onehot * rpv[None, :], axis=1), 1)
    vs_a = jnp.sum(onehot * vstart[None, :], axis=1)
    q = jnp.floor((pos.astype(F32) + 0.5) / rpv_a.astype(F32)).astype(I32)
    dest = (vs_a + q) * MOE_R + (pos - q * rpv_a)

    n_used = vend[-1]
    v = jnp.minimum(jnp.arange(n_vis, dtype=I32), n_used - 1)[:, None]
    sel = ((vstart[None, :] <= v) & (v < vend[None, :])).astype(I32)
    visit_e = jnp.sum(sel * jnp.arange(N_EXPERTS, dtype=I32)[None, :], axis=1)
    local = v[:, 0] - jnp.sum(sel * vstart[None, :], axis=1)
    v_nblk = jnp.sum(sel * nblk[None, :], axis=1)
    v_bpv = jnp.sum(sel * bpv[None, :], axis=1)
    visit_nb = jnp.clip(v_nblk - local * v_bpv, 0, v_bpv)
    visit_nb = jnp.where(jnp.arange(n_vis) < n_used, visit_nb, 0)
    return dest.astype(I32), visit_e.astype(I32), visit_nb.astype(I32), n_used.astype(I32).reshape(1)


def kernel(x, mem, norm_mix_w, w_in, conv_w, conv_b, if_bias, m_head_norm_w, w_branch_m, w_branch_a, rel_bias, w_out, norm_x_w, norm_mem_w, wq_x, wkv_x, wo_x, norm_ffn_w, w_router, b_router, w_gu, b_gu, w_down, b_down, final_norm_w):
    B, S, D = x.shape
    T = B * S
    l = 0
    x2d = x.reshape(T, D)

    wi = w_in[l]
    o_qm, o_km, o_vm, o_om = 0, M_WIDTH, 2 * M_WIDTH, 3 * M_WIDTH
    o_if = 4 * M_WIDTH
    o_qa = o_if + 2 * M_HEADS
    o_ka, o_va = o_qa + A_WIDTH, o_qa + 2 * A_WIDTH
    o_gm = o_qa + 3 * A_WIDTH
    o_ga = o_gm + D
    seg = lambda o, w: wi[:, o:o + w]
    w_main = jnp.concatenate(
        [seg(o_gm, D), seg(o_ga, D), seg(o_qm, M_WIDTH), seg(o_km, M_WIDTH), seg(o_vm, M_WIDTH),
         seg(o_om, M_WIDTH), seg(o_qa, A_WIDTH), seg(o_ka, A_WIDTH), seg(o_va, A_WIDTH)], axis=1).astype(BF16)
    w_if = jnp.pad(seg(o_if, 2 * M_HEADS), ((0, 0), (0, LANES_V7X - 2 * M_HEADS))).astype(BF16)

    proj, if_rows = _inproj(x2d, norm_mix_w[l].reshape(1, D), w_main, w_if)
    p3 = proj.reshape(B, S, N_MAIN)

    nc = S // CHUNK
    gcol3 = if_rows.reshape(B, S, LANES_V7X)
    grow4 = if_rows[:, :2 * M_HEADS].reshape(B, nc, CHUNK, 2 * M_HEADS).transpose(0, 1, 3, 2)
    bias_col = jnp.pad(if_bias[l], (0, LANES_V7X - 2 * M_HEADS)).reshape(1, LANES_V7X)
    bias_row = jnp.broadcast_to(if_bias[l][:, None], (2 * M_HEADS, CHUNK))
    hm = _mlstm(p3, gcol3, grow4, conv_w[l], conv_b[l].reshape(1, 2 * M_WIDTH), bias_col, bias_row,
                m_head_norm_w[l].reshape(1, M_WIDTH))

    ya = _band_attn(p3, _band_bias(rel_bias[l]))

    x1 = _merge(hm.reshape(T, M_WIDTH), ya.reshape(T, A_WIDTH), proj, x2d,
                w_branch_m[l].astype(BF16), w_branch_a[l].astype(BF16), w_out[l].astype(BF16))

    km, vm = _memkv(mem, norm_mem_w[l].reshape(1, D), wkv_x[l].astype(BF16))
    wr = jnp.pad(w_router[l], ((0, 0), (0, LANES_V7X - N_EXPERTS)))
    wr_hi = wr.astype(BF16)
    wr_lo = (wr - wr_hi.astype(F32)).astype(BF16)
    br = jnp.concatenate([b_router[l], jnp.full((LANES_V7X - N_EXPERTS,), NEG_BIG, F32)]).reshape(1, LANES_V7X)
    x2, h3p, te, tg = _xattn(x1, S, norm_x_w[l].reshape(1, D), wq_x[l].astype(BF16), km, vm,
                             wo_x[l].astype(BF16), norm_ffn_w[l].reshape(1, D), wr_hi, wr_lo, br)

    A = T * TOP_K
    n_vis = (A // MOE_BLK + N_EXPERTS + N_EXPERTS * (MOE_G - 1)) // MOE_G
    dest, visit_e, visit_nb, n_used = _route(te[:, :TOP_K], n_vis)
    dest3 = (dest * TOK_SUB).reshape(T // min(MOE_TT, T), 1, -1)
    xs = _dispatch(visit_nb, dest3, h3p, n_vis * MOE_R)
    yrows = _experts(visit_e, visit_nb, n_used, xs, w_gu[l], b_gu[l].reshape(N_EXPERTS, 2 * D_FF // MOE_TF, MOE_TF),
                     w_down[l], b_down[l].reshape(N_EXPERTS, 1, D))
    out = _combine(dest3, tg, x2, yrows, final_norm_w.reshape(1, D))
    return out.reshape(B, S, D)
```

```python
import functools

import jax
import jax.numpy as jnp
from jax import lax
from jax.experimental import pallas as pl
from jax.experimental.pallas import tpu as pltpu

F32 = jnp.float32
BF16 = jnp.bfloat16
I32 = jnp.int32
U32 = jnp.uint32

D_MODEL = 2048
CHUNK = 64
EPS = 1e-6
M_HEADS = 4
M_HEAD_DIM = 256
M_WIDTH = M_HEADS * M_HEAD_DIM
CONV_WIDTH = 4
A_HEADS = 16
A_HEAD_DIM = 64
A_WIDTH = A_HEADS * A_HEAD_DIM
LEFT_CHUNKS = 8
MAX_REL = 128
X_HEADS = 4
X_HEAD_DIM = 128
X_WIDTH = X_HEADS * X_HEAD_DIM
N_EXPERTS = 32
TOP_K = 4
D_FF = D_MODEL
SWIGLU_LIMIT = 7.0
SWIGLU_ALPHA = 1.702

LANES_V7X = 128
SUBLANES_V7X = 8
VMEM_BYTES_V7X = 64 * 1024 * 1024
VMEM_LIMIT_CAP = 60000 * 1024
DMA_PRIORITIES = 2

NEG_BIG = -1e30

COL_GM = 0
COL_GA = 2048
COL_QM = 4096
COL_KM = 5120
COL_VM = 6144
COL_OM = 7168
COL_QA = 8192
COL_KA = 9216
COL_VA = 10240
N_MAIN = 11264

IN_TM = 1024
IN_TN = 1024
ATT_TQ = 512
ATT_SUB = 128
ATT_WIN = ATT_SUB + LEFT_CHUNKS * CHUNK
MERGE_TM = 512
XATT_TM = 512
MOE_BLK = 256
MOE_G = 9
MOE_R = MOE_G * MOE_BLK
MOE_BIG = 8
MOE_MID = 4
MOE_TF = 256
MOE_TT = 256


def _vmem_limit(nbytes):
    return int(min(VMEM_LIMIT_CAP, max(nbytes, 16 * 1024 * 1024)))


def _nt_dot(a, b):
    return lax.dot_general(a, b, (((1,), (1,)), ((), ())), preferred_element_type=F32)


def _tn_dot(a, b):
    return lax.dot_general(a, b, (((0,), (0,)), ((), ())), preferred_element_type=F32)


def _sigmoid(x):
    return 1.0 / (1.0 + jnp.exp(-x))


def _log_sigmoid(x):
    return jnp.minimum(x, 0.0) - jnp.log1p(jnp.exp(-jnp.abs(x)))


def _rms(x, w):
    return x * lax.rsqrt(jnp.mean(x * x, axis=-1, keepdims=True) + EPS) * w


def _pack_bf16_pairs(xb):
    n = xb.shape[1] // 2
    hi = lax.bitcast_convert_type(xb[:, :n].astype(F32), U32)
    lo = lax.bitcast_convert_type(xb[:, n:].astype(F32), U32)
    return hi | (lo >> 16)


def _unpack_bf16_pairs(w):
    hi = lax.bitcast_convert_type(w & jnp.uint32(0xFFFF0000), F32)
    lo = lax.bitcast_convert_type(w << 16, F32)
    return hi, lo


TOK_SUB = SUBLANES_V7X
TOK_WORDS = TOK_SUB * LANES_V7X


def _store_token_tiles(ref, tok0, packed):
    n = packed.shape[0]
    for s in range(TOK_SUB):
        ref[pl.ds(tok0 * TOK_SUB + s, n, stride=TOK_SUB), :] = packed[:, s * LANES_V7X:(s + 1) * LANES_V7X]


def _load_token_slab(ref, tok0, n, s):
    return ref[pl.ds(tok0 * TOK_SUB + s, n, stride=TOK_SUB), :]


def _inproj_kernel(x_ref, nw_ref, w_ref, wif_ref, o_ref, if_ref, h_scr):
    @pl.when(pl.program_id(1) == 0)
    def _():
        hb = _rms(x_ref[...], nw_ref[...]).astype(BF16)
        h_scr[...] = hb
        if_ref[...] = jnp.dot(hb, wif_ref[...], preferred_element_type=F32)

    o_ref[...] = jnp.dot(h_scr[...], w_ref[...], preferred_element_type=F32).astype(o_ref.dtype)


def _inproj(x2d, norm_w, w_main, w_if):
    T, D = x2d.shape
    tm = min(IN_TM, T)
    grid = (T // tm, N_MAIN // IN_TN)
    vmem = 2 * tm * D * 4 + tm * D * 2 + 2 * D * IN_TN * 2 + 2 * tm * IN_TN * 2 + 4 * tm * LANES_V7X * 4
    return pl.pallas_call(
        _inproj_kernel,
        out_shape=(jax.ShapeDtypeStruct((T, N_MAIN), BF16), jax.ShapeDtypeStruct((T, LANES_V7X), F32)),
        grid=grid,
        in_specs=[
            pl.BlockSpec((tm, D), lambda i, j: (i, 0)),
            pl.BlockSpec((1, D), lambda i, j: (0, 0)),
            pl.BlockSpec((D, IN_TN), lambda i, j: (0, j)),
            pl.BlockSpec((D, LANES_V7X), lambda i, j: (0, 0)),
        ],
        out_specs=(
            pl.BlockSpec((tm, IN_TN), lambda i, j: (i, j)),
            pl.BlockSpec((tm, LANES_V7X), lambda i, j: (i, 0)),
        ),
        scratch_shapes=[pltpu.VMEM((tm, D), BF16)],
        compiler_params=pltpu.CompilerParams(
            dimension_semantics=("parallel", "arbitrary"),
            vmem_limit_bytes=_vmem_limit(vmem + (8 << 20))),
        name="inproj",
    )(x2d, norm_w, w_main, w_if)


def _mlstm_kernel(q_ref, k_ref, v_ref, o_ref, gc_ref, gr_ref, cw_ref, cb_ref, bc_ref, br_ref, nw_ref,
                  out_ref, tail, c_st, n_st, m_st):
    nb = q_ref.shape[0]
    L = CHUNK
    dh = M_HEAD_DIM
    halo = tail.shape[1]

    @pl.when(pl.program_id(0) == 0)
    def _():
        tail[...] = jnp.zeros_like(tail)
        c_st[...] = jnp.zeros_like(c_st)
        n_st[...] = jnp.zeros_like(n_st)
        m_st[...] = jnp.zeros_like(m_st)

    row = lax.broadcasted_iota(I32, (L, L), 0)
    col = lax.broadcasted_iota(I32, (L, L), 1)
    lower = col <= row

    srow = lax.broadcasted_iota(I32, (L, halo + L), 0)
    scol = lax.broadcasted_iota(I32, (L, halo + L), 1)
    shift_mat = jnp.concatenate(
        [jnp.where(scol == srow + (halo - (CONV_WIDTH - 1) + j), 1.0, 0.0) for j in range(CONV_WIDTH - 1)],
        axis=0).astype(BF16)

    erow = lax.broadcasted_iota(I32, (dh, dh), 0)
    ecol = lax.broadcasted_iota(I32, (dh, dh), 1)
    eye = jnp.where(erow == ecol, 1.0, 0.0).astype(BF16)

    streams = [(b, h) for b in range(nb) for h in range(M_HEADS)]

    qk = []
    for b in range(nb):
        halves = []
        for half, ref in enumerate((q_ref, k_ref)):
            cols = slice(half * M_WIDTH, (half + 1) * M_WIDTH)
            cur = ref[b]
            ext = jnp.concatenate([tail[b, :, cols], cur], axis=0)
            back = jnp.dot(shift_mat, ext, preferred_element_type=F32)
            conv = cb_ref[:, cols] + cur.astype(F32) * cw_ref[CONV_WIDTH - 1:CONV_WIDTH, cols]
            for j in range(CONV_WIDTH - 1):
                conv = conv + back[j * L:(j + 1) * L] * cw_ref[j:j + 1, cols]
            halves.append(conv * _sigmoid(conv))
            tail[b, :, cols] = cur[L - halo:L]
        qk.append(halves)

    st = []
    for b, h in streams:
        s = b * M_HEADS + h
        lo, hi = h * dh, (h + 1) * dh
        gcol = gc_ref[b] + bc_ref[...]
        grow = gr_ref[b, 0] + br_ref[...]
        q = qk[b][0][:, lo:hi]
        k = qk[b][1][:, lo:hi] * (dh ** -0.5)
        ig_c = gcol[:, h:h + 1]
        ig_r = grow[h:h + 1, :]
        ls_c = _log_sigmoid(gcol[:, M_HEADS + h:M_HEADS + h + 1])
        ls_r = _log_sigmoid(grow[M_HEADS + h:M_HEADS + h + 1, :])
        b_c = jnp.sum(jnp.where(lower, ls_r, 0.0), axis=1, keepdims=True)
        b_r = jnp.sum(jnp.where(row <= col, ls_c, 0.0), axis=0, keepdims=True)
        g = jnp.sum(ls_r, axis=1, keepdims=True)
        m_old = m_st[s][:, 0:1]
        w_end = g - b_c + ig_c
        a = jnp.max(w_end, axis=0, keepdims=True)
        dlog = jnp.where(lower, b_c - b_r + ig_r, NEG_BIG)
        inter = b_c + m_old
        m_j = jnp.maximum(inter, jnp.max(dlog, axis=1, keepdims=True))
        st.append(dict(s=s, lo=lo, hi=hi, b=b, q=q, k=k, qb=q.astype(BF16), kb=k.astype(BF16),
                       vb=v_ref[b, :, lo:hi], g=g, a=a, m_old=m_old, m_j=m_j,
                       e_end=jnp.exp(w_end - a), decay=jnp.exp(dlog - m_j), w_int=jnp.exp(inter - m_j),
                       c_old=c_st[s], n_old=n_st[s]))

    for d in st:
        d["sm"] = _nt_dot(d["qb"], d["kb"]) * d["decay"]
    for d in st:
        d["num"] = (jnp.dot(d["sm"].astype(BF16), d["vb"], preferred_element_type=F32)
                    + d["w_int"] * _nt_dot(d["qb"], d["c_old"].astype(BF16)))
    for d in st:
        den = (jnp.sum(d["sm"], axis=1, keepdims=True)
               + d["w_int"] * jnp.sum(d["q"] * d["n_old"], axis=1, keepdims=True))
        hh = d["num"] / jnp.maximum(jnp.abs(den), jnp.exp(-d["m_j"]))
        hn = _rms(hh, nw_ref[:, d["lo"]:d["hi"]])
        og = _sigmoid(o_ref[d["b"], :, d["lo"]:d["hi"]].astype(F32))
        out_ref[d["b"], :, d["lo"]:d["hi"]] = (og * hn).astype(out_ref.dtype)
    for d in st:
        s = d["s"]
        m_new = jnp.maximum(d["g"] + d["m_old"], d["a"])
        s_prev = jnp.exp(d["g"] + d["m_old"] - m_new)
        s_loc = jnp.exp(d["a"] - m_new)
        ve = (d["vb"].astype(F32) * (d["e_end"] * s_loc)).astype(BF16)
        ve_t = _nt_dot(eye, ve).astype(BF16)
        c_st[s] = s_prev * d["c_old"] + jnp.dot(ve_t, d["kb"], preferred_element_type=F32)
        n_st[s] = s_prev * d["n_old"] + s_loc * jnp.sum(d["e_end"] * d["k"], axis=0, keepdims=True)
        m_st[s] = jnp.broadcast_to(m_new, (1, LANES_V7X))


def _mlstm(p3, gcol3, grow4, conv_w, conv_b, bias_col, bias_row, head_w):
    B, S, _ = p3.shape
    nc = S // CHUNK
    ns = B * M_HEADS
    blk = lambda cb: pl.BlockSpec((B, CHUNK, M_WIDTH), lambda c, cb=cb: (0, c, cb))
    full = lambda shape: pl.BlockSpec(shape, lambda c: (0,) * len(shape))
    vmem = (2 * 5 * B * CHUNK * M_WIDTH * 2 + B * (CHUNK + 8) * 2 * M_WIDTH * 4
            + ns * M_HEAD_DIM * M_HEAD_DIM * 4 + (16 << 20))
    return pl.pallas_call(
        _mlstm_kernel,
        out_shape=jax.ShapeDtypeStruct((B, S, M_WIDTH), BF16),
        grid=(nc,),
        in_specs=[
            blk(COL_QM // M_WIDTH), blk(COL_KM // M_WIDTH), blk(COL_VM // M_WIDTH), blk(COL_OM // M_WIDTH),
            pl.BlockSpec((B, CHUNK, LANES_V7X), lambda c: (0, c, 0)),
            pl.BlockSpec((B, 1, 2 * M_HEADS, CHUNK), lambda c: (0, c, 0, 0)),
            full((CONV_WIDTH, 2 * M_WIDTH)), full((1, 2 * M_WIDTH)),
            full((1, LANES_V7X)), full((2 * M_HEADS, CHUNK)), full((1, M_WIDTH)),
        ],
        out_specs=pl.BlockSpec((B, CHUNK, M_WIDTH), lambda c: (0, c, 0)),
        scratch_shapes=[
            pltpu.VMEM((B, 2 * SUBLANES_V7X, 2 * M_WIDTH), BF16),
            pltpu.VMEM((ns, M_HEAD_DIM, M_HEAD_DIM), F32),
            pltpu.VMEM((ns, 1, M_HEAD_DIM), F32),
            pltpu.VMEM((ns, 1, LANES_V7X), F32),
        ],
        compiler_params=pltpu.CompilerParams(
            dimension_semantics=("arbitrary",), vmem_limit_bytes=_vmem_limit(vmem)),
        name="mlstm",
    )(p3, p3, p3, p3, gcol3, grow4, conv_w, conv_b, bias_col, bias_row, head_w)


def _band_attn_kernel(q_ref, kp_ref, kc_ref, vp_ref, vc_ref, bias_ref, out_ref):
    qb_idx = pl.program_id(2)
    tq = q_ref.shape[1]
    q2 = q_ref[0]
    kcat = jnp.concatenate([kp_ref[0], kc_ref[0]], axis=0)
    vcat = jnp.concatenate([vp_ref[0], vc_ref[0]], axis=0)
    lane = lax.broadcasted_iota(I32, (1, LANES_V7X), 1)
    wcol = lax.broadcasted_iota(I32, (1, ATT_WIN), 1)
    first_pen = jnp.where(qb_idx == 0, NEG_BIG, 0.0)
    scale = A_HEAD_DIM ** -0.5
    bias = bias_ref[0]

    subs = []
    for t in range(tq // ATT_SUB):
        r0 = t * ATT_SUB
        w0 = tq + r0 - LEFT_CHUNKS * CHUNK
        pen = jnp.where(wcol < (tq - w0), first_pen, 0.0)
        qs = q2[r0:r0 + ATT_SUB] * scale
        zero = jnp.zeros_like(qs)
        qst = jnp.concatenate([jnp.where(lane < A_HEAD_DIM, qs, zero),
                               jnp.where(lane >= A_HEAD_DIM, qs, zero)], axis=0)
        subs.append(dict(r0=r0, w0=w0, pen=pen, qst=qst))
    for d in subs:
        d["sc"] = _nt_dot(d["qst"], kcat[d["w0"]:d["w0"] + ATT_WIN]) + bias + d["pen"]
    for d in subs:
        p = jnp.exp(d["sc"] - jnp.max(d["sc"], axis=1, keepdims=True))
        d["den"] = jnp.sum(p, axis=1, keepdims=True)
        d["p"] = p.astype(BF16)
    for d in subs:
        d["o"] = jnp.dot(d["p"], vcat[d["w0"]:d["w0"] + ATT_WIN], preferred_element_type=F32) / d["den"]
    for d in subs:
        o, r0 = d["o"], d["r0"]
        out_ref[0, r0:r0 + ATT_SUB, :] = jnp.where(
            lane < A_HEAD_DIM, o[:ATT_SUB], o[ATT_SUB:]).astype(out_ref.dtype)


def _band_attn(p3, bias):
    B, S, _ = p3.shape
    tq = ATT_TQ
    nq = S // tq
    nhp = A_HEADS // 2
    cq, ck, cv = COL_QA // LANES_V7X, COL_KA // LANES_V7X, COL_VA // LANES_V7X
    cur = lambda c0: pl.BlockSpec((1, tq, LANES_V7X), lambda b, hp, i, c0=c0: (b, i, c0 + hp))
    prev = lambda c0: pl.BlockSpec((1, tq, LANES_V7X), lambda b, hp, i, c0=c0: (b, jnp.maximum(i - 1, 0), c0 + hp))
    return pl.pallas_call(
        _band_attn_kernel,
        out_shape=jax.ShapeDtypeStruct((B, S, A_WIDTH), BF16),
        grid=(B, nhp, nq),
        in_specs=[cur(cq), prev(ck), cur(ck), prev(cv), cur(cv),
                  pl.BlockSpec((1, 2 * ATT_SUB, ATT_WIN), lambda b, hp, i: (hp, 0, 0))],
        out_specs=pl.BlockSpec((1, tq, LANES_V7X), lambda b, hp, i: (b, i, hp)),
        compiler_params=pltpu.CompilerParams(
            dimension_semantics=("parallel", "parallel", "arbitrary"),
            vmem_limit_bytes=_vmem_limit(32 << 20)),
        name="band_attn",
    )(p3, p3, p3, p3, p3, bias)


def _band_bias(rel_table):
    n = 6 * LANES_V7X
    c = jnp.arange(n)
    d = jnp.where(c < ATT_WIN, c, c - n)
    rel = jnp.clip(LEFT_CHUNKS * CHUNK - d, -MAX_REL, MAX_REL) + MAX_REL
    vec = rel_table[:, rel].astype(F32)
    sheared = jnp.tile(vec, (1, ATT_SUB))[:, :ATT_SUB * (n - 1)].reshape(A_HEADS, ATT_SUB, n - 1)
    i = jnp.arange(ATT_SUB)[:, None]
    kp = jnp.arange(ATT_WIN)[None, :]
    ci, kc = i // CHUNK, kp // CHUNK
    valid = (kc >= ci) & (kc <= ci + LEFT_CHUNKS)
    bias = jnp.where(valid[None], sheared[:, :, :ATT_WIN], NEG_BIG)
    return bias.reshape(A_HEADS // 2, 2 * ATT_SUB, ATT_WIN)


def _merge_kernel(hm_ref, ya_ref, gm_ref, ga_ref, x_ref, wbm_ref, wba_ref, wo_ref, out_ref):
    ym = jnp.dot(hm_ref[...], wbm_ref[...], preferred_element_type=F32)
    ya = jnp.dot(ya_ref[...], wba_ref[...], preferred_element_type=F32)
    merged = _sigmoid(gm_ref[...].astype(F32)) * ym + _sigmoid(ga_ref[...].astype(F32)) * ya
    out_ref[...] = x_ref[...] + jnp.dot(merged.astype(BF16), wo_ref[...], preferred_element_type=F32)


def _const_spec(shape):
    return pl.BlockSpec(shape, lambda i: (0,) * len(shape), pipeline_mode=pl.Buffered(1))


def _merge(hm2d, ya2d, p2d, x2d, wbm, wba, wout):
    T, D = x2d.shape
    tm = min(MERGE_TM, T)
    vmem = ((2 * M_WIDTH * D + D * D) * 2 + 2 * (2 * tm * M_WIDTH * 2 + 2 * tm * D * 2 + 2 * tm * D * 4)
            + 6 * tm * D * 4)
    return pl.pallas_call(
        _merge_kernel,
        out_shape=jax.ShapeDtypeStruct((T, D), F32),
        grid=(T // tm,),
        in_specs=[
            pl.BlockSpec((tm, M_WIDTH), lambda i: (i, 0)),
            pl.BlockSpec((tm, A_WIDTH), lambda i: (i, 0)),
            pl.BlockSpec((tm, D), lambda i: (i, COL_GM // D)),
            pl.BlockSpec((tm, D), lambda i: (i, COL_GA // D)),
            pl.BlockSpec((tm, D), lambda i: (i, 0)),
            _const_spec((M_WIDTH, D)), _const_spec((A_WIDTH, D)), _const_spec((D, D)),
        ],
        out_specs=pl.BlockSpec((tm, D), lambda i: (i, 0)),
        compiler_params=pltpu.CompilerParams(
            dimension_semantics=("parallel",), vmem_limit_bytes=_vmem_limit(vmem)),
        name="merge_out",
    )(hm2d, ya2d, p2d, p2d, x2d, wbm, wba, wout)


def _memkv_kernel(mem_ref, nw_ref, w_ref, k_ref, v_ref):
    hb = _rms(mem_ref[0], nw_ref[...]).astype(BF16)
    kv = jnp.dot(hb, w_ref[...], preferred_element_type=F32)
    k_ref[0] = kv[:, :X_WIDTH].astype(k_ref.dtype)
    v_ref[0] = kv[:, X_WIDTH:].astype(v_ref.dtype)


def _memkv(mem, norm_w, wkv):
    B, N, D = mem.shape
    return pl.pallas_call(
        _memkv_kernel,
        out_shape=(jax.ShapeDtypeStruct((B, N, X_WIDTH), BF16), jax.ShapeDtypeStruct((B, N, X_WIDTH), BF16)),
        grid=(B,),
        in_specs=[pl.BlockSpec((1, N, D), lambda b: (b, 0, 0)),
                  pl.BlockSpec((1, D), lambda b: (0, 0)),
                  pl.BlockSpec((D, 2 * X_WIDTH), lambda b: (0, 0))],
        out_specs=(pl.BlockSpec((1, N, X_WIDTH), lambda b: (b, 0, 0)),
                   pl.BlockSpec((1, N, X_WIDTH), lambda b: (b, 0, 0))),
        compiler_params=pltpu.CompilerParams(
            dimension_semantics=("parallel",), vmem_limit_bytes=_vmem_limit(24 << 20)),
        name="mem_kv",
    )(mem, norm_w, wkv)


def _xattn_kernel(x_ref, nxw_ref, wq_ref, km_ref, vm_ref, wo_ref, nfw_ref, wrh_ref, wrl_ref, br_ref,
                  x2_ref, h3_ref, te_ref, tg_ref):
    x1 = x_ref[...]
    hb = _rms(x1, nxw_ref[...]).astype(BF16)
    q = jnp.dot(hb, wq_ref[...], preferred_element_type=F32).astype(BF16)
    km = km_ref[0]
    vm = vm_ref[0]
    outs = []
    for h in range(X_HEADS):
        lo, hi = h * X_HEAD_DIM, (h + 1) * X_HEAD_DIM
        sc = _nt_dot(q[:, lo:hi], km[:, lo:hi]) * (X_HEAD_DIM ** -0.5)
        mx = jnp.max(sc, axis=1, keepdims=True)
        p = jnp.exp(sc - mx)
        den = jnp.sum(p, axis=1, keepdims=True)
        outs.append(jnp.dot(p.astype(BF16), vm[:, lo:hi], preferred_element_type=F32) / den)
    o = jnp.concatenate(outs, axis=1).astype(BF16)
    x2 = x1 + jnp.dot(o, wo_ref[...], preferred_element_type=F32)
    x2_ref[...] = x2

    h3 = _rms(x2, nfw_ref[...])
    h_hi = h3.astype(BF16)
    _store_token_tiles(h3_ref, 0, _pack_bf16_pairs(h_hi))
    h_lo = (h3 - h_hi.astype(F32)).astype(BF16)
    logits = (jnp.dot(h_hi, wrh_ref[...], preferred_element_type=F32)
              + jnp.dot(h_lo, wrh_ref[...], preferred_element_type=F32)
              + jnp.dot(h_hi, wrl_ref[...], preferred_element_type=F32)
              + br_ref[...])

    colf = lax.broadcasted_iota(I32, logits.shape, 1).astype(F32)
    work = logits
    vals, idxs = [], []
    for _ in range(TOP_K):
        mx = jnp.max(work, axis=1, keepdims=True)
        ix = jnp.min(jnp.where(work == mx, colf, float(LANES_V7X)), axis=1, keepdims=True)
        vals.append(mx)
        idxs.append(ix)
        work = jnp.where(colf == ix, -jnp.inf, work)
    es = [jnp.exp(v - vals[0]) for v in vals]
    tot = es[0] + es[1] + es[2] + es[3]
    te = jnp.zeros_like(logits)
    tg = jnp.zeros_like(logits)
    for kk in range(TOP_K):
        te = jnp.where(colf == float(kk), idxs[kk], te)
        tg = jnp.where(colf == float(kk), es[kk] / tot, tg)
    te_ref[...] = te.astype(I32)
    tg_ref[...] = tg


def _xattn(x1, batch_rows, norm_x_w, wq, km, vm, wo, norm_ffn_w, wr_hi, wr_lo, br):
    T, D = x1.shape
    tm = min(XATT_TM, batch_rows)
    per_b = batch_rows // tm
    n_mem = km.shape[1]
    row = lambda w: pl.BlockSpec((tm, w), lambda i: (i, 0))
    vmem = (2 * D * X_WIDTH * 2 + 2 * D * LANES_V7X * 2 + 4 * n_mem * X_WIDTH * 2
            + 2 * 3 * tm * D * 4 + 8 * tm * D * 4)
    return pl.pallas_call(
        _xattn_kernel,
        out_shape=(jax.ShapeDtypeStruct((T, D), F32), jax.ShapeDtypeStruct((T * TOK_SUB, LANES_V7X), U32),
                   jax.ShapeDtypeStruct((T, LANES_V7X), I32), jax.ShapeDtypeStruct((T, LANES_V7X), F32)),
        grid=(T // tm,),
        in_specs=[
            row(D), _const_spec((1, D)), _const_spec((D, X_WIDTH)),
            pl.BlockSpec((1, n_mem, X_WIDTH), lambda i: (i // per_b, 0, 0)),
            pl.BlockSpec((1, n_mem, X_WIDTH), lambda i: (i // per_b, 0, 0)),
            _const_spec((X_WIDTH, D)), _const_spec((1, D)),
            _const_spec((D, LANES_V7X)), _const_spec((D, LANES_V7X)), _const_spec((1, LANES_V7X)),
        ],
        out_specs=(row(D), pl.BlockSpec((tm * TOK_SUB, LANES_V7X), lambda i: (i, 0)),
                   row(LANES_V7X), row(LANES_V7X)),
        compiler_params=pltpu.CompilerParams(
            dimension_semantics=("parallel",), vmem_limit_bytes=_vmem_limit(vmem)),
        name="xattn_router",
    )(x1, norm_x_w, wq, km, vm, wo, norm_ffn_w, wr_hi, wr_lo, br)


def _dispatch_kernel(partial_ref, dest_ref, h_ref, xs_ref, zero_scr, sem, zsem):
    tt = h_ref.shape[0] // TOK_SUB
    nblocks = partial_ref.shape[0]

    @pl.when(pl.program_id(0) == 0)
    def _():
        zero_scr[...] = jnp.zeros_like(zero_scr)

        def zero_copy(b):
            row0 = pl.multiple_of(b * (MOE_BLK * TOK_SUB), MOE_BLK * TOK_SUB)
            return pltpu.make_async_copy(zero_scr, xs_ref.at[pl.ds(row0, MOE_BLK * TOK_SUB)], zsem)

        def zstart(b, carry):
            @pl.when(partial_ref[b] > 0)
            def _():
                zero_copy(b).start()
            return carry

        def zwait(b, carry):
            @pl.when(partial_ref[b] > 0)
            def _():
                zero_copy(b).wait()
            return carry

        lax.fori_loop(0, nblocks, zstart, 0)
        lax.fori_loop(0, nblocks, zwait, 0)

    def tile_copy(t, kk):
        src = h_ref.at[pl.ds(pl.multiple_of(t * TOK_SUB, TOK_SUB), TOK_SUB)]
        dst = xs_ref.at[pl.ds(pl.multiple_of(dest_ref[0, t * TOP_K + kk], TOK_SUB), TOK_SUB)]
        return pltpu.make_async_copy(src, dst, sem)

    def start(t, carry):
        for kk in range(TOP_K):
            tile_copy(t, kk).start(priority=kk % DMA_PRIORITIES)
        return carry

    def wait(t, carry):
        for kk in range(TOP_K):
            tile_copy(t, kk).wait()
        return carry

    lax.fori_loop(0, tt, start, 0)
    lax.fori_loop(0, tt, wait, 0)


def _dispatch(partial, dest3, h3p):
    T = h3p.shape[0] // TOK_SUB
    n_rows = partial.shape[0] * MOE_BLK
    tt = min(MOE_TT, T)
    grid_spec = pltpu.PrefetchScalarGridSpec(
        num_scalar_prefetch=1,
        grid=(T // tt,),
        in_specs=[
            pl.BlockSpec((None, 1, tt * TOP_K), lambda i, nb: (i, 0, 0), memory_space=pltpu.SMEM),
            pl.BlockSpec((tt * TOK_SUB, LANES_V7X), lambda i, nb: (i, 0)),
        ],
        out_specs=pl.BlockSpec(memory_space=pl.ANY),
        scratch_shapes=[pltpu.VMEM((MOE_BLK * TOK_SUB, LANES_V7X), U32),
                        pltpu.SemaphoreType.DMA(()), pltpu.SemaphoreType.DMA(())],
    )
    return pl.pallas_call(
        _dispatch_kernel,
        out_shape=jax.ShapeDtypeStruct((n_rows * TOK_SUB, LANES_V7X), U32),
        grid_spec=grid_spec,
        compiler_params=pltpu.CompilerParams(
            dimension_semantics=("arbitrary",), has_side_effects=True,
            vmem_limit_bytes=_vmem_limit(4 * tt * TOK_WORDS * 4 + MOE_BLK * TOK_WORDS * 4 + (8 << 20))),
        name="moe_dispatch",
    )(partial, dest3, h3p)


def _experts_kernel(ve_ref, vf_ref, nb_ref, nu_ref, tb_ref, bgu_ref, bd_ref, xs_hbm, wgu_hbm, wdn_hbm, out_hbm,
                    xb_scr, acc_scr, x_buf, o_buf, wg_buf, wu_buf, wd_buf, wgb_scr, wub_scr, wdb_scr,
                    xsems, osems, sems):
    v = pl.program_id(0)
    nb = nb_ref[v]
    first = vf_ref[v]
    n_used = nu_ref[0]
    half = TOK_WORDS
    tf = wg_buf.shape[2]
    nj = D_FF // tf
    blk_tiles = MOE_BLK * TOK_SUB
    n_blocks = out_hbm.shape[0] // blk_tiles

    def block_rows(b):
        return pl.ds(pl.multiple_of(b * MOE_BLK, MOE_BLK), MOE_BLK)

    def hbm_block(ref, blk):
        return ref.at[pl.ds(pl.multiple_of(blk * blk_tiles, blk_tiles), blk_tiles)]

    def x_copy(blk, slot):
        return pltpu.make_async_copy(hbm_block(xs_hbm, blk), x_buf.at[slot], xsems.at[slot])

    def o_copy(blk, slot):
        return pltpu.make_async_copy(o_buf.at[slot], hbm_block(out_hbm, blk), osems.at[slot])

    def tile_copies(e, j, slot):
        cols = pl.ds(pl.multiple_of(j * tf, tf), tf)
        ucols = pl.ds(pl.multiple_of(D_FF + j * tf, tf), tf)
        return (pltpu.make_async_copy(wgu_hbm.at[e, :, cols], wg_buf.at[slot], sems.at[slot, 0]),
                pltpu.make_async_copy(wgu_hbm.at[e, :, ucols], wu_buf.at[slot], sems.at[slot, 1]),
                pltpu.make_async_copy(wdn_hbm.at[e, cols, :], wd_buf.at[slot], sems.at[slot, 2]))

    def fetch(e, j, slot):
        for c in tile_copies(e, j, slot):
            c.start()

    @pl.when(nb > 0)
    def _():
        e = ve_ref[v]

        @pl.when(v == 0)
        def _():
            fetch(e, 0, 0)
            x_copy(first, 0).start()

        def stage(b, carry):
            slot = lax.rem(b, 2)

            @pl.when(b + 1 < nb)
            def _():
                x_copy(first + b + 1, 1 - slot).start()

            x_copy(first + b, slot).wait()
            rows = block_rows(b)
            for s in range(TOK_SUB):
                hi, lo = _unpack_bf16_pairs(_load_token_slab(x_buf.at[slot], 0, MOE_BLK, s))
                xb_scr[rows, s * LANES_V7X:(s + 1) * LANES_V7X] = hi.astype(BF16)
                xb_scr[rows, half + s * LANES_V7X:half + (s + 1) * LANES_V7X] = lo.astype(BF16)
            acc_scr[rows, :] = jnp.zeros((MOE_BLK, acc_scr.shape[1]), F32)
            return carry
        lax.fori_loop(0, nb, stage, 0)

        big = nb >= MOE_BIG
        mid = jnp.logical_and(nb >= MOE_MID, jnp.logical_not(big))

        def tile_step(j, carry):
            slot = lax.rem(j, 2)

            @pl.when(j + 1 < nj)
            def _():
                fetch(e, j + 1, 1 - slot)

            @pl.when(jnp.logical_and(j + 1 == nj, v + 1 < n_used))
            def _():
                fetch(ve_ref[v + 1], 0, 1 - slot)
                x_copy(vf_ref[v + 1], 0).start()

            for c in tile_copies(e, j, slot):
                c.wait()

            def ffn_rows(rows, wg, wu, wd):
                xb = xb_scr[rows, :]
                g = jnp.dot(xb, wg, preferred_element_type=F32) + bgu_ref[pl.ds(j, 1), :]
                u = jnp.dot(xb, wu, preferred_element_type=F32) + bgu_ref[pl.ds(nj + j, 1), :]
                g = jnp.minimum(g, SWIGLU_LIMIT)
                u = jnp.clip(u, -SWIGLU_LIMIT, SWIGLU_LIMIT)
                act = (u + 1.0) * (g * _sigmoid(SWIGLU_ALPHA * g))
                acc_scr[rows, :] += jnp.dot(act.astype(BF16), wd, preferred_element_type=F32)

            def cast_weights():
                wg = wg_buf[slot].astype(BF16)
                wu = wu_buf[slot].astype(BF16)
                wd = wd_buf[slot].astype(BF16)
                wgb_scr[...] = wg
                wub_scr[...] = wu
                wdb_scr[...] = wd
                return wg, wu, wd

            @pl.when(big)
            def _():
                ffn_rows(pl.ds(0, MOE_BIG * MOE_BLK), *cast_weights())

            @pl.when(mid)
            def _():
                ffn_rows(pl.ds(0, MOE_MID * MOE_BLK), *cast_weights())

            @pl.when(nb < MOE_MID)
            def _():
                cast_weights()

            def block(b, c2):
                ffn_rows(block_rows(b), wgb_scr[...], wub_scr[...], wdb_scr[...])
                return c2
            lax.fori_loop(jnp.where(big, MOE_BIG, jnp.where(mid, MOE_MID, 0)), nb, block, 0)
            return carry
        lax.fori_loop(0, nj, tile_step, 0)

        def finish(b, carry):
            slot = lax.rem(b, 2)

            @pl.when(b >= 2)
            def _():
                o_copy(first + b - 2, slot).wait()

            y = (acc_scr[block_rows(b), :] + bd_ref[...]).astype(BF16)
            _store_token_tiles(o_buf.at[slot], 0, _pack_bf16_pairs(y))
            o_copy(first + b, slot).start()
            return carry
        lax.fori_loop(0, nb, finish, 0)

        for back in (2, 1):
            @pl.when(nb >= back)
            def _():
                o_copy(first + nb - back, lax.rem(nb - back, 2)).wait()

    @pl.when(v == pl.num_programs(0) - 1)
    def _():
        o_buf[0] = jnp.zeros(o_buf.shape[1:], U32)

        def fill(b, carry):
            c = o_copy(b, 0)
            c.start()
            c.wait()
            return carry
        lax.fori_loop(tb_ref[0], n_blocks, fill, 0)


def _experts(visit_e, visit_first, visit_nb, n_used, n_blocks_used, xs, wgu, bgu, wdn, bdn):
    half = TOK_WORDS
    D = 2 * half
    tf = MOE_TF
    nj = D_FF // tf
    nv = visit_e.shape[0]
    blk_tiles = MOE_BLK * TOK_SUB
    assert nj % 2 == 0

    grid_spec = pltpu.PrefetchScalarGridSpec(
        num_scalar_prefetch=5,
        grid=(nv,),
        in_specs=[
            pl.BlockSpec((None, 2 * nj, tf), lambda v, ve, vf, nb, nu, tb: (ve[v], 0, 0)),
            pl.BlockSpec((None, 1, D), lambda v, ve, vf, nb, nu, tb: (ve[v], 0, 0)),
            pl.BlockSpec(memory_space=pl.ANY),
            pl.BlockSpec(memory_space=pl.ANY),
            pl.BlockSpec(memory_space=pl.ANY),
        ],
        out_specs=pl.BlockSpec(memory_space=pl.ANY),
        scratch_shapes=[pltpu.VMEM((MOE_R, D), BF16), pltpu.VMEM((MOE_R, D), F32),
                        pltpu.VMEM((2, blk_tiles, LANES_V7X), U32), pltpu.VMEM((2, blk_tiles, LANES_V7X), U32),
                        pltpu.VMEM((2, D, tf), F32), pltpu.VMEM((2, D, tf), F32), pltpu.VMEM((2, tf, D), F32),
                        pltpu.VMEM((D, tf), BF16), pltpu.VMEM((D, tf), BF16), pltpu.VMEM((tf, D), BF16),
                        pltpu.SemaphoreType.DMA((2,)), pltpu.SemaphoreType.DMA((2,)),
                        pltpu.SemaphoreType.DMA((2, 3))],
    )
    vmem = (MOE_R * D * 2 + MOE_R * D * 4 + 4 * blk_tiles * LANES_V7X * 4
            + 2 * 3 * D * tf * 4 + 3 * D * tf * 2 + 5 * MOE_BIG * MOE_BLK * tf * 4)
    return pl.pallas_call(
        _experts_kernel,
        out_shape=jax.ShapeDtypeStruct(xs.shape, U32),
        grid_spec=grid_spec,
        compiler_params=pltpu.CompilerParams(
            dimension_semantics=("arbitrary",), has_side_effects=True, vmem_limit_bytes=_vmem_limit(vmem)),
        name="moe_experts",
    )(visit_e, visit_first, visit_nb, n_used, n_blocks_used, bgu, bdn, xs, wgu, wdn)


def _combine_kernel(dcur_ref, dnext_ref, gate_ref, x_ref, y_ref, nw_ref, out_ref, buf, y_scr, sems):
    i = pl.program_id(0)
    n = pl.num_programs(0)
    tt = x_ref.shape[0]
    half = TOK_WORDS
    slot = lax.rem(i, 2)

    def row_copy(dref, s, t, kk):
        src = y_ref.at[pl.ds(pl.multiple_of(dref[0, t * TOP_K + kk], TOK_SUB), TOK_SUB)]
        dst = buf.at[s, kk, pl.ds(pl.multiple_of(t * TOK_SUB, TOK_SUB), TOK_SUB)]
        return pltpu.make_async_copy(src, dst, sems.at[s])

    def issue(dref, s):
        def body(t, carry):
            for kk in range(TOP_K):
                row_copy(dref, s, t, kk).start(priority=kk % DMA_PRIORITIES)
            return carry
        lax.fori_loop(0, tt, body, 0)

    @pl.when(i == 0)
    def _():
        issue(dcur_ref, 0)

    @pl.when(i + 1 < n)
    def _():
        issue(dnext_ref, 1 - slot)

    def wait(t, carry):
        for kk in range(TOP_K):
            row_copy(dcur_ref, slot, t, kk).wait()
        return carry
    lax.fori_loop(0, tt, wait, 0)

    gate = gate_ref[...]
    gks = [jnp.broadcast_to(gate[:, kk:kk + 1], (tt, LANES_V7X)) for kk in range(TOP_K)]
    for s in range(TOK_SUB):
        ca = pl.ds(s * LANES_V7X, LANES_V7X)
        cb = pl.ds(half + s * LANES_V7X, LANES_V7X)
        ya = x_ref[:, ca]
        yb = x_ref[:, cb]
        for kk in range(TOP_K):
            hi, lo = _unpack_bf16_pairs(_load_token_slab(buf.at[slot, kk], 0, tt, s))
            ya = ya + gks[kk] * hi
            yb = yb + gks[kk] * lo
        y_scr[:, ca] = ya
        y_scr[:, cb] = yb
    out_ref[...] = _rms(y_scr[...], nw_ref[...])


def _combine(dest3, gates, x2, yrows, final_w):
    T, D = x2.shape
    half = TOK_WORDS
    tt = min(MOE_TT, T)
    n = T // tt
    dspec = lambda f: pl.BlockSpec((None, 1, tt * TOP_K), lambda i: (f(i), 0, 0), memory_space=pltpu.SMEM)
    return pl.pallas_call(
        _combine_kernel,
        out_shape=jax.ShapeDtypeStruct((T, D), F32),
        grid=(n,),
        in_specs=[
            dspec(lambda i: i), dspec(lambda i: jnp.minimum(i + 1, n - 1)),
            pl.BlockSpec((tt, LANES_V7X), lambda i: (i, 0)),
            pl.BlockSpec((tt, D), lambda i: (i, 0)),
            pl.BlockSpec(memory_space=pl.ANY),
            pl.BlockSpec((1, D), lambda i: (0, 0)),
        ],
        out_specs=pl.BlockSpec((tt, D), lambda i: (i, 0)),
        scratch_shapes=[pltpu.VMEM((2, TOP_K, tt * TOK_SUB, LANES_V7X), U32), pltpu.VMEM((tt, D), F32),
                        pltpu.SemaphoreType.DMA((2,))],
        compiler_params=pltpu.CompilerParams(
            dimension_semantics=("arbitrary",),
            vmem_limit_bytes=_vmem_limit(2 * TOP_K * tt * half * 4 + 8 * tt * D * 4 + (8 << 20))),
        name="moe_combine",
    )(dest3, dest3, gates, x2, yrows, final_w)


def _route(top_e, n_blocks, n_vis):
    flat_e = top_e.reshape(-1)
    onehot = (flat_e[:, None] == jnp.arange(N_EXPERTS, dtype=I32)[None, :]).astype(I32)
    incl = jnp.cumsum(onehot, axis=0)
    counts = incl[-1]
    pos = jnp.sum((incl - onehot) * onehot, axis=1)
    nblk = (counts + MOE_BLK - 1) // MOE_BLK
    bend = jnp.cumsum(nblk)
    bstart = bend - nblk
    dest = jnp.sum(onehot * bstart[None, :], axis=1) * MOE_BLK + pos

    nvis = (nblk + MOE_G - 1) // MOE_G
    bpv = (nblk + jnp.maximum(nvis, 1) - 1) // jnp.maximum(nvis, 1)
    vend = jnp.cumsum(nvis)
    vstart = vend - nvis
    n_used = vend[-1]
    v = jnp.minimum(jnp.arange(n_vis, dtype=I32), n_used - 1)[:, None]
    sel = ((vstart[None, :] <= v) & (v < vend[None, :])).astype(I32)
    pick = lambda table: jnp.sum(sel * table[None, :], axis=1)
    visit_e = pick(jnp.arange(N_EXPERTS, dtype=I32))
    local = v[:, 0] - pick(vstart)
    visit_first = pick(bstart) + local * pick(bpv)
    visit_nb = jnp.clip(pick(nblk) - local * pick(bpv), 0, pick(bpv))
    visit_nb = jnp.where(jnp.arange(n_vis) < n_used, visit_nb, 0)

    blk = jnp.arange(n_blocks, dtype=I32)
    is_last = jnp.any((blk[:, None] == (bend - 1)[None, :]) & (nblk > 0)[None, :], axis=1)
    partial = (is_last | (blk >= bend[-1])).astype(I32)
    as_i32 = lambda a: a.astype(I32)
    return (as_i32(dest), as_i32(visit_e), as_i32(visit_first), as_i32(visit_nb),
            as_i32(n_used).reshape(1), as_i32(bend[-1]).reshape(1), partial)


def kernel(x, mem, norm_mix_w, w_in, conv_w, conv_b, if_bias, m_head_norm_w, w_branch_m, w_branch_a, rel_bias, w_out, norm_x_w, norm_mem_w, wq_x, wkv_x, wo_x, norm_ffn_w, w_router, b_router, w_gu, b_gu, w_down, b_down, final_norm_w):
    B, S, D = x.shape
    T = B * S
    l = 0
    x2d = x.reshape(T, D)

    wi = w_in[l]
    o_qm, o_km, o_vm, o_om = 0, M_WIDTH, 2 * M_WIDTH, 3 * M_WIDTH
    o_if = 4 * M_WIDTH
    o_qa = o_if + 2 * M_HEADS
    o_ka, o_va = o_qa + A_WIDTH, o_qa + 2 * A_WIDTH
    o_gm = o_qa + 3 * A_WIDTH
    o_ga = o_gm + D
    seg = lambda o, w: wi[:, o:o + w]
    w_main = jnp.concatenate(
        [seg(o_gm, D), seg(o_ga, D), seg(o_qm, M_WIDTH), seg(o_km, M_WIDTH), seg(o_vm, M_WIDTH),
         seg(o_om, M_WIDTH), seg(o_qa, A_WIDTH), seg(o_ka, A_WIDTH), seg(o_va, A_WIDTH)], axis=1).astype(BF16)
    w_if = jnp.pad(seg(o_if, 2 * M_HEADS), ((0, 0), (0, LANES_V7X - 2 * M_HEADS))).astype(BF16)

    proj, if_rows = _inproj(x2d, norm_mix_w[l].reshape(1, D), w_main, w_if)
    p3 = proj.reshape(B, S, N_MAIN)

    nc = S // CHUNK
    gcol3 = if_rows.reshape(B, S, LANES_V7X)
    grow4 = if_rows[:, :2 * M_HEADS].reshape(B, nc, CHUNK, 2 * M_HEADS).transpose(0, 1, 3, 2)
    bias_col = jnp.pad(if_bias[l], (0, LANES_V7X - 2 * M_HEADS)).reshape(1, LANES_V7X)
    bias_row = jnp.broadcast_to(if_bias[l][:, None], (2 * M_HEADS, CHUNK))
    hm = _mlstm(p3, gcol3, grow4, conv_w[l], conv_b[l].reshape(1, 2 * M_WIDTH), bias_col, bias_row,
                m_head_norm_w[l].reshape(1, M_WIDTH))

    ya = _band_attn(p3, _band_bias(rel_bias[l]))

    x1 = _merge(hm.reshape(T, M_WIDTH), ya.reshape(T, A_WIDTH), proj, x2d,
                w_branch_m[l].astype(BF16), w_branch_a[l].astype(BF16), w_out[l].astype(BF16))

    km, vm = _memkv(mem, norm_mem_w[l].reshape(1, D), wkv_x[l].astype(BF16))
    wr = jnp.pad(w_router[l], ((0, 0), (0, LANES_V7X - N_EXPERTS)))
    wr_hi = wr.astype(BF16)
    wr_lo = (wr - wr_hi.astype(F32)).astype(BF16)
    br = jnp.concatenate([b_router[l], jnp.full((LANES_V7X - N_EXPERTS,), NEG_BIG, F32)]).reshape(1, LANES_V7X)
    x2, h3p, te, tg = _xattn(x1, S, norm_x_w[l].reshape(1, D), wq_x[l].astype(BF16), km, vm,
                             wo_x[l].astype(BF16), norm_ffn_w[l].reshape(1, D), wr_hi, wr_lo, br)

    A = T * TOP_K
    n_blocks = A // MOE_BLK + N_EXPERTS
    n_vis = (n_blocks + N_EXPERTS * (MOE_G - 1)) // MOE_G
    dest, visit_e, visit_first, visit_nb, n_used, n_blocks_used, partial = _route(te[:, :TOP_K], n_blocks, n_vis)
    dest3 = (dest * TOK_SUB).reshape(T // min(MOE_TT, T), 1, -1)
    xs = _dispatch(partial, dest3, h3p)
    yrows = _experts(visit_e, visit_first, visit_nb, n_used, n_blocks_used, xs, w_gu[l],
                     b_gu[l].reshape(N_EXPERTS, 2 * D_FF // MOE_TF, MOE_TF),
                     w_down[l], b_down[l].reshape(N_EXPERTS, 1, D))
    out = _combine(dest3, tg, x2, yrows, final_norm_w.reshape(1, D))
    return out.reshape(B, S, D)
```

```python
import functools

import jax
import jax.numpy as jnp
from jax import lax
from jax.experimental import pallas as pl
from jax.experimental.pallas import tpu as pltpu

F32 = jnp.float32
BF16 = jnp.bfloat16
I32 = jnp.int32
U32 = jnp.uint32

D_MODEL = 2048
CHUNK = 64
EPS = 1e-6
M_HEADS = 4
M_HEAD_DIM = 256
M_WIDTH = M_HEADS * M_HEAD_DIM
CONV_WIDTH = 4
A_HEADS = 16
A_HEAD_DIM = 64
A_WIDTH = A_HEADS * A_HEAD_DIM
LEFT_CHUNKS = 8
MAX_REL = 128
X_HEADS = 4
X_HEAD_DIM = 128
X_WIDTH = X_HEADS * X_HEAD_DIM
N_EXPERTS = 32
TOP_K = 4
D_FF = D_MODEL
SWIGLU_LIMIT = 7.0
SWIGLU_ALPHA = 1.702

LANES_V7X = 128
SUBLANES_V7X = 8
VMEM_BYTES_V7X = 64 * 1024 * 1024
VMEM_LIMIT_CAP = 60000 * 1024
DMA_PRIORITIES = 2

NEG_BIG = -1e30

COL_GM = 0
COL_GA = 2048
COL_QM = 4096
COL_KM = 5120
COL_VM = 6144
COL_OM = 7168
COL_QA = 8192
COL_KA = 9216
COL_VA = 10240
N_MAIN = 11264

IN_TM = 1024
IN_TN = 1024
ATT_TQ = 512
ATT_SUB = 128
ATT_WIN = ATT_SUB + LEFT_CHUNKS * CHUNK
MERGE_TM = 512
XATT_TM = 512
MOE_BLK = 256
MOE_G = 9
MOE_R = MOE_G * MOE_BLK
MOE_BIG = 8
MOE_MID = 4
MOE_RING = 4
MOE_TF = 256
MOE_TT = 256


def _vmem_limit(nbytes):
    return int(min(VMEM_LIMIT_CAP, max(nbytes, 16 * 1024 * 1024)))


def _nt_dot(a, b):
    return lax.dot_general(a, b, (((1,), (1,)), ((), ())), preferred_element_type=F32)


def _tn_dot(a, b):
    return lax.dot_general(a, b, (((0,), (0,)), ((), ())), preferred_element_type=F32)


def _sigmoid(x):
    return 1.0 / (1.0 + jnp.exp(-x))


def _log_sigmoid(x):
    return jnp.minimum(x, 0.0) - jnp.log1p(jnp.exp(-jnp.abs(x)))


def _rms(x, w):
    return x * lax.rsqrt(jnp.mean(x * x, axis=-1, keepdims=True) + EPS) * w


def _pack_bf16_pairs(xb):
    n = xb.shape[1] // 2
    hi = lax.bitcast_convert_type(xb[:, :n].astype(F32), U32)
    lo = lax.bitcast_convert_type(xb[:, n:].astype(F32), U32)
    return hi | (lo >> 16)


def _unpack_bf16_pairs(w):
    hi = lax.bitcast_convert_type(w & jnp.uint32(0xFFFF0000), F32)
    lo = lax.bitcast_convert_type(w << 16, F32)
    return hi, lo


TOK_SUB = SUBLANES_V7X
TOK_WORDS = TOK_SUB * LANES_V7X


def _store_token_tiles(ref, tok0, packed):
    n = packed.shape[0]
    for s in range(TOK_SUB):
        ref[pl.ds(tok0 * TOK_SUB + s, n, stride=TOK_SUB), :] = packed[:, s * LANES_V7X:(s + 1) * LANES_V7X]


def _load_token_slab(ref, tok0, n, s):
    return ref[pl.ds(tok0 * TOK_SUB + s, n, stride=TOK_SUB), :]


def _inproj_kernel(x_ref, nw_ref, w_ref, wif_ref, o_ref, if_ref, h_scr):
    @pl.when(pl.program_id(1) == 0)
    def _():
        hb = _rms(x_ref[...], nw_ref[...]).astype(BF16)
        h_scr[...] = hb
        if_ref[...] = jnp.dot(hb, wif_ref[...], preferred_element_type=F32)

    o_ref[...] = jnp.dot(h_scr[...], w_ref[...], preferred_element_type=F32).astype(o_ref.dtype)


def _inproj(x2d, norm_w, w_main, w_if):
    T, D = x2d.shape
    tm = min(IN_TM, T)
    grid = (T // tm, N_MAIN // IN_TN)
    vmem = 2 * tm * D * 4 + tm * D * 2 + 2 * D * IN_TN * 2 + 2 * tm * IN_TN * 2 + 4 * tm * LANES_V7X * 4
    return pl.pallas_call(
        _inproj_kernel,
        out_shape=(jax.ShapeDtypeStruct((T, N_MAIN), BF16), jax.ShapeDtypeStruct((T, LANES_V7X), F32)),
        grid=grid,
        in_specs=[
            pl.BlockSpec((tm, D), lambda i, j: (i, 0)),
            pl.BlockSpec((1, D), lambda i, j: (0, 0)),
            pl.BlockSpec((D, IN_TN), lambda i, j: (0, j)),
            pl.BlockSpec((D, LANES_V7X), lambda i, j: (0, 0)),
        ],
        out_specs=(
            pl.BlockSpec((tm, IN_TN), lambda i, j: (i, j)),
            pl.BlockSpec((tm, LANES_V7X), lambda i, j: (i, 0)),
        ),
        scratch_shapes=[pltpu.VMEM((tm, D), BF16)],
        compiler_params=pltpu.CompilerParams(
            dimension_semantics=("parallel", "arbitrary"),
            vmem_limit_bytes=_vmem_limit(vmem + (8 << 20))),
        name="inproj",
    )(x2d, norm_w, w_main, w_if)


def _mlstm_kernel(q_ref, k_ref, v_ref, o_ref, gc_ref, gr_ref, cw_ref, cb_ref, bc_ref, br_ref, nw_ref,
                  out_ref, tail, c_st, n_st, m_st):
    nb = q_ref.shape[0]
    L = CHUNK
    dh = M_HEAD_DIM
    halo = tail.shape[1]

    @pl.when(pl.program_id(0) == 0)
    def _():
        tail[...] = jnp.zeros_like(tail)
        c_st[...] = jnp.zeros_like(c_st)
        n_st[...] = jnp.zeros_like(n_st)
        m_st[...] = jnp.zeros_like(m_st)

    row = lax.broadcasted_iota(I32, (L, L), 0)
    col = lax.broadcasted_iota(I32, (L, L), 1)
    lower = col <= row

    srow = lax.broadcasted_iota(I32, (L, halo + L), 0)
    scol = lax.broadcasted_iota(I32, (L, halo + L), 1)
    shift_mat = jnp.concatenate(
        [jnp.where(scol == srow + (halo - (CONV_WIDTH - 1) + j), 1.0, 0.0) for j in range(CONV_WIDTH - 1)],
        axis=0).astype(BF16)

    erow = lax.broadcasted_iota(I32, (dh, dh), 0)
    ecol = lax.broadcasted_iota(I32, (dh, dh), 1)
    eye = jnp.where(erow == ecol, 1.0, 0.0).astype(BF16)

    streams = [(b, h) for b in range(nb) for h in range(M_HEADS)]

    qk = []
    for b in range(nb):
        halves = []
        for half, ref in enumerate((q_ref, k_ref)):
            cols = slice(half * M_WIDTH, (half + 1) * M_WIDTH)
            cur = ref[b]
            ext = jnp.concatenate([tail[b, :, cols], cur], axis=0)
            back = jnp.dot(shift_mat, ext, preferred_element_type=F32)
            conv = cb_ref[:, cols] + cur.astype(F32) * cw_ref[CONV_WIDTH - 1:CONV_WIDTH, cols]
            for j in range(CONV_WIDTH - 1):
                conv = conv + back[j * L:(j + 1) * L] * cw_ref[j:j + 1, cols]
            halves.append(conv * _sigmoid(conv))
            tail[b, :, cols] = cur[L - halo:L]
        qk.append(halves)

    st = []
    for b, h in streams:
        s = b * M_HEADS + h
        lo, hi = h * dh, (h + 1) * dh
        gcol = gc_ref[b] + bc_ref[...]
        grow = gr_ref[b, 0] + br_ref[...]
        q = qk[b][0][:, lo:hi]
        k = qk[b][1][:, lo:hi] * (dh ** -0.5)
        ig_c = gcol[:, h:h + 1]
        ig_r = grow[h:h + 1, :]
        ls_c = _log_sigmoid(gcol[:, M_HEADS + h:M_HEADS + h + 1])
        ls_r = _log_sigmoid(grow[M_HEADS + h:M_HEADS + h + 1, :])
        b_c = jnp.sum(jnp.where(lower, ls_r, 0.0), axis=1, keepdims=True)
        b_r = jnp.sum(jnp.where(row <= col, ls_c, 0.0), axis=0, keepdims=True)
        g = jnp.sum(ls_r, axis=1, keepdims=True)
        m_old = m_st[s][:, 0:1]
        w_end = g - b_c + ig_c
        a = jnp.max(w_end, axis=0, keepdims=True)
        dlog = jnp.where(lower, b_c - b_r + ig_r, NEG_BIG)
        inter = b_c + m_old
        m_j = jnp.maximum(inter, jnp.max(dlog, axis=1, keepdims=True))
        st.append(dict(s=s, lo=lo, hi=hi, b=b, q=q, k=k, qb=q.astype(BF16), kb=k.astype(BF16),
                       vb=v_ref[b, :, lo:hi], g=g, a=a, m_old=m_old, m_j=m_j,
                       e_end=jnp.exp(w_end - a), decay=jnp.exp(dlog - m_j), w_int=jnp.exp(inter - m_j),
                       c_old=c_st[s], n_old=n_st[s]))

    for d in st:
        d["sm"] = _nt_dot(d["qb"], d["kb"]) * d["decay"]
    for d in st:
        d["num"] = (jnp.dot(d["sm"].astype(BF16), d["vb"], preferred_element_type=F32)
                    + d["w_int"] * _nt_dot(d["qb"], d["c_old"].astype(BF16)))
    for d in st:
        den = (jnp.sum(d["sm"], axis=1, keepdims=True)
               + d["w_int"] * jnp.sum(d["q"] * d["n_old"], axis=1, keepdims=True))
        hh = d["num"] / jnp.maximum(jnp.abs(den), jnp.exp(-d["m_j"]))
        hn = _rms(hh, nw_ref[:, d["lo"]:d["hi"]])
        og = _sigmoid(o_ref[d["b"], :, d["lo"]:d["hi"]].astype(F32))
        out_ref[d["b"], :, d["lo"]:d["hi"]] = (og * hn).astype(out_ref.dtype)
    for d in st:
        s = d["s"]
        m_new = jnp.maximum(d["g"] + d["m_old"], d["a"])
        s_prev = jnp.exp(d["g"] + d["m_old"] - m_new)
        s_loc = jnp.exp(d["a"] - m_new)
        ve = (d["vb"].astype(F32) * (d["e_end"] * s_loc)).astype(BF16)
        ve_t = _nt_dot(eye, ve).astype(BF16)
        c_st[s] = s_prev * d["c_old"] + jnp.dot(ve_t, d["kb"], preferred_element_type=F32)
        n_st[s] = s_prev * d["n_old"] + s_loc * jnp.sum(d["e_end"] * d["k"], axis=0, keepdims=True)
        m_st[s] = jnp.broadcast_to(m_new, (1, LANES_V7X))


def _mlstm(p3, gcol3, grow4, conv_w, conv_b, bias_col, bias_row, head_w):
    B, S, _ = p3.shape
    nc = S // CHUNK
    ns = B * M_HEADS
    blk = lambda cb: pl.BlockSpec((B, CHUNK, M_WIDTH), lambda c, cb=cb: (0, c, cb))
    full = lambda shape: pl.BlockSpec(shape, lambda c: (0,) * len(shape))
    vmem = (2 * 5 * B * CHUNK * M_WIDTH * 2 + B * (CHUNK + 8) * 2 * M_WIDTH * 4
            + ns * M_HEAD_DIM * M_HEAD_DIM * 4 + (16 << 20))
    return pl.pallas_call(
        _mlstm_kernel,
        out_shape=jax.ShapeDtypeStruct((B, S, M_WIDTH), BF16),
        grid=(nc,),
        in_specs=[
            blk(COL_QM // M_WIDTH), blk(COL_KM // M_WIDTH), blk(COL_VM // M_WIDTH), blk(COL_OM // M_WIDTH),
            pl.BlockSpec((B, CHUNK, LANES_V7X), lambda c: (0, c, 0)),
            pl.BlockSpec((B, 1, 2 * M_HEADS, CHUNK), lambda c: (0, c, 0, 0)),
            full((CONV_WIDTH, 2 * M_WIDTH)), full((1, 2 * M_WIDTH)),
            full((1, LANES_V7X)), full((2 * M_HEADS, CHUNK)), full((1, M_WIDTH)),
        ],
        out_specs=pl.BlockSpec((B, CHUNK, M_WIDTH), lambda c: (0, c, 0)),
        scratch_shapes=[
            pltpu.VMEM((B, 2 * SUBLANES_V7X, 2 * M_WIDTH), BF16),
            pltpu.VMEM((ns, M_HEAD_DIM, M_HEAD_DIM), F32),
            pltpu.VMEM((ns, 1, M_HEAD_DIM), F32),
            pltpu.VMEM((ns, 1, LANES_V7X), F32),
        ],
        compiler_params=pltpu.CompilerParams(
            dimension_semantics=("arbitrary",), vmem_limit_bytes=_vmem_limit(vmem)),
        name="mlstm",
    )(p3, p3, p3, p3, gcol3, grow4, conv_w, conv_b, bias_col, bias_row, head_w)


def _band_attn_kernel(q_ref, kp_ref, kc_ref, vp_ref, vc_ref, bias_ref, out_ref):
    qb_idx = pl.program_id(2)
    tq = q_ref.shape[1]
    q2 = q_ref[0]
    kcat = jnp.concatenate([kp_ref[0], kc_ref[0]], axis=0)
    vcat = jnp.concatenate([vp_ref[0], vc_ref[0]], axis=0)
    lane = lax.broadcasted_iota(I32, (1, LANES_V7X), 1)
    wcol = lax.broadcasted_iota(I32, (1, ATT_WIN), 1)
    first_pen = jnp.where(qb_idx == 0, NEG_BIG, 0.0)
    scale = A_HEAD_DIM ** -0.5
    bias = bias_ref[0]

    subs = []
    for t in range(tq // ATT_SUB):
        r0 = t * ATT_SUB
        w0 = tq + r0 - LEFT_CHUNKS * CHUNK
        pen = jnp.where(wcol < (tq - w0), first_pen, 0.0)
        qs = q2[r0:r0 + ATT_SUB] * scale
        zero = jnp.zeros_like(qs)
        qst = jnp.concatenate([jnp.where(lane < A_HEAD_DIM, qs, zero),
                               jnp.where(lane >= A_HEAD_DIM, qs, zero)], axis=0)
        subs.append(dict(r0=r0, w0=w0, pen=pen, qst=qst))
    for d in subs:
        d["sc"] = _nt_dot(d["qst"], kcat[d["w0"]:d["w0"] + ATT_WIN]) + bias + d["pen"]
    for d in subs:
        p = jnp.exp(d["sc"] - jnp.max(d["sc"], axis=1, keepdims=True))
        d["den"] = jnp.sum(p, axis=1, keepdims=True)
        d["p"] = p.astype(BF16)
    for d in subs:
        d["o"] = jnp.dot(d["p"], vcat[d["w0"]:d["w0"] + ATT_WIN], preferred_element_type=F32) / d["den"]
    for d in subs:
        o, r0 = d["o"], d["r0"]
        out_ref[0, r0:r0 + ATT_SUB, :] = jnp.where(
            lane < A_HEAD_DIM, o[:ATT_SUB], o[ATT_SUB:]).astype(out_ref.dtype)


def _band_attn(p3, bias):
    B, S, _ = p3.shape
    tq = ATT_TQ
    nq = S // tq
    nhp = A_HEADS // 2
    cq, ck, cv = COL_QA // LANES_V7X, COL_KA // LANES_V7X, COL_VA // LANES_V7X
    cur = lambda c0: pl.BlockSpec((1, tq, LANES_V7X), lambda b, hp, i, c0=c0: (b, i, c0 + hp))
    prev = lambda c0: pl.BlockSpec((1, tq, LANES_V7X), lambda b, hp, i, c0=c0: (b, jnp.maximum(i - 1, 0), c0 + hp))
    return pl.pallas_call(
        _band_attn_kernel,
        out_shape=jax.ShapeDtypeStruct((B, S, A_WIDTH), BF16),
        grid=(B, nhp, nq),
        in_specs=[cur(cq), prev(ck), cur(ck), prev(cv), cur(cv),
                  pl.BlockSpec((1, 2 * ATT_SUB, ATT_WIN), lambda b, hp, i: (hp, 0, 0))],
        out_specs=pl.BlockSpec((1, tq, LANES_V7X), lambda b, hp, i: (b, i, hp)),
        compiler_params=pltpu.CompilerParams(
            dimension_semantics=("parallel", "parallel", "arbitrary"),
            vmem_limit_bytes=_vmem_limit(32 << 20)),
        name="band_attn",
    )(p3, p3, p3, p3, p3, bias)


def _band_bias(rel_table):
    n = 6 * LANES_V7X
    c = jnp.arange(n)
    d = jnp.where(c < ATT_WIN, c, c - n)
    rel = jnp.clip(LEFT_CHUNKS * CHUNK - d, -MAX_REL, MAX_REL) + MAX_REL
    vec = rel_table[:, rel].astype(F32)
    sheared = jnp.tile(vec, (1, ATT_SUB))[:, :ATT_SUB * (n - 1)].reshape(A_HEADS, ATT_SUB, n - 1)
    i = jnp.arange(ATT_SUB)[:, None]
    kp = jnp.arange(ATT_WIN)[None, :]
    ci, kc = i // CHUNK, kp // CHUNK
    valid = (kc >= ci) & (kc <= ci + LEFT_CHUNKS)
    bias = jnp.where(valid[None], sheared[:, :, :ATT_WIN], NEG_BIG)
    return bias.reshape(A_HEADS // 2, 2 * ATT_SUB, ATT_WIN)


def _merge_kernel(hm_ref, ya_ref, gm_ref, ga_ref, x_ref, wbm_ref, wba_ref, wo_ref, out_ref):
    ym = jnp.dot(hm_ref[...], wbm_ref[...], preferred_element_type=F32)
    ya = jnp.dot(ya_ref[...], wba_ref[...], preferred_element_type=F32)
    merged = _sigmoid(gm_ref[...].astype(F32)) * ym + _sigmoid(ga_ref[...].astype(F32)) * ya
    out_ref[...] = x_ref[...] + jnp.dot(merged.astype(BF16), wo_ref[...], preferred_element_type=F32)


def _const_spec(shape):
    return pl.BlockSpec(shape, lambda i: (0,) * len(shape), pipeline_mode=pl.Buffered(1))


def _merge(hm2d, ya2d, p2d, x2d, wbm, wba, wout):
    T, D = x2d.shape
    tm = min(MERGE_TM, T)
    vmem = ((2 * M_WIDTH * D + D * D) * 2 + 2 * (2 * tm * M_WIDTH * 2 + 2 * tm * D * 2 + 2 * tm * D * 4)
            + 6 * tm * D * 4)
    return pl.pallas_call(
        _merge_kernel,
        out_shape=jax.ShapeDtypeStruct((T, D), F32),
        grid=(T // tm,),
        in_specs=[
            pl.BlockSpec((tm, M_WIDTH), lambda i: (i, 0)),
            pl.BlockSpec((tm, A_WIDTH), lambda i: (i, 0)),
            pl.BlockSpec((tm, D), lambda i: (i, COL_GM // D)),
            pl.BlockSpec((tm, D), lambda i: (i, COL_GA // D)),
            pl.BlockSpec((tm, D), lambda i: (i, 0)),
            _const_spec((M_WIDTH, D)), _const_spec((A_WIDTH, D)), _const_spec((D, D)),
        ],
        out_specs=pl.BlockSpec((tm, D), lambda i: (i, 0)),
        compiler_params=pltpu.CompilerParams(
            dimension_semantics=("parallel",), vmem_limit_bytes=_vmem_limit(vmem)),
        name="merge_out",
    )(hm2d, ya2d, p2d, p2d, x2d, wbm, wba, wout)


def _memkv_kernel(mem_ref, nw_ref, w_ref, k_ref, v_ref):
    hb = _rms(mem_ref[0], nw_ref[...]).astype(BF16)
    kv = jnp.dot(hb, w_ref[...], preferred_element_type=F32)
    k_ref[0] = kv[:, :X_WIDTH].astype(k_ref.dtype)
    v_ref[0] = kv[:, X_WIDTH:].astype(v_ref.dtype)


def _memkv(mem, norm_w, wkv):
    B, N, D = mem.shape
    return pl.pallas_call(
        _memkv_kernel,
        out_shape=(jax.ShapeDtypeStruct((B, N, X_WIDTH), BF16), jax.ShapeDtypeStruct((B, N, X_WIDTH), BF16)),
        grid=(B,),
        in_specs=[pl.BlockSpec((1, N, D), lambda b: (b, 0, 0)),
                  pl.BlockSpec((1, D), lambda b: (0, 0)),
                  pl.BlockSpec((D, 2 * X_WIDTH), lambda b: (0, 0))],
        out_specs=(pl.BlockSpec((1, N, X_WIDTH), lambda b: (b, 0, 0)),
                   pl.BlockSpec((1, N, X_WIDTH), lambda b: (b, 0, 0))),
        compiler_params=pltpu.CompilerParams(
            dimension_semantics=("parallel",), vmem_limit_bytes=_vmem_limit(24 << 20)),
        name="mem_kv",
    )(mem, norm_w, wkv)


def _xattn_kernel(x_ref, nxw_ref, wq_ref, km_ref, vm_ref, wo_ref, nfw_ref, wrh_ref, wrl_ref, br_ref,
                  x2_ref, h3_ref, te_ref, tg_ref):
    x1 = x_ref[...]
    hb = _rms(x1, nxw_ref[...]).astype(BF16)
    q = jnp.dot(hb, wq_ref[...], preferred_element_type=F32).astype(BF16)
    km = km_ref[0]
    vm = vm_ref[0]
    outs = []
    for h in range(X_HEADS):
        lo, hi = h * X_HEAD_DIM, (h + 1) * X_HEAD_DIM
        sc = _nt_dot(q[:, lo:hi], km[:, lo:hi]) * (X_HEAD_DIM ** -0.5)
        mx = jnp.max(sc, axis=1, keepdims=True)
        p = jnp.exp(sc - mx)
        den = jnp.sum(p, axis=1, keepdims=True)
        outs.append(jnp.dot(p.astype(BF16), vm[:, lo:hi], preferred_element_type=F32) / den)
    o = jnp.concatenate(outs, axis=1).astype(BF16)
    x2 = x1 + jnp.dot(o, wo_ref[...], preferred_element_type=F32)
    x2_ref[...] = x2

    h3 = _rms(x2, nfw_ref[...])
    h_hi = h3.astype(BF16)
    _store_token_tiles(h3_ref, 0, _pack_bf16_pairs(h_hi))
    h_lo = (h3 - h_hi.astype(F32)).astype(BF16)
    logits = (jnp.dot(h_hi, wrh_ref[...], preferred_element_type=F32)
              + jnp.dot(h_lo, wrh_ref[...], preferred_element_type=F32)
              + jnp.dot(h_hi, wrl_ref[...], preferred_element_type=F32)
              + br_ref[...])

    colf = lax.broadcasted_iota(I32, logits.shape, 1).astype(F32)
    work = logits
    vals, idxs = [], []
    for _ in range(TOP_K):
        mx = jnp.max(work, axis=1, keepdims=True)
        ix = jnp.min(jnp.where(work == mx, colf, float(LANES_V7X)), axis=1, keepdims=True)
        vals.append(mx)
        idxs.append(ix)
        work = jnp.where(colf == ix, -jnp.inf, work)
    es = [jnp.exp(v - vals[0]) for v in vals]
    tot = es[0] + es[1] + es[2] + es[3]
    te = jnp.zeros_like(logits)
    tg = jnp.zeros_like(logits)
    for kk in range(TOP_K):
        te = jnp.where(colf == float(kk), idxs[kk], te)
        tg = jnp.where(colf == float(kk), es[kk] / tot, tg)
    te_ref[...] = te.astype(I32)
    tg_ref[...] = tg


def _xattn(x1, batch_rows, norm_x_w, wq, km, vm, wo, norm_ffn_w, wr_hi, wr_lo, br):
    T, D = x1.shape
    tm = min(XATT_TM, batch_rows)
    per_b = batch_rows // tm
    n_mem = km.shape[1]
    row = lambda w: pl.BlockSpec((tm, w), lambda i: (i, 0))
    vmem = (2 * D * X_WIDTH * 2 + 2 * D * LANES_V7X * 2 + 4 * n_mem * X_WIDTH * 2
            + 2 * 3 * tm * D * 4 + 8 * tm * D * 4)
    return pl.pallas_call(
        _xattn_kernel,
        out_shape=(jax.ShapeDtypeStruct((T, D), F32), jax.ShapeDtypeStruct((T * TOK_SUB, LANES_V7X), U32),
                   jax.ShapeDtypeStruct((T, LANES_V7X), I32), jax.ShapeDtypeStruct((T, LANES_V7X), F32)),
        grid=(T // tm,),
        in_specs=[
            row(D), _const_spec((1, D)), _const_spec((D, X_WIDTH)),
            pl.BlockSpec((1, n_mem, X_WIDTH), lambda i: (i // per_b, 0, 0)),
            pl.BlockSpec((1, n_mem, X_WIDTH), lambda i: (i // per_b, 0, 0)),
            _const_spec((X_WIDTH, D)), _const_spec((1, D)),
            _const_spec((D, LANES_V7X)), _const_spec((D, LANES_V7X)), _const_spec((1, LANES_V7X)),
        ],
        out_specs=(row(D), pl.BlockSpec((tm * TOK_SUB, LANES_V7X), lambda i: (i, 0)),
                   row(LANES_V7X), row(LANES_V7X)),
        compiler_params=pltpu.CompilerParams(
            dimension_semantics=("parallel",), vmem_limit_bytes=_vmem_limit(vmem)),
        name="xattn_router",
    )(x1, norm_x_w, wq, km, vm, wo, norm_ffn_w, wr_hi, wr_lo, br)


def _dispatch_kernel(partial_ref, dest_ref, h_ref, xs_ref, zero_scr, sem, zsem):
    tt = h_ref.shape[0] // TOK_SUB
    nblocks = partial_ref.shape[0]

    @pl.when(pl.program_id(0) == 0)
    def _():
        zero_scr[...] = jnp.zeros_like(zero_scr)

        def zero_copy(b):
            row0 = pl.multiple_of(b * (MOE_BLK * TOK_SUB), MOE_BLK * TOK_SUB)
            return pltpu.make_async_copy(zero_scr, xs_ref.at[pl.ds(row0, MOE_BLK * TOK_SUB)], zsem)

        def zstart(b, carry):
            @pl.when(partial_ref[b] > 0)
            def _():
                zero_copy(b).start()
            return carry

        def zwait(b, carry):
            @pl.when(partial_ref[b] > 0)
            def _():
                zero_copy(b).wait()
            return carry

        lax.fori_loop(0, nblocks, zstart, 0)
        lax.fori_loop(0, nblocks, zwait, 0)

    def tile_copy(t, kk):
        src = h_ref.at[pl.ds(pl.multiple_of(t * TOK_SUB, TOK_SUB), TOK_SUB)]
        dst = xs_ref.at[pl.ds(pl.multiple_of(dest_ref[0, t * TOP_K + kk], TOK_SUB), TOK_SUB)]
        return pltpu.make_async_copy(src, dst, sem)

    def start(t, carry):
        for kk in range(TOP_K):
            tile_copy(t, kk).start(priority=kk % DMA_PRIORITIES)
        return carry

    def wait(t, carry):
        for kk in range(TOP_K):
            tile_copy(t, kk).wait()
        return carry

    lax.fori_loop(0, tt, start, 0)
    lax.fori_loop(0, tt, wait, 0)


def _dispatch(partial, dest3, h3p):
    T = h3p.shape[0] // TOK_SUB
    n_rows = partial.shape[0] * MOE_BLK
    tt = min(MOE_TT, T)
    grid_spec = pltpu.PrefetchScalarGridSpec(
        num_scalar_prefetch=1,
        grid=(T // tt,),
        in_specs=[
            pl.BlockSpec((None, 1, tt * TOP_K), lambda i, nb: (i, 0, 0), memory_space=pltpu.SMEM),
            pl.BlockSpec((tt * TOK_SUB, LANES_V7X), lambda i, nb: (i, 0)),
        ],
        out_specs=pl.BlockSpec(memory_space=pl.ANY),
        scratch_shapes=[pltpu.VMEM((MOE_BLK * TOK_SUB, LANES_V7X), U32),
                        pltpu.SemaphoreType.DMA(()), pltpu.SemaphoreType.DMA(())],
    )
    return pl.pallas_call(
        _dispatch_kernel,
        out_shape=jax.ShapeDtypeStruct((n_rows * TOK_SUB, LANES_V7X), U32),
        grid_spec=grid_spec,
        compiler_params=pltpu.CompilerParams(
            dimension_semantics=("arbitrary",), has_side_effects=True,
            vmem_limit_bytes=_vmem_limit(4 * tt * TOK_WORDS * 4 + MOE_BLK * TOK_WORDS * 4 + (8 << 20))),
        name="moe_dispatch",
    )(partial, dest3, h3p)


def _experts_kernel(ve_ref, vf_ref, nb_ref, nu_ref, tb_ref, bgu_ref, bd_ref, xs_hbm, wgu_hbm, wdn_hbm, out_hbm,
                    xb_scr, acc_scr, x_buf, o_buf, wg_buf, wu_buf, wd_buf, wgb_scr, wub_scr, wdb_scr,
                    xsems, osems, sems):
    v = pl.program_id(0)
    nb = nb_ref[v]
    first = vf_ref[v]
    n_used = nu_ref[0]
    half = TOK_WORDS
    tf = wg_buf.shape[2]
    nj = D_FF // tf
    blk_tiles = MOE_BLK * TOK_SUB
    n_blocks = out_hbm.shape[0] // blk_tiles
    ring = x_buf.shape[0]

    def block_rows(b):
        return pl.ds(pl.multiple_of(b * MOE_BLK, MOE_BLK), MOE_BLK)

    def hbm_block(ref, blk):
        return ref.at[pl.ds(pl.multiple_of(blk * blk_tiles, blk_tiles), blk_tiles)]

    def x_copy(blk, slot):
        return pltpu.make_async_copy(hbm_block(xs_hbm, blk), x_buf.at[slot], xsems.at[slot])

    def o_copy(blk, slot):
        return pltpu.make_async_copy(o_buf.at[slot], hbm_block(out_hbm, blk), osems.at[slot])

    def tile_copies(e, j, slot):
        cols = pl.ds(pl.multiple_of(j * tf, tf), tf)
        ucols = pl.ds(pl.multiple_of(D_FF + j * tf, tf), tf)
        return (pltpu.make_async_copy(wgu_hbm.at[e, :, cols], wg_buf.at[slot], sems.at[slot, 0]),
                pltpu.make_async_copy(wgu_hbm.at[e, :, ucols], wu_buf.at[slot], sems.at[slot, 1]),
                pltpu.make_async_copy(wdn_hbm.at[e, cols, :], wd_buf.at[slot], sems.at[slot, 2]))

    def fetch(e, j, slot):
        for c in tile_copies(e, j, slot):
            c.start()

    @pl.when(nb > 0)
    def _():
        e = ve_ref[v]

        def prefetch_rows(first_blk, n_blk):
            for b in range(ring):
                @pl.when(b < n_blk)
                def _():
                    x_copy(first_blk + b, b).start()

        @pl.when(v == 0)
        def _():
            fetch(e, 0, 0)
            prefetch_rows(first, nb)

        def stage(b, carry):
            slot = lax.rem(b, ring)
            x_copy(first + b, slot).wait()
            rows = block_rows(b)
            for s in range(TOK_SUB):
                hi, lo = _unpack_bf16_pairs(_load_token_slab(x_buf.at[slot], 0, MOE_BLK, s))
                xb_scr[rows, s * LANES_V7X:(s + 1) * LANES_V7X] = hi.astype(BF16)
                xb_scr[rows, half + s * LANES_V7X:half + (s + 1) * LANES_V7X] = lo.astype(BF16)
            acc_scr[rows, :] = jnp.zeros((MOE_BLK, acc_scr.shape[1]), F32)

            @pl.when(b + ring < nb)
            def _():
                x_copy(first + b + ring, slot).start()
            return carry
        lax.fori_loop(0, nb, stage, 0)

        big = nb >= MOE_BIG
        mid = jnp.logical_and(nb >= MOE_MID, jnp.logical_not(big))

        def tile_step(j, carry):
            slot = lax.rem(j, 2)

            @pl.when(j + 1 < nj)
            def _():
                fetch(e, j + 1, 1 - slot)

            @pl.when(jnp.logical_and(j + 1 == nj, v + 1 < n_used))
            def _():
                fetch(ve_ref[v + 1], 0, 1 - slot)
                prefetch_rows(vf_ref[v + 1], nb_ref[v + 1])

            for c in tile_copies(e, j, slot):
                c.wait()

            def ffn_rows(rows, wg, wu, wd):
                xb = xb_scr[rows, :]
                g = jnp.dot(xb, wg, preferred_element_type=F32) + bgu_ref[pl.ds(j, 1), :]
                u = jnp.dot(xb, wu, preferred_element_type=F32) + bgu_ref[pl.ds(nj + j, 1), :]
                g = jnp.minimum(g, SWIGLU_LIMIT)
                u = jnp.clip(u, -SWIGLU_LIMIT, SWIGLU_LIMIT)
                act = (u + 1.0) * (g * _sigmoid(SWIGLU_ALPHA * g))
                acc_scr[rows, :] += jnp.dot(act.astype(BF16), wd, preferred_element_type=F32)

            def cast_weights():
                wg = wg_buf[slot].astype(BF16)
                wu = wu_buf[slot].astype(BF16)
                wd = wd_buf[slot].astype(BF16)
                wgb_scr[...] = wg
                wub_scr[...] = wu
                wdb_scr[...] = wd
                return wg, wu, wd

            @pl.when(big)
            def _():
                ffn_rows(pl.ds(0, MOE_BIG * MOE_BLK), *cast_weights())

            @pl.when(mid)
            def _():
                ffn_rows(pl.ds(0, MOE_MID * MOE_BLK), *cast_weights())

            @pl.when(nb < MOE_MID)
            def _():
                cast_weights()

            def block(b, c2):
                ffn_rows(block_rows(b), wgb_scr[...], wub_scr[...], wdb_scr[...])
                return c2
            lax.fori_loop(jnp.where(big, MOE_BIG, jnp.where(mid, MOE_MID, 0)), nb, block, 0)
            return carry
        lax.fori_loop(0, nj, tile_step, 0)

        def finish(b, carry):
            slot = lax.rem(b, ring)

            @pl.when(b >= ring)
            def _():
                o_copy(first + b - ring, slot).wait()

            y = (acc_scr[block_rows(b), :] + bd_ref[...]).astype(BF16)
            _store_token_tiles(o_buf.at[slot], 0, _pack_bf16_pairs(y))
            o_copy(first + b, slot).start()
            return carry
        lax.fori_loop(0, nb, finish, 0)

        for back in range(ring, 0, -1):
            @pl.when(nb >= back)
            def _():
                o_copy(first + nb - back, lax.rem(nb - back, ring)).wait()

    @pl.when(v == pl.num_programs(0) - 1)
    def _():
        o_buf[0] = jnp.zeros(o_buf.shape[1:], U32)

        def fill(b, carry):
            c = o_copy(b, 0)
            c.start()
            c.wait()
            return carry
        lax.fori_loop(tb_ref[0], n_blocks, fill, 0)


def _experts(visit_e, visit_first, visit_nb, n_used, n_blocks_used, xs, wgu, bgu, wdn, bdn):
    half = TOK_WORDS
    D = 2 * half
    tf = MOE_TF
    nj = D_FF // tf
    nv = visit_e.shape[0]
    blk_tiles = MOE_BLK * TOK_SUB
    assert nj % 2 == 0

    grid_spec = pltpu.PrefetchScalarGridSpec(
        num_scalar_prefetch=5,
        grid=(nv,),
        in_specs=[
            pl.BlockSpec((None, 2 * nj, tf), lambda v, ve, vf, nb, nu, tb: (ve[v], 0, 0)),
            pl.BlockSpec((None, 1, D), lambda v, ve, vf, nb, nu, tb: (ve[v], 0, 0)),
            pl.BlockSpec(memory_space=pl.ANY),
            pl.BlockSpec(memory_space=pl.ANY),
            pl.BlockSpec(memory_space=pl.ANY),
        ],
        out_specs=pl.BlockSpec(memory_space=pl.ANY),
        scratch_shapes=[pltpu.VMEM((MOE_R, D), BF16), pltpu.VMEM((MOE_R, D), F32),
                        pltpu.VMEM((MOE_RING, blk_tiles, LANES_V7X), U32),
                        pltpu.VMEM((MOE_RING, blk_tiles, LANES_V7X), U32),
                        pltpu.VMEM((2, D, tf), F32), pltpu.VMEM((2, D, tf), F32), pltpu.VMEM((2, tf, D), F32),
                        pltpu.VMEM((D, tf), BF16), pltpu.VMEM((D, tf), BF16), pltpu.VMEM((tf, D), BF16),
                        pltpu.SemaphoreType.DMA((MOE_RING,)), pltpu.SemaphoreType.DMA((MOE_RING,)),
                        pltpu.SemaphoreType.DMA((2, 3))],
    )
    vmem = (MOE_R * D * 2 + MOE_R * D * 4 + 2 * MOE_RING * blk_tiles * LANES_V7X * 4
            + 2 * 3 * D * tf * 4 + 3 * D * tf * 2 + 5 * MOE_BIG * MOE_BLK * tf * 4)
    return pl.pallas_call(
        _experts_kernel,
        out_shape=jax.ShapeDtypeStruct(xs.shape, U32),
        grid_spec=grid_spec,
        compiler_params=pltpu.CompilerParams(
            dimension_semantics=("arbitrary",), has_side_effects=True, vmem_limit_bytes=_vmem_limit(vmem)),
        name="moe_experts",
    )(visit_e, visit_first, visit_nb, n_used, n_blocks_used, bgu, bdn, xs, wgu, wdn)


def _combine_kernel(dcur_ref, dnext_ref, gate_ref, x_ref, y_ref, nw_ref, out_ref, buf, y_scr, sems):
    i = pl.program_id(0)
    n = pl.num_programs(0)
    tt = x_ref.shape[0]
    half = TOK_WORDS
    slot = lax.rem(i, 2)

    def row_copy(dref, s, t, kk):
        src = y_ref.at[pl.ds(pl.multiple_of(dref[0, t * TOP_K + kk], TOK_SUB), TOK_SUB)]
        dst = buf.at[s, kk, pl.ds(pl.multiple_of(t * TOK_SUB, TOK_SUB), TOK_SUB)]
        return pltpu.make_async_copy(src, dst, sems.at[s])

    def issue(dref, s):
        def body(t, carry):
            for kk in range(TOP_K):
                row_copy(dref, s, t, kk).start(priority=kk % DMA_PRIORITIES)
            return carry
        lax.fori_loop(0, tt, body, 0)

    @pl.when(i == 0)
    def _():
        issue(dcur_ref, 0)

    @pl.when(i + 1 < n)
    def _():
        issue(dnext_ref, 1 - slot)

    def wait(t, carry):
        for kk in range(TOP_K):
            row_copy(dcur_ref, slot, t, kk).wait()
        return carry
    lax.fori_loop(0, tt, wait, 0)

    gate = gate_ref[...]
    gks = [jnp.broadcast_to(gate[:, kk:kk + 1], (tt, LANES_V7X)) for kk in range(TOP_K)]
    for s in range(TOK_SUB):
        ca = pl.ds(s * LANES_V7X, LANES_V7X)
        cb = pl.ds(half + s * LANES_V7X, LANES_V7X)
        ya = x_ref[:, ca]
        yb = x_ref[:, cb]
        for kk in range(TOP_K):
            hi, lo = _unpack_bf16_pairs(_load_token_slab(buf.at[slot, kk], 0, tt, s))
            ya = ya + gks[kk] * hi
            yb = yb + gks[kk] * lo
        y_scr[:, ca] = ya
        y_scr[:, cb] = yb
    out_ref[...] = _rms(y_scr[...], nw_ref[...])


def _combine(dest3, gates, x2, yrows, final_w):
    T, D = x2.shape
    half = TOK_WORDS
    tt = min(MOE_TT, T)
    n = T // tt
    dspec = lambda f: pl.BlockSpec((None, 1, tt * TOP_K), lambda i: (f(i), 0, 0), memory_space=pltpu.SMEM)
    return pl.pallas_call(
        _combine_kernel,
        out_shape=jax.ShapeDtypeStruct((T, D), F32),
        grid=(n,),
        in_specs=[
            dspec(lambda i: i), dspec(lambda i: jnp.minimum(i + 1, n - 1)),
            pl.BlockSpec((tt, LANES_V7X), lambda i: (i, 0)),
            pl.BlockSpec((tt, D), lambda i: (i, 0)),
            pl.BlockSpec(memory_space=pl.ANY),
            pl.BlockSpec((1, D), lambda i: (0, 0)),
        ],
        out_specs=pl.BlockSpec((tt, D), lambda i: (i, 0)),
        scratch_shapes=[pltpu.VMEM((2, TOP_K, tt * TOK_SUB, LANES_V7X), U32), pltpu.VMEM((tt, D), F32),
                        pltpu.SemaphoreType.DMA((2,))],
        compiler_params=pltpu.CompilerParams(
            dimension_semantics=("arbitrary",),
            vmem_limit_bytes=_vmem_limit(2 * TOP_K * tt * half * 4 + 8 * tt * D * 4 + (8 << 20))),
        name="moe_combine",
    )(dest3, dest3, gates, x2, yrows, final_w)


def _route(top_e, n_blocks, n_vis):
    flat_e = top_e.reshape(-1)
    onehot = (flat_e[:, None] == jnp.arange(N_EXPERTS, dtype=I32)[None, :]).astype(I32)
    incl = jnp.cumsum(onehot, axis=0)
    counts = incl[-1]
    pos = jnp.sum((incl - onehot) * onehot, axis=1)
    nblk = (counts + MOE_BLK - 1) // MOE_BLK
    bend = jnp.cumsum(nblk)
    bstart = bend - nblk
    dest = jnp.sum(onehot * bstart[None, :], axis=1) * MOE_BLK + pos

    nvis = (nblk + MOE_G - 1) // MOE_G
    bpv = (nblk + jnp.maximum(nvis, 1) - 1) // jnp.maximum(nvis, 1)
    vend = jnp.cumsum(nvis)
    vstart = vend - nvis
    n_used = vend[-1]
    v = jnp.minimum(jnp.arange(n_vis, dtype=I32), n_used - 1)[:, None]
    sel = ((vstart[None, :] <= v) & (v < vend[None, :])).astype(I32)
    pick = lambda table: jnp.sum(sel * table[None, :], axis=1)
    visit_e = pick(jnp.arange(N_EXPERTS, dtype=I32))
    local = v[:, 0] - pick(vstart)
    visit_first = pick(bstart) + local * pick(bpv)
    visit_nb = jnp.clip(pick(nblk) - local * pick(bpv), 0, pick(bpv))
    visit_nb = jnp.where(jnp.arange(n_vis) < n_used, visit_nb, 0)

    blk = jnp.arange(n_blocks, dtype=I32)
    is_last = jnp.any((blk[:, None] == (bend - 1)[None, :]) & (nblk > 0)[None, :], axis=1)
    partial = (is_last | (blk >= bend[-1])).astype(I32)
    as_i32 = lambda a: a.astype(I32)
    return (as_i32(dest), as_i32(visit_e), as_i32(visit_first), as_i32(visit_nb),
            as_i32(n_used).reshape(1), as_i32(bend[-1]).reshape(1), partial)


def kernel(x, mem, norm_mix_w, w_in, conv_w, conv_b, if_bias, m_head_norm_w, w_branch_m, w_branch_a, rel_bias, w_out, norm_x_w, norm_mem_w, wq_x, wkv_x, wo_x, norm_ffn_w, w_router, b_router, w_gu, b_gu, w_down, b_down, final_norm_w):
    B, S, D = x.shape
    T = B * S
    l = 0
    x2d = x.reshape(T, D)

    wi = w_in[l]
    o_qm, o_km, o_vm, o_om = 0, M_WIDTH, 2 * M_WIDTH, 3 * M_WIDTH
    o_if = 4 * M_WIDTH
    o_qa = o_if + 2 * M_HEADS
    o_ka, o_va = o_qa + A_WIDTH, o_qa + 2 * A_WIDTH
    o_gm = o_qa + 3 * A_WIDTH
    o_ga = o_gm + D
    seg = lambda o, w: wi[:, o:o + w]
    w_main = jnp.concatenate(
        [seg(o_gm, D), seg(o_ga, D), seg(o_qm, M_WIDTH), seg(o_km, M_WIDTH), seg(o_vm, M_WIDTH),
         seg(o_om, M_WIDTH), seg(o_qa, A_WIDTH), seg(o_ka, A_WIDTH), seg(o_va, A_WIDTH)], axis=1).astype(BF16)
    w_if = jnp.pad(seg(o_if, 2 * M_HEADS), ((0, 0), (0, LANES_V7X - 2 * M_HEADS))).astype(BF16)

    proj, if_rows = _inproj(x2d, norm_mix_w[l].reshape(1, D), w_main, w_if)
    p3 = proj.reshape(B, S, N_MAIN)

    nc = S // CHUNK
    gcol3 = if_rows.reshape(B, S, LANES_V7X)
    grow4 = if_rows[:, :2 * M_HEADS].reshape(B, nc, CHUNK, 2 * M_HEADS).transpose(0, 1, 3, 2)
    bias_col = jnp.pad(if_bias[l], (0, LANES_V7X - 2 * M_HEADS)).reshape(1, LANES_V7X)
    bias_row = jnp.broadcast_to(if_bias[l][:, None], (2 * M_HEADS, CHUNK))
    hm = _mlstm(p3, gcol3, grow4, conv_w[l], conv_b[l].reshape(1, 2 * M_WIDTH), bias_col, bias_row,
                m_head_norm_w[l].reshape(1, M_WIDTH))

    ya = _band_attn(p3, _band_bias(rel_bias[l]))

    x1 = _merge(hm.reshape(T, M_WIDTH), ya.reshape(T, A_WIDTH), proj, x2d,
                w_branch_m[l].astype(BF16), w_branch_a[l].astype(BF16), w_out[l].astype(BF16))

    km, vm = _memkv(mem, norm_mem_w[l].reshape(1, D), wkv_x[l].astype(BF16))
    wr = jnp.pad(w_router[l], ((0, 0), (0, LANES_V7X - N_EXPERTS)))
    wr_hi = wr.astype(BF16)
    wr_lo = (wr - wr_hi.astype(F32)).astype(BF16)
    br = jnp.concatenate([b_router[l], jnp.full((LANES_V7X - N_EXPERTS,), NEG_BIG, F32)]).reshape(1, LANES_V7X)
    x2, h3p, te, tg = _xattn(x1, S, norm_x_w[l].reshape(1, D), wq_x[l].astype(BF16), km, vm,
                             wo_x[l].astype(BF16), norm_ffn_w[l].reshape(1, D), wr_hi, wr_lo, br)

    A = T * TOP_K
    n_blocks = A // MOE_BLK + N_EXPERTS
    n_vis = (n_blocks + N_EXPERTS * (MOE_G - 1)) // MOE_G
    dest, visit_e, visit_first, visit_nb, n_used, n_blocks_used, partial = _route(te[:, :TOP_K], n_blocks, n_vis)
    dest3 = (dest * TOK_SUB).reshape(T // min(MOE_TT, T), 1, -1)
    xs = _dispatch(partial, dest3, h3p)
    yrows = _experts(visit_e, visit_first, visit_nb, n_used, n_blocks_used, xs, w_gu[l],
                     b_gu[l].reshape(N_EXPERTS, 2 * D_FF // MOE_TF, MOE_TF),
                     w_down[l], b_down[l].reshape(N_EXPERTS, 1, D))
    out = _combine(dest3, tg, x2, yrows, final_norm_w.reshape(1, D))
    return out.reshape(B, S, D)
```

```python
import functools

import jax
import jax.numpy as jnp
from jax import lax
from jax.experimental import pallas as pl
from jax.experimental.pallas import tpu as pltpu

F32 = jnp.float32
BF16 = jnp.bfloat16
I32 = jnp.int32
U32 = jnp.uint32

D_MODEL = 2048
CHUNK = 64
EPS = 1e-6
M_HEADS = 4
M_HEAD_DIM = 256
M_WIDTH = M_HEADS * M_HEAD_DIM
CONV_WIDTH = 4
A_HEADS = 16
A_HEAD_DIM = 64
A_WIDTH = A_HEADS * A_HEAD_DIM
LEFT_CHUNKS = 8
MAX_REL = 128
X_HEADS = 4
X_HEAD_DIM = 128
X_WIDTH = X_HEADS * X_HEAD_DIM
N_EXPERTS = 32
TOP_K = 4
D_FF = D_MODEL
SWIGLU_LIMIT = 7.0
SWIGLU_ALPHA = 1.702

LANES_V7X = 128
SUBLANES_V7X = 8
VMEM_BYTES_V7X = 64 * 1024 * 1024
VMEM_LIMIT_CAP = 60000 * 1024
DMA_PRIORITIES = 2

NEG_BIG = -1e30

COL_GM = 0
COL_GA = 2048
COL_QM = 4096
COL_KM = 5120
COL_VM = 6144
COL_OM = 7168
COL_QA = 8192
COL_KA = 9216
COL_VA = 10240
N_MAIN = 11264

IN_TM = 1024
IN_TN = 1024
ATT_TQ = 1024
ATT_SUB = 128
ATT_WIN = ATT_SUB + LEFT_CHUNKS * CHUNK
MERGE_TM = 512
XATT_TM = 512
MOE_BLK = 256
MOE_G = 9
MOE_R = MOE_G * MOE_BLK
MOE_BIG = 8
MOE_MID = 4
MOE_RING = 4
MOE_TF = 256
MOE_TT = 256


def _vmem_limit(nbytes):
    return int(min(VMEM_LIMIT_CAP, max(nbytes, 16 * 1024 * 1024)))


def _nt_dot(a, b):
    return lax.dot_general(a, b, (((1,), (1,)), ((), ())), preferred_element_type=F32)


def _tn_dot(a, b):
    return lax.dot_general(a, b, (((0,), (0,)), ((), ())), preferred_element_type=F32)


def _sigmoid(x):
    return 1.0 / (1.0 + jnp.exp(-x))


def _log_sigmoid(x):
    return jnp.minimum(x, 0.0) - jnp.log1p(jnp.exp(-jnp.abs(x)))


def _rms(x, w):
    return x * lax.rsqrt(jnp.mean(x * x, axis=-1, keepdims=True) + EPS) * w


def _pack_bf16_pairs(xb):
    n = xb.shape[1] // 2
    hi = lax.bitcast_convert_type(xb[:, :n].astype(F32), U32)
    lo = lax.bitcast_convert_type(xb[:, n:].astype(F32), U32)
    return hi | (lo >> 16)


def _unpack_bf16_pairs(w):
    hi = lax.bitcast_convert_type(w & jnp.uint32(0xFFFF0000), F32)
    lo = lax.bitcast_convert_type(w << 16, F32)
    return hi, lo


TOK_SUB = SUBLANES_V7X
TOK_WORDS = TOK_SUB * LANES_V7X


def _store_token_tiles(ref, tok0, packed):
    n = packed.shape[0]
    for s in range(TOK_SUB):
        ref[pl.ds(tok0 * TOK_SUB + s, n, stride=TOK_SUB), :] = packed[:, s * LANES_V7X:(s + 1) * LANES_V7X]


def _load_token_slab(ref, tok0, n, s):
    return ref[pl.ds(tok0 * TOK_SUB + s, n, stride=TOK_SUB), :]


def _inproj_kernel(x_ref, nw_ref, w_ref, wif_ref, o_ref, if_ref, h_scr):
    @pl.when(pl.program_id(1) == 0)
    def _():
        hb = _rms(x_ref[...], nw_ref[...]).astype(BF16)
        h_scr[...] = hb
        if_ref[...] = jnp.dot(hb, wif_ref[...], preferred_element_type=F32)

    o_ref[...] = jnp.dot(h_scr[...], w_ref[...], preferred_element_type=F32).astype(o_ref.dtype)


def _inproj(x2d, norm_w, w_main, w_if):
    T, D = x2d.shape
    tm = min(IN_TM, T)
    grid = (T // tm, N_MAIN // IN_TN)
    vmem = 2 * tm * D * 4 + tm * D * 2 + 2 * D * IN_TN * 2 + 2 * tm * IN_TN * 2 + 4 * tm * LANES_V7X * 4
    return pl.pallas_call(
        _inproj_kernel,
        out_shape=(jax.ShapeDtypeStruct((T, N_MAIN), BF16), jax.ShapeDtypeStruct((T, LANES_V7X), F32)),
        grid=grid,
        in_specs=[
            pl.BlockSpec((tm, D), lambda i, j: (i, 0)),
            pl.BlockSpec((1, D), lambda i, j: (0, 0)),
            pl.BlockSpec((D, IN_TN), lambda i, j: (0, j)),
            pl.BlockSpec((D, LANES_V7X), lambda i, j: (0, 0)),
        ],
        out_specs=(
            pl.BlockSpec((tm, IN_TN), lambda i, j: (i, j)),
            pl.BlockSpec((tm, LANES_V7X), lambda i, j: (i, 0)),
        ),
        scratch_shapes=[pltpu.VMEM((tm, D), BF16)],
        compiler_params=pltpu.CompilerParams(
            dimension_semantics=("parallel", "arbitrary"),
            vmem_limit_bytes=_vmem_limit(vmem + (8 << 20))),
        name="inproj",
    )(x2d, norm_w, w_main, w_if)


def _mlstm_kernel(q_ref, k_ref, v_ref, o_ref, gc_ref, gr_ref, cw_ref, cb_ref, bc_ref, br_ref, nw_ref,
                  out_ref, tail, c_st, n_st, m_st):
    nb = q_ref.shape[0]
    L = CHUNK
    dh = M_HEAD_DIM
    halo = tail.shape[1]

    @pl.when(pl.program_id(0) == 0)
    def _():
        tail[...] = jnp.zeros_like(tail)
        c_st[...] = jnp.zeros_like(c_st)
        n_st[...] = jnp.zeros_like(n_st)
        m_st[...] = jnp.zeros_like(m_st)

    row = lax.broadcasted_iota(I32, (L, L), 0)
    col = lax.broadcasted_iota(I32, (L, L), 1)
    lower = col <= row

    srow = lax.broadcasted_iota(I32, (L, halo + L), 0)
    scol = lax.broadcasted_iota(I32, (L, halo + L), 1)
    shift_mat = jnp.concatenate(
        [jnp.where(scol == srow + (halo - (CONV_WIDTH - 1) + j), 1.0, 0.0) for j in range(CONV_WIDTH - 1)],
        axis=0).astype(BF16)

    erow = lax.broadcasted_iota(I32, (dh, dh), 0)
    ecol = lax.broadcasted_iota(I32, (dh, dh), 1)
    eye = jnp.where(erow == ecol, 1.0, 0.0).astype(BF16)

    streams = [(b, h) for b in range(nb) for h in range(M_HEADS)]

    qk = []
    for b in range(nb):
        halves = []
        for half, ref in enumerate((q_ref, k_ref)):
            cols = slice(half * M_WIDTH, (half + 1) * M_WIDTH)
            cur = ref[b]
            ext = jnp.concatenate([tail[b, :, cols], cur], axis=0)
            back = jnp.dot(shift_mat, ext, preferred_element_type=F32)
            conv = cb_ref[:, cols] + cur.astype(F32) * cw_ref[CONV_WIDTH - 1:CONV_WIDTH, cols]
            for j in range(CONV_WIDTH - 1):
                conv = conv + back[j * L:(j + 1) * L] * cw_ref[j:j + 1, cols]
            halves.append(conv * _sigmoid(conv))
            tail[b, :, cols] = cur[L - halo:L]
        qk.append(halves)

    st = []
    for b, h in streams:
        s = b * M_HEADS + h
        lo, hi = h * dh, (h + 1) * dh
        gcol = gc_ref[b] + bc_ref[...]
        grow = gr_ref[b, 0] + br_ref[...]
        q = qk[b][0][:, lo:hi]
        k = qk[b][1][:, lo:hi] * (dh ** -0.5)
        ig_c = gcol[:, h:h + 1]
        ig_r = grow[h:h + 1, :]
        ls_c = _log_sigmoid(gcol[:, M_HEADS + h:M_HEADS + h + 1])
        ls_r = _log_sigmoid(grow[M_HEADS + h:M_HEADS + h + 1, :])
        b_c = jnp.sum(jnp.where(lower, ls_r, 0.0), axis=1, keepdims=True)
        b_r = jnp.sum(jnp.where(row <= col, ls_c, 0.0), axis=0, keepdims=True)
        g = jnp.sum(ls_r, axis=1, keepdims=True)
        m_old = m_st[s][:, 0:1]
        w_end = g - b_c + ig_c
        a = jnp.max(w_end, axis=0, keepdims=True)
        dlog = jnp.where(lower, b_c - b_r + ig_r, NEG_BIG)
        inter = b_c + m_old
        m_j = jnp.maximum(inter, jnp.max(dlog, axis=1, keepdims=True))
        st.append(dict(s=s, lo=lo, hi=hi, b=b, q=q, k=k, qb=q.astype(BF16), kb=k.astype(BF16),
                       vb=v_ref[b, :, lo:hi], g=g, a=a, m_old=m_old, m_j=m_j,
                       e_end=jnp.exp(w_end - a), decay=jnp.exp(dlog - m_j), w_int=jnp.exp(inter - m_j),
                       c_old=c_st[s], n_old=n_st[s]))

    for d in st:
        d["sm"] = _nt_dot(d["qb"], d["kb"]) * d["decay"]
    for d in st:
        d["num"] = (jnp.dot(d["sm"].astype(BF16), d["vb"], preferred_element_type=F32)
                    + d["w_int"] * _nt_dot(d["qb"], d["c_old"].astype(BF16)))
    for d in st:
        den = (jnp.sum(d["sm"], axis=1, keepdims=True)
               + d["w_int"] * jnp.sum(d["q"] * d["n_old"], axis=1, keepdims=True))
        hh = d["num"] / jnp.maximum(jnp.abs(den), jnp.exp(-d["m_j"]))
        hn = _rms(hh, nw_ref[:, d["lo"]:d["hi"]])
        og = _sigmoid(o_ref[d["b"], :, d["lo"]:d["hi"]].astype(F32))
        out_ref[d["b"], :, d["lo"]:d["hi"]] = (og * hn).astype(out_ref.dtype)
    for d in st:
        s = d["s"]
        m_new = jnp.maximum(d["g"] + d["m_old"], d["a"])
        s_prev = jnp.exp(d["g"] + d["m_old"] - m_new)
        s_loc = jnp.exp(d["a"] - m_new)
        ve = (d["vb"].astype(F32) * (d["e_end"] * s_loc)).astype(BF16)
        ve_t = _nt_dot(eye, ve).astype(BF16)
        c_st[s] = s_prev * d["c_old"] + jnp.dot(ve_t, d["kb"], preferred_element_type=F32)
        n_st[s] = s_prev * d["n_old"] + s_loc * jnp.sum(d["e_end"] * d["k"], axis=0, keepdims=True)
        m_st[s] = jnp.broadcast_to(m_new, (1, LANES_V7X))


def _mlstm(p3, gcol3, grow4, conv_w, conv_b, bias_col, bias_row, head_w):
    B, S, _ = p3.shape
    nc = S // CHUNK
    ns = B * M_HEADS
    blk = lambda cb: pl.BlockSpec((B, CHUNK, M_WIDTH), lambda c, cb=cb: (0, c, cb))
    full = lambda shape: pl.BlockSpec(shape, lambda c: (0,) * len(shape))
    vmem = (2 * 5 * B * CHUNK * M_WIDTH * 2 + B * (CHUNK + 8) * 2 * M_WIDTH * 4
            + ns * M_HEAD_DIM * M_HEAD_DIM * 4 + (16 << 20))
    return pl.pallas_call(
        _mlstm_kernel,
        out_shape=jax.ShapeDtypeStruct((B, S, M_WIDTH), BF16),
        grid=(nc,),
        in_specs=[
            blk(COL_QM // M_WIDTH), blk(COL_KM // M_WIDTH), blk(COL_VM // M_WIDTH), blk(COL_OM // M_WIDTH),
            pl.BlockSpec((B, CHUNK, LANES_V7X), lambda c: (0, c, 0)),
            pl.BlockSpec((B, 1, 2 * M_HEADS, CHUNK), lambda c: (0, c, 0, 0)),
            full((CONV_WIDTH, 2 * M_WIDTH)), full((1, 2 * M_WIDTH)),
            full((1, LANES_V7X)), full((2 * M_HEADS, CHUNK)), full((1, M_WIDTH)),
        ],
        out_specs=pl.BlockSpec((B, CHUNK, M_WIDTH), lambda c: (0, c, 0)),
        scratch_shapes=[
            pltpu.VMEM((B, 2 * SUBLANES_V7X, 2 * M_WIDTH), BF16),
            pltpu.VMEM((ns, M_HEAD_DIM, M_HEAD_DIM), F32),
            pltpu.VMEM((ns, 1, M_HEAD_DIM), F32),
            pltpu.VMEM((ns, 1, LANES_V7X), F32),
        ],
        compiler_params=pltpu.CompilerParams(
            dimension_semantics=("arbitrary",), vmem_limit_bytes=_vmem_limit(vmem)),
        name="mlstm",
    )(p3, p3, p3, p3, gcol3, grow4, conv_w, conv_b, bias_col, bias_row, head_w)


def _band_attn_kernel(q_ref, kp_ref, kc_ref, vp_ref, vc_ref, bias_ref, out_ref):
    qb_idx = pl.program_id(2)
    tq = q_ref.shape[1]
    q2 = q_ref[0]
    kcat = jnp.concatenate([kp_ref[0], kc_ref[0]], axis=0)
    vcat = jnp.concatenate([vp_ref[0], vc_ref[0]], axis=0)
    lane = lax.broadcasted_iota(I32, (1, LANES_V7X), 1)
    wcol = lax.broadcasted_iota(I32, (1, ATT_WIN), 1)
    first_pen = jnp.where(qb_idx == 0, NEG_BIG, 0.0)
    scale = A_HEAD_DIM ** -0.5
    bias = bias_ref[0]

    subs = []
    for t in range(tq // ATT_SUB):
        r0 = t * ATT_SUB
        w0 = tq + r0 - LEFT_CHUNKS * CHUNK
        pen = jnp.where(wcol < (tq - w0), first_pen, 0.0)
        qs = q2[r0:r0 + ATT_SUB] * scale
        zero = jnp.zeros_like(qs)
        qst = jnp.concatenate([jnp.where(lane < A_HEAD_DIM, qs, zero),
                               jnp.where(lane >= A_HEAD_DIM, qs, zero)], axis=0)
        subs.append(dict(r0=r0, w0=w0, pen=pen, qst=qst))
    for d in subs:
        d["sc"] = _nt_dot(d["qst"], kcat[d["w0"]:d["w0"] + ATT_WIN]) + bias + d["pen"]
    for d in subs:
        p = jnp.exp(d["sc"] - jnp.max(d["sc"], axis=1, keepdims=True))
        d["den"] = jnp.sum(p, axis=1, keepdims=True)
        d["p"] = p.astype(BF16)
    for d in subs:
        d["o"] = jnp.dot(d["p"], vcat[d["w0"]:d["w0"] + ATT_WIN], preferred_element_type=F32) / d["den"]
    for d in subs:
        o, r0 = d["o"], d["r0"]
        out_ref[0, r0:r0 + ATT_SUB, :] = jnp.where(
            lane < A_HEAD_DIM, o[:ATT_SUB], o[ATT_SUB:]).astype(out_ref.dtype)


def _band_attn(p3, bias):
    B, S, _ = p3.shape
    tq = ATT_TQ
    nq = S // tq
    nhp = A_HEADS // 2
    cq, ck, cv = COL_QA // LANES_V7X, COL_KA // LANES_V7X, COL_VA // LANES_V7X
    cur = lambda c0: pl.BlockSpec((1, tq, LANES_V7X), lambda b, hp, i, c0=c0: (b, i, c0 + hp))
    prev = lambda c0: pl.BlockSpec((1, tq, LANES_V7X), lambda b, hp, i, c0=c0: (b, jnp.maximum(i - 1, 0), c0 + hp))
    return pl.pallas_call(
        _band_attn_kernel,
        out_shape=jax.ShapeDtypeStruct((B, S, A_WIDTH), BF16),
        grid=(B, nhp, nq),
        in_specs=[cur(cq), prev(ck), cur(ck), prev(cv), cur(cv),
                  pl.BlockSpec((1, 2 * ATT_SUB, ATT_WIN), lambda b, hp, i: (hp, 0, 0))],
        out_specs=pl.BlockSpec((1, tq, LANES_V7X), lambda b, hp, i: (b, i, hp)),
        compiler_params=pltpu.CompilerParams(
            dimension_semantics=("parallel", "parallel", "arbitrary"),
            vmem_limit_bytes=_vmem_limit(32 << 20)),
        name="band_attn",
    )(p3, p3, p3, p3, p3, bias)


def _band_bias(rel_table):
    n = 6 * LANES_V7X
    c = jnp.arange(n)
    d = jnp.where(c < ATT_WIN, c, c - n)
    rel = jnp.clip(LEFT_CHUNKS * CHUNK - d, -MAX_REL, MAX_REL) + MAX_REL
    vec = rel_table[:, rel].astype(F32)
    sheared = jnp.tile(vec, (1, ATT_SUB))[:, :ATT_SUB * (n - 1)].reshape(A_HEADS, ATT_SUB, n - 1)
    i = jnp.arange(ATT_SUB)[:, None]
    kp = jnp.arange(ATT_WIN)[None, :]
    ci, kc = i // CHUNK, kp // CHUNK
    valid = (kc >= ci) & (kc <= ci + LEFT_CHUNKS)
    bias = jnp.where(valid[None], sheared[:, :, :ATT_WIN], NEG_BIG)
    return bias.reshape(A_HEADS // 2, 2 * ATT_SUB, ATT_WIN)


def _merge_kernel(hm_ref, ya_ref, gm_ref, ga_ref, x_ref, wbm_ref, wba_ref, wo_ref, out_ref):
    ym = jnp.dot(hm_ref[...], wbm_ref[...], preferred_element_type=F32)
    ya = jnp.dot(ya_ref[...], wba_ref[...], preferred_element_type=F32)
    merged = _sigmoid(gm_ref[...].astype(F32)) * ym + _sigmoid(ga_ref[...].astype(F32)) * ya
    out_ref[...] = x_ref[...] + jnp.dot(merged.astype(BF16), wo_ref[...], preferred_element_type=F32)


def _const_spec(shape):
    return pl.BlockSpec(shape, lambda i: (0,) * len(shape), pipeline_mode=pl.Buffered(1))


def _merge(hm2d, ya2d, p2d, x2d, wbm, wba, wout):
    T, D = x2d.shape
    tm = min(MERGE_TM, T)
    vmem = ((2 * M_WIDTH * D + D * D) * 2 + 2 * (2 * tm * M_WIDTH * 2 + 2 * tm * D * 2 + 2 * tm * D * 4)
            + 6 * tm * D * 4)
    return pl.pallas_call(
        _merge_kernel,
        out_shape=jax.ShapeDtypeStruct((T, D), F32),
        grid=(T // tm,),
        in_specs=[
            pl.BlockSpec((tm, M_WIDTH), lambda i: (i, 0)),
            pl.BlockSpec((tm, A_WIDTH), lambda i: (i, 0)),
            pl.BlockSpec((tm, D), lambda i: (i, COL_GM // D)),
            pl.BlockSpec((tm, D), lambda i: (i, COL_GA // D)),
            pl.BlockSpec((tm, D), lambda i: (i, 0)),
            _const_spec((M_WIDTH, D)), _const_spec((A_WIDTH, D)), _const_spec((D, D)),
        ],
        out_specs=pl.BlockSpec((tm, D), lambda i: (i, 0)),
        compiler_params=pltpu.CompilerParams(
            dimension_semantics=("parallel",), vmem_limit_bytes=_vmem_limit(vmem)),
        name="merge_out",
    )(hm2d, ya2d, p2d, p2d, x2d, wbm, wba, wout)


def _memkv_kernel(mem_ref, nw_ref, w_ref, k_ref, v_ref):
    hb = _rms(mem_ref[0], nw_ref[...]).astype(BF16)
    kv = jnp.dot(hb, w_ref[...], preferred_element_type=F32)
    k_ref[0] = kv[:, :X_WIDTH].astype(k_ref.dtype)
    v_ref[0] = kv[:, X_WIDTH:].astype(v_ref.dtype)


def _memkv(mem, norm_w, wkv):
    B, N, D = mem.shape
    return pl.pallas_call(
        _memkv_kernel,
        out_shape=(jax.ShapeDtypeStruct((B, N, X_WIDTH), BF16), jax.ShapeDtypeStruct((B, N, X_WIDTH), BF16)),
        grid=(B,),
        in_specs=[pl.BlockSpec((1, N, D), lambda b: (b, 0, 0)),
                  pl.BlockSpec((1, D), lambda b: (0, 0)),
                  pl.BlockSpec((D, 2 * X_WIDTH), lambda b: (0, 0))],
        out_specs=(pl.BlockSpec((1, N, X_WIDTH), lambda b: (b, 0, 0)),
                   pl.BlockSpec((1, N, X_WIDTH), lambda b: (b, 0, 0))),
        compiler_params=pltpu.CompilerParams(
            dimension_semantics=("parallel",), vmem_limit_bytes=_vmem_limit(24 << 20)),
        name="mem_kv",
    )(mem, norm_w, wkv)


def _xattn_kernel(x_ref, nxw_ref, wq_ref, km_ref, vm_ref, wo_ref, nfw_ref, wrh_ref, wrl_ref, br_ref,
                  x2_ref, h3_ref, te_ref, tg_ref):
    x1 = x_ref[...]
    hb = _rms(x1, nxw_ref[...]).astype(BF16)
    q = jnp.dot(hb, wq_ref[...], preferred_element_type=F32).astype(BF16)
    km = km_ref[0]
    vm = vm_ref[0]
    outs = []
    for h in range(X_HEADS):
        lo, hi = h * X_HEAD_DIM, (h + 1) * X_HEAD_DIM
        sc = _nt_dot(q[:, lo:hi], km[:, lo:hi]) * (X_HEAD_DIM ** -0.5)
        mx = jnp.max(sc, axis=1, keepdims=True)
        p = jnp.exp(sc - mx)
        den = jnp.sum(p, axis=1, keepdims=True)
        outs.append(jnp.dot(p.astype(BF16), vm[:, lo:hi], preferred_element_type=F32) / den)
    o = jnp.concatenate(outs, axis=1).astype(BF16)
    x2 = x1 + jnp.dot(o, wo_ref[...], preferred_element_type=F32)
    x2_ref[...] = x2

    h3 = _rms(x2, nfw_ref[...])
    h_hi = h3.astype(BF16)
    _store_token_tiles(h3_ref, 0, _pack_bf16_pairs(h_hi))
    h_lo = (h3 - h_hi.astype(F32)).astype(BF16)
    logits = (jnp.dot(h_hi, wrh_ref[...], preferred_element_type=F32)
              + jnp.dot(h_lo, wrh_ref[...], preferred_element_type=F32)
              + jnp.dot(h_hi, wrl_ref[...], preferred_element_type=F32)
              + br_ref[...])

    colf = lax.broadcasted_iota(I32, logits.shape, 1).astype(F32)
    work = logits
    vals, idxs = [], []
    for _ in range(TOP_K):
        mx = jnp.max(work, axis=1, keepdims=True)
        ix = jnp.min(jnp.where(work == mx, colf, float(LANES_V7X)), axis=1, keepdims=True)
        vals.append(mx)
        idxs.append(ix)
        work = jnp.where(colf == ix, -jnp.inf, work)
    es = [jnp.exp(v - vals[0]) for v in vals]
    tot = es[0] + es[1] + es[2] + es[3]
    te = jnp.zeros_like(logits)
    tg = jnp.zeros_like(logits)
    for kk in range(TOP_K):
        te = jnp.where(colf == float(kk), idxs[kk], te)
        tg = jnp.where(colf == float(kk), es[kk] / tot, tg)
    te_ref[...] = te.astype(I32)
    tg_ref[...] = tg


def _xattn(x1, batch_rows, norm_x_w, wq, km, vm, wo, norm_ffn_w, wr_hi, wr_lo, br):
    T, D = x1.shape
    tm = min(XATT_TM, batch_rows)
    per_b = batch_rows // tm
    n_mem = km.shape[1]
    row = lambda w: pl.BlockSpec((tm, w), lambda i: (i, 0))
    vmem = (2 * D * X_WIDTH * 2 + 2 * D * LANES_V7X * 2 + 4 * n_mem * X_WIDTH * 2
            + 2 * 3 * tm * D * 4 + 8 * tm * D * 4)
    return pl.pallas_call(
        _xattn_kernel,
        out_shape=(jax.ShapeDtypeStruct((T, D), F32), jax.ShapeDtypeStruct((T * TOK_SUB, LANES_V7X), U32),
                   jax.ShapeDtypeStruct((T, LANES_V7X), I32), jax.ShapeDtypeStruct((T, LANES_V7X), F32)),
        grid=(T // tm,),
        in_specs=[
            row(D), _const_spec((1, D)), _const_spec((D, X_WIDTH)),
            pl.BlockSpec((1, n_mem, X_WIDTH), lambda i: (i // per_b, 0, 0)),
            pl.BlockSpec((1, n_mem, X_WIDTH), lambda i: (i // per_b, 0, 0)),
            _const_spec((X_WIDTH, D)), _const_spec((1, D)),
            _const_spec((D, LANES_V7X)), _const_spec((D, LANES_V7X)), _const_spec((1, LANES_V7X)),
        ],
        out_specs=(row(D), pl.BlockSpec((tm * TOK_SUB, LANES_V7X), lambda i: (i, 0)),
                   row(LANES_V7X), row(LANES_V7X)),
        compiler_params=pltpu.CompilerParams(
            dimension_semantics=("parallel",), vmem_limit_bytes=_vmem_limit(vmem)),
        name="xattn_router",
    )(x1, norm_x_w, wq, km, vm, wo, norm_ffn_w, wr_hi, wr_lo, br)


def _dispatch_kernel(partial_ref, dest_ref, h_ref, xs_ref, zero_scr, sem, zsem):
    tt = h_ref.shape[0] // TOK_SUB
    nblocks = partial_ref.shape[0]

    @pl.when(pl.program_id(0) == 0)
    def _():
        zero_scr[...] = jnp.zeros_like(zero_scr)

        def zero_copy(b):
            row0 = pl.multiple_of(b * (MOE_BLK * TOK_SUB), MOE_BLK * TOK_SUB)
            return pltpu.make_async_copy(zero_scr, xs_ref.at[pl.ds(row0, MOE_BLK * TOK_SUB)], zsem)

        def zstart(b, carry):
            @pl.when(partial_ref[b] > 0)
            def _():
                zero_copy(b).start()
            return carry

        def zwait(b, carry):
            @pl.when(partial_ref[b] > 0)
            def _():
                zero_copy(b).wait()
            return carry

        lax.fori_loop(0, nblocks, zstart, 0)
        lax.fori_loop(0, nblocks, zwait, 0)

    def tile_copy(t, kk):
        src = h_ref.at[pl.ds(pl.multiple_of(t * TOK_SUB, TOK_SUB), TOK_SUB)]
        dst = xs_ref.at[pl.ds(pl.multiple_of(dest_ref[0, t * TOP_K + kk], TOK_SUB), TOK_SUB)]
        return pltpu.make_async_copy(src, dst, sem)

    def start(t, carry):
        for kk in range(TOP_K):
            tile_copy(t, kk).start(priority=kk % DMA_PRIORITIES)
        return carry

    def wait(t, carry):
        for kk in range(TOP_K):
            tile_copy(t, kk).wait()
        return carry

    lax.fori_loop(0, tt, start, 0)
    lax.fori_loop(0, tt, wait, 0)


def _dispatch(partial, dest3, h3p):
    T = h3p.shape[0] // TOK_SUB
    n_rows = partial.shape[0] * MOE_BLK
    tt = min(MOE_TT, T)
    grid_spec = pltpu.PrefetchScalarGridSpec(
        num_scalar_prefetch=1,
        grid=(T // tt,),
        in_specs=[
            pl.BlockSpec((None, 1, tt * TOP_K), lambda i, nb: (i, 0, 0), memory_space=pltpu.SMEM),
            pl.BlockSpec((tt * TOK_SUB, LANES_V7X), lambda i, nb: (i, 0)),
        ],
        out_specs=pl.BlockSpec(memory_space=pl.ANY),
        scratch_shapes=[pltpu.VMEM((MOE_BLK * TOK_SUB, LANES_V7X), U32),
                        pltpu.SemaphoreType.DMA(()), pltpu.SemaphoreType.DMA(())],
    )
    return pl.pallas_call(
        _dispatch_kernel,
        out_shape=jax.ShapeDtypeStruct((n_rows * TOK_SUB, LANES_V7X), U32),
        grid_spec=grid_spec,
        compiler_params=pltpu.CompilerParams(
            dimension_semantics=("arbitrary",), has_side_effects=True,
            vmem_limit_bytes=_vmem_limit(4 * tt * TOK_WORDS * 4 + MOE_BLK * TOK_WORDS * 4 + (8 << 20))),
        name="moe_dispatch",
    )(partial, dest3, h3p)


def _experts_kernel(ve_ref, vf_ref, nb_ref, nu_ref, tb_ref, bgu_ref, bd_ref, xs_hbm, wgu_hbm, wdn_hbm, out_hbm,
                    xb_scr, acc_scr, x_buf, o_buf, wg_buf, wu_buf, wd_buf, wgb_scr, wub_scr, wdb_scr,
                    xsems, osems, sems):
    v = pl.program_id(0)
    nb = nb_ref[v]
    first = vf_ref[v]
    n_used = nu_ref[0]
    half = TOK_WORDS
    tf = wg_buf.shape[2]
    nj = D_FF // tf
    blk_tiles = MOE_BLK * TOK_SUB
    n_blocks = out_hbm.shape[0] // blk_tiles
    ring = x_buf.shape[0]

    def block_rows(b):
        return pl.ds(pl.multiple_of(b * MOE_BLK, MOE_BLK), MOE_BLK)

    def hbm_block(ref, blk):
        return ref.at[pl.ds(pl.multiple_of(blk * blk_tiles, blk_tiles), blk_tiles)]

    def x_copy(blk, slot):
        return pltpu.make_async_copy(hbm_block(xs_hbm, blk), x_buf.at[slot], xsems.at[slot])

    def o_copy(blk, slot):
        return pltpu.make_async_copy(o_buf.at[slot], hbm_block(out_hbm, blk), osems.at[slot])

    def tile_copies(e, j, slot):
        cols = pl.ds(pl.multiple_of(j * tf, tf), tf)
        ucols = pl.ds(pl.multiple_of(D_FF + j * tf, tf), tf)
        return (pltpu.make_async_copy(wgu_hbm.at[e, :, cols], wg_buf.at[slot], sems.at[slot, 0]),
                pltpu.make_async_copy(wgu_hbm.at[e, :, ucols], wu_buf.at[slot], sems.at[slot, 1]),
                pltpu.make_async_copy(wdn_hbm.at[e, cols, :], wd_buf.at[slot], sems.at[slot, 2]))

    def fetch(e, j, slot):
        for c in tile_copies(e, j, slot):
            c.start()

    @pl.when(nb > 0)
    def _():
        e = ve_ref[v]

        def prefetch_rows(first_blk, n_blk):
            for b in range(ring):
                @pl.when(b < n_blk)
                def _():
                    x_copy(first_blk + b, b).start()

        @pl.when(v == 0)
        def _():
            fetch(e, 0, 0)
            prefetch_rows(first, nb)

        def stage(b, carry):
            slot = lax.rem(b, ring)
            x_copy(first + b, slot).wait()
            rows = block_rows(b)
            for s in range(TOK_SUB):
                hi, lo = _unpack_bf16_pairs(_load_token_slab(x_buf.at[slot], 0, MOE_BLK, s))
                xb_scr[rows, s * LANES_V7X:(s + 1) * LANES_V7X] = hi.astype(BF16)
                xb_scr[rows, half + s * LANES_V7X:half + (s + 1) * LANES_V7X] = lo.astype(BF16)
            acc_scr[rows, :] = jnp.zeros((MOE_BLK, acc_scr.shape[1]), F32)

            @pl.when(b + ring < nb)
            def _():
                x_copy(first + b + ring, slot).start()
            return carry
        lax.fori_loop(0, nb, stage, 0)

        big = nb >= MOE_BIG
        mid = jnp.logical_and(nb >= MOE_MID, jnp.logical_not(big))

        def tile_step(j, carry):
            slot = lax.rem(j, 2)

            @pl.when(j + 1 < nj)
            def _():
                fetch(e, j + 1, 1 - slot)

            @pl.when(jnp.logical_and(j + 1 == nj, v + 1 < n_used))
            def _():
                fetch(ve_ref[v + 1], 0, 1 - slot)
                prefetch_rows(vf_ref[v + 1], nb_ref[v + 1])

            for c in tile_copies(e, j, slot):
                c.wait()

            def ffn_rows(rows, wg, wu, wd):
                xb = xb_scr[rows, :]
                g = jnp.dot(xb, wg, preferred_element_type=F32) + bgu_ref[pl.ds(j, 1), :]
                u = jnp.dot(xb, wu, preferred_element_type=F32) + bgu_ref[pl.ds(nj + j, 1), :]
                g = jnp.minimum(g, SWIGLU_LIMIT)
                u = jnp.clip(u, -SWIGLU_LIMIT, SWIGLU_LIMIT)
                act = (u + 1.0) * (g * _sigmoid(SWIGLU_ALPHA * g))
                acc_scr[rows, :] += jnp.dot(act.astype(BF16), wd, preferred_element_type=F32)

            def cast_weights():
                wg = wg_buf[slot].astype(BF16)
                wu = wu_buf[slot].astype(BF16)
                wd = wd_buf[slot].astype(BF16)
                wgb_scr[...] = wg
                wub_scr[...] = wu
                wdb_scr[...] = wd
                return wg, wu, wd

            @pl.when(big)
            def _():
                ffn_rows(pl.ds(0, MOE_BIG * MOE_BLK), *cast_weights())

            @pl.when(mid)
            def _():
                ffn_rows(pl.ds(0, MOE_MID * MOE_BLK), *cast_weights())

            @pl.when(nb < MOE_MID)
            def _():
                cast_weights()

            def block(b, c2):
                ffn_rows(block_rows(b), wgb_scr[...], wub_scr[...], wdb_scr[...])
                return c2
            lax.fori_loop(jnp.where(big, MOE_BIG, jnp.where(mid, MOE_MID, 0)), nb, block, 0)
            return carry
        lax.fori_loop(0, nj, tile_step, 0)

        def finish(b, carry):
            slot = lax.rem(b, ring)

            @pl.when(b >= ring)
            def _():
                o_copy(first + b - ring, slot).wait()

            y = (acc_scr[block_rows(b), :] + bd_ref[...]).astype(BF16)
            _store_token_tiles(o_buf.at[slot], 0, _pack_bf16_pairs(y))
            o_copy(first + b, slot).start()
            return carry
        lax.fori_loop(0, nb, finish, 0)

        for back in range(ring, 0, -1):
            @pl.when(nb >= back)
            def _():
                o_copy(first + nb - back, lax.rem(nb - back, ring)).wait()

    @pl.when(v == pl.num_programs(0) - 1)
    def _():
        o_buf[0] = jnp.zeros(o_buf.shape[1:], U32)

        def fill(b, carry):
            c = o_copy(b, 0)
            c.start()
            c.wait()
            return carry
        lax.fori_loop(tb_ref[0], n_blocks, fill, 0)


def _experts(visit_e, visit_first, visit_nb, n_used, n_blocks_used, xs, wgu, bgu, wdn, bdn):
    half = TOK_WORDS
    D = 2 * half
    tf = MOE_TF
    nj = D_FF // tf
    nv = visit_e.shape[0]
    blk_tiles = MOE_BLK * TOK_SUB
    assert nj % 2 == 0

    grid_spec = pltpu.PrefetchScalarGridSpec(
        num_scalar_prefetch=5,
        grid=(nv,),
        in_specs=[
            pl.BlockSpec((None, 2 * nj, tf), lambda v, ve, vf, nb, nu, tb: (ve[v], 0, 0)),
            pl.BlockSpec((None, 1, D), lambda v, ve, vf, nb, nu, tb: (ve[v], 0, 0)),
            pl.BlockSpec(memory_space=pl.ANY),
            pl.BlockSpec(memory_space=pl.ANY),
            pl.BlockSpec(memory_space=pl.ANY),
        ],
        out_specs=pl.BlockSpec(memory_space=pl.ANY),
        scratch_shapes=[pltpu.VMEM((MOE_R, D), BF16), pltpu.VMEM((MOE_R, D), F32),
                        pltpu.VMEM((MOE_RING, blk_tiles, LANES_V7X), U32),
                        pltpu.VMEM((MOE_RING, blk_tiles, LANES_V7X), U32),
                        pltpu.VMEM((2, D, tf), F32), pltpu.VMEM((2, D, tf), F32), pltpu.VMEM((2, tf, D), F32),
                        pltpu.VMEM((D, tf), BF16), pltpu.VMEM((D, tf), BF16), pltpu.VMEM((tf, D), BF16),
                        pltpu.SemaphoreType.DMA((MOE_RING,)), pltpu.SemaphoreType.DMA((MOE_RING,)),
                        pltpu.SemaphoreType.DMA((2, 3))],
    )
    vmem = (MOE_R * D * 2 + MOE_R * D * 4 + 2 * MOE_RING * blk_tiles * LANES_V7X * 4
            + 2 * 3 * D * tf * 4 + 3 * D * tf * 2 + 5 * MOE_BIG * MOE_BLK * tf * 4)
    return pl.pallas_call(
        _experts_kernel,
        out_shape=jax.ShapeDtypeStruct(xs.shape, U32),
        grid_spec=grid_spec,
        compiler_params=pltpu.CompilerParams(
            dimension_semantics=("arbitrary",), has_side_effects=True, vmem_limit_bytes=_vmem_limit(vmem)),
        name="moe_experts",
    )(visit_e, visit_first, visit_nb, n_used, n_blocks_used, bgu, bdn, xs, wgu, wdn)


def _combine_kernel(dcur_ref, dnext_ref, gate_ref, x_ref, y_ref, nw_ref, out_ref, buf, y_scr, sems):
    i = pl.program_id(0)
    n = pl.num_programs(0)
    tt = x_ref.shape[0]
    half = TOK_WORDS
    slot = lax.rem(i, 2)

    def row_copy(dref, s, t, kk):
        src = y_ref.at[pl.ds(pl.multiple_of(dref[0, t * TOP_K + kk], TOK_SUB), TOK_SUB)]
        dst = buf.at[s, kk, pl.ds(pl.multiple_of(t * TOK_SUB, TOK_SUB), TOK_SUB)]
        return pltpu.make_async_copy(src, dst, sems.at[s])

    def issue(dref, s):
        def body(t, carry):
            for kk in range(TOP_K):
                row_copy(dref, s, t, kk).start(priority=kk % DMA_PRIORITIES)
            return carry
        lax.fori_loop(0, tt, body, 0)

    @pl.when(i == 0)
    def _():
        issue(dcur_ref, 0)

    @pl.when(i + 1 < n)
    def _():
        issue(dnext_ref, 1 - slot)

    def wait(t, carry):
        for kk in range(TOP_K):
            row_copy(dcur_ref, slot, t, kk).wait()
        return carry
    lax.fori_loop(0, tt, wait, 0)

    gate = gate_ref[...]
    gks = [jnp.broadcast_to(gate[:, kk:kk + 1], (tt, LANES_V7X)) for kk in range(TOP_K)]
    for s in range(TOK_SUB):
        ca = pl.ds(s * LANES_V7X, LANES_V7X)
        cb = pl.ds(half + s * LANES_V7X, LANES_V7X)
        ya = x_ref[:, ca]
        yb = x_ref[:, cb]
        for kk in range(TOP_K):
            hi, lo = _unpack_bf16_pairs(_load_token_slab(buf.at[slot, kk], 0, tt, s))
            ya = ya + gks[kk] * hi
            yb = yb + gks[kk] * lo
        y_scr[:, ca] = ya
        y_scr[:, cb] = yb
    out_ref[...] = _rms(y_scr[...], nw_ref[...])


def _combine(dest3, gates, x2, yrows, final_w):
    T, D = x2.shape
    half = TOK_WORDS
    tt = min(MOE_TT, T)
    n = T // tt
    dspec = lambda f: pl.BlockSpec((None, 1, tt * TOP_K), lambda i: (f(i), 0, 0), memory_space=pltpu.SMEM)
    return pl.pallas_call(
        _combine_kernel,
        out_shape=jax.ShapeDtypeStruct((T, D), F32),
        grid=(n,),
        in_specs=[
            dspec(lambda i: i), dspec(lambda i: jnp.minimum(i + 1, n - 1)),
            pl.BlockSpec((tt, LANES_V7X), lambda i: (i, 0)),
            pl.BlockSpec((tt, D), lambda i: (i, 0)),
            pl.BlockSpec(memory_space=pl.ANY),
            pl.BlockSpec((1, D), lambda i: (0, 0)),
        ],
        out_specs=pl.BlockSpec((tt, D), lambda i: (i, 0)),
        scratch_shapes=[pltpu.VMEM((2, TOP_K, tt * TOK_SUB, LANES_V7X), U32), pltpu.VMEM((tt, D), F32),
                        pltpu.SemaphoreType.DMA((2,))],
        compiler_params=pltpu.CompilerParams(
            dimension_semantics=("arbitrary",),
            vmem_limit_bytes=_vmem_limit(2 * TOP_K * tt * half * 4 + 8 * tt * D * 4 + (8 << 20))),
        name="moe_combine",
    )(dest3, dest3, gates, x2, yrows, final_w)


def _route(top_e, n_blocks, n_vis):
    flat_e = top_e.reshape(-1)
    onehot = (flat_e[:, None] == jnp.arange(N_EXPERTS, dtype=I32)[None, :]).astype(I32)
    incl = jnp.cumsum(onehot, axis=0)
    counts = incl[-1]
    pos = jnp.sum((incl - onehot) * onehot, axis=1)
    nblk = (counts + MOE_BLK - 1) // MOE_BLK
    bend = jnp.cumsum(nblk)
    bstart = bend - nblk
    dest = jnp.sum(onehot * bstart[None, :], axis=1) * MOE_BLK + pos

    nvis = (nblk + MOE_G - 1) // MOE_G
    bpv = (nblk + jnp.maximum(nvis, 1) - 1) // jnp.maximum(nvis, 1)
    vend = jnp.cumsum(nvis)
    vstart = vend - nvis
    n_used = vend[-1]
    v = jnp.minimum(jnp.arange(n_vis, dtype=I32), n_used - 1)[:, None]
    sel = ((vstart[None, :] <= v) & (v < vend[None, :])).astype(I32)
    pick = lambda table: jnp.sum(sel * table[None, :], axis=1)
    visit_e = pick(jnp.arange(N_EXPERTS, dtype=I32))
    local = v[:, 0] - pick(vstart)
    visit_first = pick(bstart) + local * pick(bpv)
    visit_nb = jnp.clip(pick(nblk) - local * pick(bpv), 0, pick(bpv))
    visit_nb = jnp.where(jnp.arange(n_vis) < n_used, visit_nb, 0)

    blk = jnp.arange(n_blocks, dtype=I32)
    is_last = jnp.any((blk[:, None] == (bend - 1)[None, :]) & (nblk > 0)[None, :], axis=1)
    partial = (is_last | (blk >= bend[-1])).astype(I32)
    as_i32 = lambda a: a.astype(I32)
    return (as_i32(dest), as_i32(visit_e), as_i32(visit_first), as_i32(visit_nb),
            as_i32(n_used).reshape(1), as_i32(bend[-1]).reshape(1), partial)


def kernel(x, mem, norm_mix_w, w_in, conv_w, conv_b, if_bias, m_head_norm_w, w_branch_m, w_branch_a, rel_bias, w_out, norm_x_w, norm_mem_w, wq_x, wkv_x, wo_x, norm_ffn_w, w_router, b_router, w_gu, b_gu, w_down, b_down, final_norm_w):
    B, S, D = x.shape
    T = B * S
    l = 0
    x2d = x.reshape(T, D)

    wi = w_in[l]
    o_qm, o_km, o_vm, o_om = 0, M_WIDTH, 2 * M_WIDTH, 3 * M_WIDTH
    o_if = 4 * M_WIDTH
    o_qa = o_if + 2 * M_HEADS
    o_ka, o_va = o_qa + A_WIDTH, o_qa + 2 * A_WIDTH
    o_gm = o_qa + 3 * A_WIDTH
    o_ga = o_gm + D
    seg = lambda o, w: wi[:, o:o + w].astype(BF16)
    w_main = jnp.concatenate(
        [seg(o_gm, D), seg(o_ga, D), seg(o_qm, M_WIDTH), seg(o_km, M_WIDTH), seg(o_vm, M_WIDTH),
         seg(o_om, M_WIDTH), seg(o_qa, A_WIDTH), seg(o_ka, A_WIDTH), seg(o_va, A_WIDTH)], axis=1)
    w_if = jnp.pad(seg(o_if, 2 * M_HEADS), ((0, 0), (0, LANES_V7X - 2 * M_HEADS)))

    proj, if_rows = _inproj(x2d, norm_mix_w[l].reshape(1, D), w_main, w_if)
    p3 = proj.reshape(B, S, N_MAIN)

    nc = S // CHUNK
    gcol3 = if_rows.reshape(B, S, LANES_V7X)
    grow4 = if_rows[:, :2 * M_HEADS].reshape(B, nc, CHUNK, 2 * M_HEADS).transpose(0, 1, 3, 2)
    bias_col = jnp.pad(if_bias[l], (0, LANES_V7X - 2 * M_HEADS)).reshape(1, LANES_V7X)
    bias_row = jnp.broadcast_to(if_bias[l][:, None], (2 * M_HEADS, CHUNK))
    hm = _mlstm(p3, gcol3, grow4, conv_w[l], conv_b[l].reshape(1, 2 * M_WIDTH), bias_col, bias_row,
                m_head_norm_w[l].reshape(1, M_WIDTH))

    ya = _band_attn(p3, _band_bias(rel_bias[l]))

    x1 = _merge(hm.reshape(T, M_WIDTH), ya.reshape(T, A_WIDTH), proj, x2d,
                w_branch_m[l].astype(BF16), w_branch_a[l].astype(BF16), w_out[l].astype(BF16))

    km, vm = _memkv(mem, norm_mem_w[l].reshape(1, D), wkv_x[l].astype(BF16))
    wr = jnp.pad(w_router[l], ((0, 0), (0, LANES_V7X - N_EXPERTS)))
    wr_hi = wr.astype(BF16)
    wr_lo = (wr - wr_hi.astype(F32)).astype(BF16)
    br = jnp.concatenate([b_router[l], jnp.full((LANES_V7X - N_EXPERTS,), NEG_BIG, F32)]).reshape(1, LANES_V7X)
    x2, h3p, te, tg = _xattn(x1, S, norm_x_w[l].reshape(1, D), wq_x[l].astype(BF16), km, vm,
                             wo_x[l].astype(BF16), norm_ffn_w[l].reshape(1, D), wr_hi, wr_lo, br)

    A = T * TOP_K
    n_blocks = A // MOE_BLK + N_EXPERTS
    n_vis = (n_blocks + N_EXPERTS * (MOE_G - 1)) // MOE_G
    dest, visit_e, visit_first, visit_nb, n_used, n_blocks_used, partial = _route(te[:, :TOP_K], n_blocks, n_vis)
    dest3 = (dest * TOK_SUB).reshape(T // min(MOE_TT, T), 1, -1)
    xs = _dispatch(partial, dest3, h3p)
    yrows = _experts(visit_e, visit_first, visit_nb, n_used, n_blocks_used, xs, w_gu[l],
                     b_gu[l].reshape(N_EXPERTS, 2 * D_FF // MOE_TF, MOE_TF),
                     w_down[l], b_down[l].reshape(N_EXPERTS, 1, D))
    out = _combine(dest3, tg, x2, yrows, final_norm_w.reshape(1, D))
    return out.reshape(B, S, D)
```

```python
import functools

import jax
import jax.numpy as jnp
from jax import lax
from jax.experimental import pallas as pl
from jax.experimental.pallas import tpu as pltpu

F32 = jnp.float32
BF16 = jnp.bfloat16
I32 = jnp.int32
U32 = jnp.uint32

D_MODEL = 2048
CHUNK = 64
EPS = 1e-6
M_HEADS = 4
M_HEAD_DIM = 256
M_WIDTH = M_HEADS * M_HEAD_DIM
CONV_WIDTH = 4
A_HEADS = 16
A_HEAD_DIM = 64
A_WIDTH = A_HEADS * A_HEAD_DIM
LEFT_CHUNKS = 8
MAX_REL = 128
X_HEADS = 4
X_HEAD_DIM = 128
X_WIDTH = X_HEADS * X_HEAD_DIM
N_EXPERTS = 32
TOP_K = 4
D_FF = D_MODEL
SWIGLU_LIMIT = 7.0
SWIGLU_ALPHA = 1.702

LANES_V7X = 128
SUBLANES_V7X = 8
VMEM_BYTES_V7X = 64 * 1024 * 1024
VMEM_LIMIT_CAP = 60000 * 1024
DMA_PRIORITIES = 2

NEG_BIG = -1e30

COL_GM = 0
COL_GA = 2048
COL_QM = 4096
COL_KM = 5120
COL_VM = 6144
COL_OM = 7168
COL_QA = 8192
COL_KA = 9216
COL_VA = 10240
N_MAIN = 11264

IN_TM = 1024
IN_TN = 1024
ATT_TQ = 1024
ATT_SUB = 128
ATT_WIN = ATT_SUB + LEFT_CHUNKS * CHUNK
MERGE_TM = 512
XATT_TM = 512
MOE_BLK = 256
MOE_G = 9
MOE_R = MOE_G * MOE_BLK
MOE_BIG = 8
MOE_MID = 4
MOE_RING = 4
MOE_TF = 256
MOE_TT = 256


def _vmem_limit(nbytes):
    return int(min(VMEM_LIMIT_CAP, max(nbytes, 16 * 1024 * 1024)))


def _nt_dot(a, b):
    return lax.dot_general(a, b, (((1,), (1,)), ((), ())), preferred_element_type=F32)


def _tn_dot(a, b):
    return lax.dot_general(a, b, (((0,), (0,)), ((), ())), preferred_element_type=F32)


def _sigmoid(x):
    return 1.0 / (1.0 + jnp.exp(-x))


def _log_sigmoid(x):
    return jnp.minimum(x, 0.0) - jnp.log1p(jnp.exp(-jnp.abs(x)))


def _rms(x, w):
    return x * lax.rsqrt(jnp.mean(x * x, axis=-1, keepdims=True) + EPS) * w


def _pack_bf16_pairs(xb):
    n = xb.shape[1] // 2
    hi = lax.bitcast_convert_type(xb[:, :n].astype(F32), U32)
    lo = lax.bitcast_convert_type(xb[:, n:].astype(F32), U32)
    return hi | (lo >> 16)


def _unpack_bf16_pairs(w):
    hi = lax.bitcast_convert_type(w & jnp.uint32(0xFFFF0000), F32)
    lo = lax.bitcast_convert_type(w << 16, F32)
    return hi, lo


TOK_SUB = SUBLANES_V7X
TOK_WORDS = TOK_SUB * LANES_V7X


def _store_token_tiles(ref, tok0, packed):
    n = packed.shape[0]
    for s in range(TOK_SUB):
        ref[pl.ds(tok0 * TOK_SUB + s, n, stride=TOK_SUB), :] = packed[:, s * LANES_V7X:(s + 1) * LANES_V7X]


def _load_token_slab(ref, tok0, n, s):
    return ref[pl.ds(tok0 * TOK_SUB + s, n, stride=TOK_SUB), :]


def _inproj_kernel(x_ref, nw_ref, w_ref, wif_ref, o_ref, if_ref, h_scr):
    @pl.when(pl.program_id(1) == 0)
    def _():
        hb = _rms(x_ref[...], nw_ref[...]).astype(BF16)
        h_scr[...] = hb
        if_ref[...] = jnp.dot(hb, wif_ref[...], preferred_element_type=F32)

    o_ref[...] = jnp.dot(h_scr[...], w_ref[...], preferred_element_type=F32).astype(o_ref.dtype)


def _inproj(x2d, norm_w, w_main, w_if):
    T, D = x2d.shape
    tm = min(IN_TM, T)
    grid = (T // tm, N_MAIN // IN_TN)
    vmem = 2 * tm * D * 4 + tm * D * 2 + 2 * D * IN_TN * 2 + 2 * tm * IN_TN * 2 + 4 * tm * LANES_V7X * 4
    return pl.pallas_call(
        _inproj_kernel,
        out_shape=(jax.ShapeDtypeStruct((T, N_MAIN), BF16), jax.ShapeDtypeStruct((T, LANES_V7X), F32)),
        grid=grid,
        in_specs=[
            pl.BlockSpec((tm, D), lambda i, j: (i, 0)),
            pl.BlockSpec((1, D), lambda i, j: (0, 0)),
            pl.BlockSpec((D, IN_TN), lambda i, j: (0, j)),
            pl.BlockSpec((D, LANES_V7X), lambda i, j: (0, 0)),
        ],
        out_specs=(
            pl.BlockSpec((tm, IN_TN), lambda i, j: (i, j)),
            pl.BlockSpec((tm, LANES_V7X), lambda i, j: (i, 0)),
        ),
        scratch_shapes=[pltpu.VMEM((tm, D), BF16)],
        compiler_params=pltpu.CompilerParams(
            dimension_semantics=("parallel", "arbitrary"),
            vmem_limit_bytes=_vmem_limit(vmem + (8 << 20))),
        name="inproj",
    )(x2d, norm_w, w_main, w_if)


def _mlstm_kernel(q_ref, k_ref, v_ref, o_ref, gc_ref, gr_ref, cw_ref, cb_ref, bc_ref, br_ref, nw_ref,
                  out_ref, tail, c_st, n_st, m_st):
    nb = q_ref.shape[0]
    L = CHUNK
    dh = M_HEAD_DIM
    halo = tail.shape[1]

    @pl.when(pl.program_id(0) == 0)
    def _():
        tail[...] = jnp.zeros_like(tail)
        c_st[...] = jnp.zeros_like(c_st)
        n_st[...] = jnp.zeros_like(n_st)
        m_st[...] = jnp.zeros_like(m_st)

    row = lax.broadcasted_iota(I32, (L, L), 0)
    col = lax.broadcasted_iota(I32, (L, L), 1)
    lower = col <= row

    srow = lax.broadcasted_iota(I32, (L, halo + L), 0)
    scol = lax.broadcasted_iota(I32, (L, halo + L), 1)
    shift_mat = jnp.concatenate(
        [jnp.where(scol == srow + (halo - (CONV_WIDTH - 1) + j), 1.0, 0.0) for j in range(CONV_WIDTH - 1)],
        axis=0).astype(BF16)

    erow = lax.broadcasted_iota(I32, (dh, dh), 0)
    ecol = lax.broadcasted_iota(I32, (dh, dh), 1)
    eye = jnp.where(erow == ecol, 1.0, 0.0).astype(BF16)

    streams = [(b, h) for b in range(nb) for h in range(M_HEADS)]

    qk = []
    for b in range(nb):
        halves = []
        for half, ref in enumerate((q_ref, k_ref)):
            cols = slice(half * M_WIDTH, (half + 1) * M_WIDTH)
            cur = ref[b]
            ext = jnp.concatenate([tail[b, :, cols], cur], axis=0)
            back = jnp.dot(shift_mat, ext, preferred_element_type=F32)
            conv = cb_ref[:, cols] + cur.astype(F32) * cw_ref[CONV_WIDTH - 1:CONV_WIDTH, cols]
            for j in range(CONV_WIDTH - 1):
                conv = conv + back[j * L:(j + 1) * L] * cw_ref[j:j + 1, cols]
            halves.append(conv * _sigmoid(conv))
            tail[b, :, cols] = cur[L - halo:L]
        qk.append(halves)

    st = []
    for b, h in streams:
        s = b * M_HEADS + h
        lo, hi = h * dh, (h + 1) * dh
        gcol = gc_ref[b] + bc_ref[...]
        grow = gr_ref[b, 0] + br_ref[...]
        q = qk[b][0][:, lo:hi]
        k = qk[b][1][:, lo:hi] * (dh ** -0.5)
        ig_c = gcol[:, h:h + 1]
        ig_r = grow[h:h + 1, :]
        ls_c = _log_sigmoid(gcol[:, M_HEADS + h:M_HEADS + h + 1])
        ls_r = _log_sigmoid(grow[M_HEADS + h:M_HEADS + h + 1, :])
        b_c = jnp.sum(jnp.where(lower, ls_r, 0.0), axis=1, keepdims=True)
        b_r = jnp.sum(jnp.where(row <= col, ls_c, 0.0), axis=0, keepdims=True)
        g = jnp.sum(ls_r, axis=1, keepdims=True)
        m_old = m_st[s][:, 0:1]
        w_end = g - b_c + ig_c
        a = jnp.max(w_end, axis=0, keepdims=True)
        dlog = jnp.where(lower, b_c - b_r + ig_r, NEG_BIG)
        inter = b_c + m_old
        m_j = jnp.maximum(inter, jnp.max(dlog, axis=1, keepdims=True))
        st.append(dict(s=s, lo=lo, hi=hi, b=b, q=q, k=k, qb=q.astype(BF16), kb=k.astype(BF16),
                       vb=v_ref[b, :, lo:hi], g=g, a=a, m_old=m_old, m_j=m_j,
                       e_end=jnp.exp(w_end - a), decay=jnp.exp(dlog - m_j), w_int=jnp.exp(inter - m_j),
                       c_old=c_st[s], n_old=n_st[s]))

    for d in st:
        d["qc"] = _nt_dot(d["qb"], d["c_old"].astype(BF16))
    for d in st:
        d["sm"] = _nt_dot(d["qb"], d["kb"]) * d["decay"]
    for d in st:
        d["num"] = (jnp.dot(d["sm"].astype(BF16), d["vb"], preferred_element_type=F32)
                    + d["w_int"] * d["qc"])
    for d in st:
        den = (jnp.sum(d["sm"], axis=1, keepdims=True)
               + d["w_int"] * jnp.sum(d["q"] * d["n_old"], axis=1, keepdims=True))
        hh = d["num"] / jnp.maximum(jnp.abs(den), jnp.exp(-d["m_j"]))
        hn = _rms(hh, nw_ref[:, d["lo"]:d["hi"]])
        og = _sigmoid(o_ref[d["b"], :, d["lo"]:d["hi"]].astype(F32))
        out_ref[d["b"], :, d["lo"]:d["hi"]] = (og * hn).astype(out_ref.dtype)
    for d in st:
        s = d["s"]
        m_new = jnp.maximum(d["g"] + d["m_old"], d["a"])
        s_prev = jnp.exp(d["g"] + d["m_old"] - m_new)
        s_loc = jnp.exp(d["a"] - m_new)
        ve = (d["vb"].astype(F32) * (d["e_end"] * s_loc)).astype(BF16)
        ve_t = _nt_dot(eye, ve).astype(BF16)
        c_st[s] = s_prev * d["c_old"] + jnp.dot(ve_t, d["kb"], preferred_element_type=F32)
        n_st[s] = s_prev * d["n_old"] + s_loc * jnp.sum(d["e_end"] * d["k"], axis=0, keepdims=True)
        m_st[s] = jnp.broadcast_to(m_new, (1, LANES_V7X))


def _mlstm(p3, gcol3, grow4, conv_w, conv_b, bias_col, bias_row, head_w):
    B, S, _ = p3.shape
    nc = S // CHUNK
    ns = B * M_HEADS
    blk = lambda cb: pl.BlockSpec((B, CHUNK, M_WIDTH), lambda c, cb=cb: (0, c, cb))
    full = lambda shape: pl.BlockSpec(shape, lambda c: (0,) * len(shape))
    vmem = (2 * 5 * B * CHUNK * M_WIDTH * 2 + B * (CHUNK + 8) * 2 * M_WIDTH * 4
            + ns * M_HEAD_DIM * M_HEAD_DIM * 4 + (16 << 20))
    return pl.pallas_call(
        _mlstm_kernel,
        out_shape=jax.ShapeDtypeStruct((B, S, M_WIDTH), BF16),
        grid=(nc,),
        in_specs=[
            blk(COL_QM // M_WIDTH), blk(COL_KM // M_WIDTH), blk(COL_VM // M_WIDTH), blk(COL_OM // M_WIDTH),
            pl.BlockSpec((B, CHUNK, LANES_V7X), lambda c: (0, c, 0)),
            pl.BlockSpec((B, 1, 2 * M_HEADS, CHUNK), lambda c: (0, c, 0, 0)),
            full((CONV_WIDTH, 2 * M_WIDTH)), full((1, 2 * M_WIDTH)),
            full((1, LANES_V7X)), full((2 * M_HEADS, CHUNK)), full((1, M_WIDTH)),
        ],
        out_specs=pl.BlockSpec((B, CHUNK, M_WIDTH), lambda c: (0, c, 0)),
        scratch_shapes=[
            pltpu.VMEM((B, 2 * SUBLANES_V7X, 2 * M_WIDTH), BF16),
            pltpu.VMEM((ns, M_HEAD_DIM, M_HEAD_DIM), F32),
            pltpu.VMEM((ns, 1, M_HEAD_DIM), F32),
            pltpu.VMEM((ns, 1, LANES_V7X), F32),
        ],
        compiler_params=pltpu.CompilerParams(
            dimension_semantics=("arbitrary",), vmem_limit_bytes=_vmem_limit(vmem)),
        name="mlstm",
    )(p3, p3, p3, p3, gcol3, grow4, conv_w, conv_b, bias_col, bias_row, head_w)


def _band_attn_kernel(q_ref, kp_ref, kc_ref, vp_ref, vc_ref, bias_ref, out_ref):
    qb_idx = pl.program_id(2)
    tq = q_ref.shape[1]
    q2 = q_ref[0]
    kcat = jnp.concatenate([kp_ref[0], kc_ref[0]], axis=0)
    vcat = jnp.concatenate([vp_ref[0], vc_ref[0]], axis=0)
    lane = lax.broadcasted_iota(I32, (1, LANES_V7X), 1)
    wcol = lax.broadcasted_iota(I32, (1, ATT_WIN), 1)
    first_pen = jnp.where(qb_idx == 0, NEG_BIG, 0.0)
    scale = A_HEAD_DIM ** -0.5
    bias = bias_ref[0]

    subs = []
    for t in range(tq // ATT_SUB):
        r0 = t * ATT_SUB
        w0 = tq + r0 - LEFT_CHUNKS * CHUNK
        pen = jnp.where(wcol < (tq - w0), first_pen, 0.0)
        qs = q2[r0:r0 + ATT_SUB] * scale
        zero = jnp.zeros_like(qs)
        qst = jnp.concatenate([jnp.where(lane < A_HEAD_DIM, qs, zero),
                               jnp.where(lane >= A_HEAD_DIM, qs, zero)], axis=0)
        subs.append(dict(r0=r0, w0=w0, pen=pen, qst=qst))
    for d in subs:
        d["sc"] = _nt_dot(d["qst"], kcat[d["w0"]:d["w0"] + ATT_WIN]) + bias + d["pen"]
    for d in subs:
        p = jnp.exp(d["sc"] - jnp.max(d["sc"], axis=1, keepdims=True))
        d["den"] = jnp.sum(p, axis=1, keepdims=True)
        d["p"] = p.astype(BF16)
    for d in subs:
        d["o"] = jnp.dot(d["p"], vcat[d["w0"]:d["w0"] + ATT_WIN], preferred_element_type=F32) / d["den"]
    for d in subs:
        o, r0 = d["o"], d["r0"]
        out_ref[0, r0:r0 + ATT_SUB, :] = jnp.where(
            lane < A_HEAD_DIM, o[:ATT_SUB], o[ATT_SUB:]).astype(out_ref.dtype)


def _band_attn(p3, bias):
    B, S, _ = p3.shape
    tq = ATT_TQ
    nq = S // tq
    nhp = A_HEADS // 2
    cq, ck, cv = COL_QA // LANES_V7X, COL_KA // LANES_V7X, COL_VA // LANES_V7X
    cur = lambda c0: pl.BlockSpec((1, tq, LANES_V7X), lambda b, hp, i, c0=c0: (b, i, c0 + hp))
    prev = lambda c0: pl.BlockSpec((1, tq, LANES_V7X), lambda b, hp, i, c0=c0: (b, jnp.maximum(i - 1, 0), c0 + hp))
    return pl.pallas_call(
        _band_attn_kernel,
        out_shape=jax.ShapeDtypeStruct((B, S, A_WIDTH), BF16),
        grid=(B, nhp, nq),
        in_specs=[cur(cq), prev(ck), cur(ck), prev(cv), cur(cv),
                  pl.BlockSpec((1, 2 * ATT_SUB, ATT_WIN), lambda b, hp, i: (hp, 0, 0))],
        out_specs=pl.BlockSpec((1, tq, LANES_V7X), lambda b, hp, i: (b, i, hp)),
        compiler_params=pltpu.CompilerParams(
            dimension_semantics=("parallel", "parallel", "arbitrary"),
            vmem_limit_bytes=_vmem_limit(32 << 20)),
        name="band_attn",
    )(p3, p3, p3, p3, p3, bias)


def _band_bias(rel_table):
    n = 6 * LANES_V7X
    c = jnp.arange(n)
    d = jnp.where(c < ATT_WIN, c, c - n)
    rel = jnp.clip(LEFT_CHUNKS * CHUNK - d, -MAX_REL, MAX_REL) + MAX_REL
    vec = rel_table[:, rel].astype(F32)
    sheared = jnp.tile(vec, (1, ATT_SUB))[:, :ATT_SUB * (n - 1)].reshape(A_HEADS, ATT_SUB, n - 1)
    i = jnp.arange(ATT_SUB)[:, None]
    kp = jnp.arange(ATT_WIN)[None, :]
    ci, kc = i // CHUNK, kp // CHUNK
    valid = (kc >= ci) & (kc <= ci + LEFT_CHUNKS)
    bias = jnp.where(valid[None], sheared[:, :, :ATT_WIN], NEG_BIG)
    return bias.reshape(A_HEADS // 2, 2 * ATT_SUB, ATT_WIN)


def _merge_kernel(hm_ref, ya_ref, gm_ref, ga_ref, x_ref, wbm_ref, wba_ref, wo_ref, out_ref):
    ym = jnp.dot(hm_ref[...], wbm_ref[...], preferred_element_type=F32)
    ya = jnp.dot(ya_ref[...], wba_ref[...], preferred_element_type=F32)
    merged = _sigmoid(gm_ref[...].astype(F32)) * ym + _sigmoid(ga_ref[...].astype(F32)) * ya
    out_ref[...] = x_ref[...] + jnp.dot(merged.astype(BF16), wo_ref[...], preferred_element_type=F32)


def _const_spec(shape):
    return pl.BlockSpec(shape, lambda i: (0,) * len(shape), pipeline_mode=pl.Buffered(1))


def _merge(hm2d, ya2d, p2d, x2d, wbm, wba, wout):
    T, D = x2d.shape
    tm = min(MERGE_TM, T)
    vmem = ((2 * M_WIDTH * D + D * D) * 2 + 2 * (2 * tm * M_WIDTH * 2 + 2 * tm * D * 2 + 2 * tm * D * 4)
            + 6 * tm * D * 4)
    return pl.pallas_call(
        _merge_kernel,
        out_shape=jax.ShapeDtypeStruct((T, D), F32),
        grid=(T // tm,),
        in_specs=[
            pl.BlockSpec((tm, M_WIDTH), lambda i: (i, 0)),
            pl.BlockSpec((tm, A_WIDTH), lambda i: (i, 0)),
            pl.BlockSpec((tm, D), lambda i: (i, COL_GM // D)),
            pl.BlockSpec((tm, D), lambda i: (i, COL_GA // D)),
            pl.BlockSpec((tm, D), lambda i: (i, 0)),
            _const_spec((M_WIDTH, D)), _const_spec((A_WIDTH, D)), _const_spec((D, D)),
        ],
        out_specs=pl.BlockSpec((tm, D), lambda i: (i, 0)),
        compiler_params=pltpu.CompilerParams(
            dimension_semantics=("parallel",), vmem_limit_bytes=_vmem_limit(vmem)),
        name="merge_out",
    )(hm2d, ya2d, p2d, p2d, x2d, wbm, wba, wout)


def _memkv_kernel(mem_ref, nw_ref, w_ref, k_ref, v_ref):
    hb = _rms(mem_ref[0], nw_ref[...]).astype(BF16)
    kv = jnp.dot(hb, w_ref[...], preferred_element_type=F32)
    k_ref[0] = kv[:, :X_WIDTH].astype(k_ref.dtype)
    v_ref[0] = kv[:, X_WIDTH:].astype(v_ref.dtype)


def _memkv(mem, norm_w, wkv):
    B, N, D = mem.shape
    return pl.pallas_call(
        _memkv_kernel,
        out_shape=(jax.ShapeDtypeStruct((B, N, X_WIDTH), BF16), jax.ShapeDtypeStruct((B, N, X_WIDTH), BF16)),
        grid=(B,),
        in_specs=[pl.BlockSpec((1, N, D), lambda b: (b, 0, 0)),
                  pl.BlockSpec((1, D), lambda b: (0, 0)),
                  pl.BlockSpec((D, 2 * X_WIDTH), lambda b: (0, 0))],
        out_specs=(pl.BlockSpec((1, N, X_WIDTH), lambda b: (b, 0, 0)),
                   pl.BlockSpec((1, N, X_WIDTH), lambda b: (b, 0, 0))),
        compiler_params=pltpu.CompilerParams(
            dimension_semantics=("parallel",), vmem_limit_bytes=_vmem_limit(24 << 20)),
        name="mem_kv",
    )(mem, norm_w, wkv)


def _xattn_kernel(x_ref, nxw_ref, wq_ref, km_ref, vm_ref, wo_ref, nfw_ref, wrh_ref, wrl_ref, br_ref,
                  x2_ref, h3_ref, te_ref, tg_ref):
    x1 = x_ref[...]
    hb = _rms(x1, nxw_ref[...]).astype(BF16)
    q = jnp.dot(hb, wq_ref[...], preferred_element_type=F32).astype(BF16)
    km = km_ref[0]
    vm = vm_ref[0]
    outs = []
    for h in range(X_HEADS):
        lo, hi = h * X_HEAD_DIM, (h + 1) * X_HEAD_DIM
        sc = _nt_dot(q[:, lo:hi], km[:, lo:hi]) * (X_HEAD_DIM ** -0.5)
        mx = jnp.max(sc, axis=1, keepdims=True)
        p = jnp.exp(sc - mx)
        den = jnp.sum(p, axis=1, keepdims=True)
        outs.append(jnp.dot(p.astype(BF16), vm[:, lo:hi], preferred_element_type=F32) / den)
    o = jnp.concatenate(outs, axis=1).astype(BF16)
    x2 = x1 + jnp.dot(o, wo_ref[...], preferred_element_type=F32)
    x2_ref[...] = x2

    h3 = _rms(x2, nfw_ref[...])
    h_hi = h3.astype(BF16)
    _store_token_tiles(h3_ref, 0, _pack_bf16_pairs(h_hi))
    h_lo = (h3 - h_hi.astype(F32)).astype(BF16)
    logits = (jnp.dot(h_hi, wrh_ref[...], preferred_element_type=F32)
              + jnp.dot(h_lo, wrh_ref[...], preferred_element_type=F32)
              + jnp.dot(h_hi, wrl_ref[...], preferred_element_type=F32)
              + br_ref[...])

    colf = lax.broadcasted_iota(I32, logits.shape, 1).astype(F32)
    work = logits
    vals, idxs = [], []
    for _ in range(TOP_K):
        mx = jnp.max(work, axis=1, keepdims=True)
        ix = jnp.min(jnp.where(work == mx, colf, float(LANES_V7X)), axis=1, keepdims=True)
        vals.append(mx)
        idxs.append(ix)
        work = jnp.where(colf == ix, -jnp.inf, work)
    es = [jnp.exp(v - vals[0]) for v in vals]
    tot = es[0] + es[1] + es[2] + es[3]
    te = jnp.zeros_like(logits)
    tg = jnp.zeros_like(logits)
    for kk in range(TOP_K):
        te = jnp.where(colf == float(kk), idxs[kk], te)
        tg = jnp.where(colf == float(kk), es[kk] / tot, tg)
    te_ref[...] = te.astype(I32)
    tg_ref[...] = tg


def _xattn(x1, batch_rows, norm_x_w, wq, km, vm, wo, norm_ffn_w, wr_hi, wr_lo, br):
    T, D = x1.shape
    tm = min(XATT_TM, batch_rows)
    per_b = batch_rows // tm
    n_mem = km.shape[1]
    row = lambda w: pl.BlockSpec((tm, w), lambda i: (i, 0))
    vmem = (2 * D * X_WIDTH * 2 + 2 * D * LANES_V7X * 2 + 4 * n_mem * X_WIDTH * 2
            + 2 * 3 * tm * D * 4 + 8 * tm * D * 4)
    return pl.pallas_call(
        _xattn_kernel,
        out_shape=(jax.ShapeDtypeStruct((T, D), F32), jax.ShapeDtypeStruct((T * TOK_SUB, LANES_V7X), U32),
                   jax.ShapeDtypeStruct((T, LANES_V7X), I32), jax.ShapeDtypeStruct((T, LANES_V7X), F32)),
        grid=(T // tm,),
        in_specs=[
            row(D), _const_spec((1, D)), _const_spec((D, X_WIDTH)),
            pl.BlockSpec((1, n_mem, X_WIDTH), lambda i: (i // per_b, 0, 0)),
            pl.BlockSpec((1, n_mem, X_WIDTH), lambda i: (i // per_b, 0, 0)),
            _const_spec((X_WIDTH, D)), _const_spec((1, D)),
            _const_spec((D, LANES_V7X)), _const_spec((D, LANES_V7X)), _const_spec((1, LANES_V7X)),
        ],
        out_specs=(row(D), pl.BlockSpec((tm * TOK_SUB, LANES_V7X), lambda i: (i, 0)),
                   row(LANES_V7X), row(LANES_V7X)),
        compiler_params=pltpu.CompilerParams(
            dimension_semantics=("parallel",), vmem_limit_bytes=_vmem_limit(vmem)),
        name="xattn_router",
    )(x1, norm_x_w, wq, km, vm, wo, norm_ffn_w, wr_hi, wr_lo, br)


def _dispatch_kernel(partial_ref, dest_ref, h_ref, xs_ref, zero_scr, sem, zsem):
    tt = h_ref.shape[0] // TOK_SUB
    nblocks = partial_ref.shape[0]

    @pl.when(pl.program_id(0) == 0)
    def _():
        zero_scr[...] = jnp.zeros_like(zero_scr)

        def zero_copy(b):
            row0 = pl.multiple_of(b * (MOE_BLK * TOK_SUB), MOE_BLK * TOK_SUB)
            return pltpu.make_async_copy(zero_scr, xs_ref.at[pl.ds(row0, MOE_BLK * TOK_SUB)], zsem)

        def zstart(b, carry):
            @pl.when(partial_ref[b] > 0)
            def _():
                zero_copy(b).start()
            return carry

        def zwait(b, carry):
            @pl.when(partial_ref[b] > 0)
            def _():
                zero_copy(b).wait()
            return carry

        lax.fori_loop(0, nblocks, zstart, 0)
        lax.fori_loop(0, nblocks, zwait, 0)

    def tile_copy(t, kk):
        src = h_ref.at[pl.ds(pl.multiple_of(t * TOK_SUB, TOK_SUB), TOK_SUB)]
        dst = xs_ref.at[pl.ds(pl.multiple_of(dest_ref[0, t * TOP_K + kk], TOK_SUB), TOK_SUB)]
        return pltpu.make_async_copy(src, dst, sem)

    def start(t, carry):
        for kk in range(TOP_K):
            tile_copy(t, kk).start(priority=kk % DMA_PRIORITIES)
        return carry

    def wait(t, carry):
        for kk in range(TOP_K):
            tile_copy(t, kk).wait()
        return carry

    lax.fori_loop(0, tt, start, 0)
    lax.fori_loop(0, tt, wait, 0)


def _dispatch(partial, dest3, h3p):
    T = h3p.shape[0] // TOK_SUB
    n_rows = partial.shape[0] * MOE_BLK
    tt = min(MOE_TT, T)
    grid_spec = pltpu.PrefetchScalarGridSpec(
        num_scalar_prefetch=1,
        grid=(T // tt,),
        in_specs=[
            pl.BlockSpec((None, 1, tt * TOP_K), lambda i, nb: (i, 0, 0), memory_space=pltpu.SMEM),
            pl.BlockSpec((tt * TOK_SUB, LANES_V7X), lambda i, nb: (i, 0)),
        ],
        out_specs=pl.BlockSpec(memory_space=pl.ANY),
        scratch_shapes=[pltpu.VMEM((MOE_BLK * TOK_SUB, LANES_V7X), U32),
                        pltpu.SemaphoreType.DMA(()), pltpu.SemaphoreType.DMA(())],
    )
    return pl.pallas_call(
        _dispatch_kernel,
        out_shape=jax.ShapeDtypeStruct((n_rows * TOK_SUB, LANES_V7X), U32),
        grid_spec=grid_spec,
        compiler_params=pltpu.CompilerParams(
            dimension_semantics=("arbitrary",), has_side_effects=True,
            vmem_limit_bytes=_vmem_limit(4 * tt * TOK_WORDS * 4 + MOE_BLK * TOK_WORDS * 4 + (8 << 20))),
        name="moe_dispatch",
    )(partial, dest3, h3p)


def _experts_kernel(ve_ref, vf_ref, nb_ref, nu_ref, tb_ref, bgu_ref, bd_ref, xs_hbm, wgu_hbm, wdn_hbm, out_hbm,
                    xb_scr, acc_scr, x_buf, o_buf, wg_buf, wu_buf, wd_buf, wgb_scr, wub_scr, wdb_scr,
                    xsems, osems, sems):
    v = pl.program_id(0)
    nb = nb_ref[v]
    first = vf_ref[v]
    n_used = nu_ref[0]
    half = TOK_WORDS
    tf = wg_buf.shape[2]
    nj = D_FF // tf
    blk_tiles = MOE_BLK * TOK_SUB
    n_blocks = out_hbm.shape[0] // blk_tiles
    ring = x_buf.shape[0]

    def block_rows(b):
        return pl.ds(pl.multiple_of(b * MOE_BLK, MOE_BLK), MOE_BLK)

    def hbm_block(ref, blk):
        return ref.at[pl.ds(pl.multiple_of(blk * blk_tiles, blk_tiles), blk_tiles)]

    def x_copy(blk, slot):
        return pltpu.make_async_copy(hbm_block(xs_hbm, blk), x_buf.at[slot], xsems.at[slot])

    def o_copy(blk, slot):
        return pltpu.make_async_copy(o_buf.at[slot], hbm_block(out_hbm, blk), osems.at[slot])

    def tile_copies(e, j, slot):
        cols = pl.ds(pl.multiple_of(j * tf, tf), tf)
        ucols = pl.ds(pl.multiple_of(D_FF + j * tf, tf), tf)
        return (pltpu.make_async_copy(wgu_hbm.at[e, :, cols], wg_buf.at[slot], sems.at[slot, 0]),
                pltpu.make_async_copy(wgu_hbm.at[e, :, ucols], wu_buf.at[slot], sems.at[slot, 1]),
                pltpu.make_async_copy(wdn_hbm.at[e, cols, :], wd_buf.at[slot], sems.at[slot, 2]))

    def fetch(e, j, slot):
        for c in tile_copies(e, j, slot):
            c.start()

    @pl.when(nb > 0)
    def _():
        e = ve_ref[v]

        def prefetch_rows(first_blk, n_blk):
            for b in range(ring):
                @pl.when(b < n_blk)
                def _():
                    x_copy(first_blk + b, b).start()

        @pl.when(v == 0)
        def _():
            fetch(e, 0, 0)
            prefetch_rows(first, nb)

        def stage(b, carry):
            slot = lax.rem(b, ring)
            x_copy(first + b, slot).wait()
            rows = block_rows(b)
            for s in range(TOK_SUB):
                hi, lo = _unpack_bf16_pairs(_load_token_slab(x_buf.at[slot], 0, MOE_BLK, s))
                xb_scr[rows, s * LANES_V7X:(s + 1) * LANES_V7X] = hi.astype(BF16)
                xb_scr[rows, half + s * LANES_V7X:half + (s + 1) * LANES_V7X] = lo.astype(BF16)
            acc_scr[rows, :] = jnp.zeros((MOE_BLK, acc_scr.shape[1]), F32)

            @pl.when(b + ring < nb)
            def _():
                x_copy(first + b + ring, slot).start()
            return carry
        lax.fori_loop(0, nb, stage, 0)

        big = nb >= MOE_BIG
        mid = jnp.logical_and(nb >= MOE_MID, jnp.logical_not(big))

        def tile_step(j, carry):
            slot = lax.rem(j, 2)

            @pl.when(j + 1 < nj)
            def _():
                fetch(e, j + 1, 1 - slot)

            @pl.when(jnp.logical_and(j + 1 == nj, v + 1 < n_used))
            def _():
                fetch(ve_ref[v + 1], 0, 1 - slot)
                prefetch_rows(vf_ref[v + 1], nb_ref[v + 1])

            for c in tile_copies(e, j, slot):
                c.wait()

            def ffn_rows(rows, wg, wu, wd):
                xb = xb_scr[rows, :]
                g = jnp.dot(xb, wg, preferred_element_type=F32) + bgu_ref[pl.ds(j, 1), :]
                u = jnp.dot(xb, wu, preferred_element_type=F32) + bgu_ref[pl.ds(nj + j, 1), :]
                g = jnp.minimum(g, SWIGLU_LIMIT)
                u = jnp.clip(u, -SWIGLU_LIMIT, SWIGLU_LIMIT)
                act = (u + 1.0) * (g * _sigmoid(SWIGLU_ALPHA * g))
                acc_scr[rows, :] += jnp.dot(act.astype(BF16), wd, preferred_element_type=F32)

            def cast_weights():
                wg = wg_buf[slot].astype(BF16)
                wu = wu_buf[slot].astype(BF16)
                wd = wd_buf[slot].astype(BF16)
                wgb_scr[...] = wg
                wub_scr[...] = wu
                wdb_scr[...] = wd
                return wg, wu, wd

            @pl.when(big)
            def _():
                ffn_rows(pl.ds(0, MOE_BIG * MOE_BLK), *cast_weights())

            @pl.when(mid)
            def _():
                ffn_rows(pl.ds(0, MOE_MID * MOE_BLK), *cast_weights())

            @pl.when(nb < MOE_MID)
            def _():
                cast_weights()

            def block(b, c2):
                ffn_rows(block_rows(b), wgb_scr[...], wub_scr[...], wdb_scr[...])
                return c2
            lax.fori_loop(jnp.where(big, MOE_BIG, jnp.where(mid, MOE_MID, 0)), nb, block, 0)
            return carry
        lax.fori_loop(0, nj, tile_step, 0)

        def finish(b, carry):
            slot = lax.rem(b, ring)

            @pl.when(b >= ring)
            def _():
                o_copy(first + b - ring, slot).wait()

            y = (acc_scr[block_rows(b), :] + bd_ref[...]).astype(BF16)
            _store_token_tiles(o_buf.at[slot], 0, _pack_bf16_pairs(y))
            o_copy(first + b, slot).start()
            return carry
        lax.fori_loop(0, nb, finish, 0)

        for back in range(ring, 0, -1):
            @pl.when(nb >= back)
            def _():
                o_copy(first + nb - back, lax.rem(nb - back, ring)).wait()

    @pl.when(v == pl.num_programs(0) - 1)
    def _():
        o_buf[0] = jnp.zeros(o_buf.shape[1:], U32)

        def fill(b, carry):
            c = o_copy(b, 0)
            c.start()
            c.wait()
            return carry
        lax.fori_loop(tb_ref[0], n_blocks, fill, 0)


def _experts(visit_e, visit_first, visit_nb, n_used, n_blocks_used, xs, wgu, bgu, wdn, bdn):
    half = TOK_WORDS
    D = 2 * half
    tf = MOE_TF
    nj = D_FF // tf
    nv = visit_e.shape[0]
    blk_tiles = MOE_BLK * TOK_SUB
    assert nj % 2 == 0

    grid_spec = pltpu.PrefetchScalarGridSpec(
        num_scalar_prefetch=5,
        grid=(nv,),
        in_specs=[
            pl.BlockSpec((None, 2 * nj, tf), lambda v, ve, vf, nb, nu, tb: (ve[v], 0, 0)),
            pl.BlockSpec((None, 1, D), lambda v, ve, vf, nb, nu, tb: (ve[v], 0, 0)),
            pl.BlockSpec(memory_space=pl.ANY),
            pl.BlockSpec(memory_space=pl.ANY),
            pl.BlockSpec(memory_space=pl.ANY),
        ],
        out_specs=pl.BlockSpec(memory_space=pl.ANY),
        scratch_shapes=[pltpu.VMEM((MOE_R, D), BF16), pltpu.VMEM((MOE_R, D), F32),
                        pltpu.VMEM((MOE_RING, blk_tiles, LANES_V7X), U32),
                        pltpu.VMEM((MOE_RING, blk_tiles, LANES_V7X), U32),
                        pltpu.VMEM((2, D, tf), F32), pltpu.VMEM((2, D, tf), F32), pltpu.VMEM((2, tf, D), F32),
                        pltpu.VMEM((D, tf), BF16), pltpu.VMEM((D, tf), BF16), pltpu.VMEM((tf, D), BF16),
                        pltpu.SemaphoreType.DMA((MOE_RING,)), pltpu.SemaphoreType.DMA((MOE_RING,)),
                        pltpu.SemaphoreType.DMA((2, 3))],
    )
    vmem = (MOE_R * D * 2 + MOE_R * D * 4 + 2 * MOE_RING * blk_tiles * LANES_V7X * 4
            + 2 * 3 * D * tf * 4 + 3 * D * tf * 2 + 5 * MOE_BIG * MOE_BLK * tf * 4)
    return pl.pallas_call(
        _experts_kernel,
        out_shape=jax.ShapeDtypeStruct(xs.shape, U32),
        grid_spec=grid_spec,
        compiler_params=pltpu.CompilerParams(
            dimension_semantics=("arbitrary",), has_side_effects=True, vmem_limit_bytes=_vmem_limit(vmem)),
        name="moe_experts",
    )(visit_e, visit_first, visit_nb, n_used, n_blocks_used, bgu, bdn, xs, wgu, wdn)


def _combine_kernel(dcur_ref, dnext_ref, gate_ref, x_ref, y_ref, nw_ref, out_ref, buf, y_scr, sems):
    i = pl.program_id(0)
    n = pl.num_programs(0)
    tt = x_ref.shape[0]
    half = TOK_WORDS
    slot = lax.rem(i, 2)

    def row_copy(dref, s, t, kk):
        src = y_ref.at[pl.ds(pl.multiple_of(dref[0, t * TOP_K + kk], TOK_SUB), TOK_SUB)]
        dst = buf.at[s, kk, pl.ds(pl.multiple_of(t * TOK_SUB, TOK_SUB), TOK_SUB)]
        return pltpu.make_async_copy(src, dst, sems.at[s])

    def issue(dref, s):
        def body(t, carry):
            for kk in range(TOP_K):
                row_copy(dref, s, t, kk).start(priority=kk % DMA_PRIORITIES)
            return carry
        lax.fori_loop(0, tt, body, 0)

    @pl.when(i == 0)
    def _():
        issue(dcur_ref, 0)

    @pl.when(i + 1 < n)
    def _():
        issue(dnext_ref, 1 - slot)

    def wait(t, carry):
        for kk in range(TOP_K):
            row_copy(dcur_ref, slot, t, kk).wait()
        return carry
    lax.fori_loop(0, tt, wait, 0)

    gate = gate_ref[...]
    gks = [jnp.broadcast_to(gate[:, kk:kk + 1], (tt, LANES_V7X)) for kk in range(TOP_K)]
    for s in range(TOK_SUB):
        ca = pl.ds(s * LANES_V7X, LANES_V7X)
        cb = pl.ds(half + s * LANES_V7X, LANES_V7X)
        ya = x_ref[:, ca]
        yb = x_ref[:, cb]
        for kk in range(TOP_K):
            hi, lo = _unpack_bf16_pairs(_load_token_slab(buf.at[slot, kk], 0, tt, s))
            ya = ya + gks[kk] * hi
            yb = yb + gks[kk] * lo
        y_scr[:, ca] = ya
        y_scr[:, cb] = yb
    out_ref[...] = _rms(y_scr[...], nw_ref[...])


def _combine(dest3, gates, x2, yrows, final_w):
    T, D = x2.shape
    half = TOK_WORDS
    tt = min(MOE_TT, T)
    n = T // tt
    dspec = lambda f: pl.BlockSpec((None, 1, tt * TOP_K), lambda i: (f(i), 0, 0), memory_space=pltpu.SMEM)
    return pl.pallas_call(
        _combine_kernel,
        out_shape=jax.ShapeDtypeStruct((T, D), F32),
        grid=(n,),
        in_specs=[
            dspec(lambda i: i), dspec(lambda i: jnp.minimum(i + 1, n - 1)),
            pl.BlockSpec((tt, LANES_V7X), lambda i: (i, 0)),
            pl.BlockSpec((tt, D), lambda i: (i, 0)),
            pl.BlockSpec(memory_space=pl.ANY),
            pl.BlockSpec((1, D), lambda i: (0, 0)),
        ],
        out_specs=pl.BlockSpec((tt, D), lambda i: (i, 0)),
        scratch_shapes=[pltpu.VMEM((2, TOP_K, tt * TOK_SUB, LANES_V7X), U32), pltpu.VMEM((tt, D), F32),
                        pltpu.SemaphoreType.DMA((2,))],
        compiler_params=pltpu.CompilerParams(
            dimension_semantics=("arbitrary",),
            vmem_limit_bytes=_vmem_limit(2 * TOP_K * tt * half * 4 + 8 * tt * D * 4 + (8 << 20))),
        name="moe_combine",
    )(dest3, dest3, gates, x2, yrows, final_w)


def _route(top_e, n_blocks, n_vis):
    flat_e = top_e.reshape(-1)
    onehot = (flat_e[:, None] == jnp.arange(N_EXPERTS, dtype=I32)[None, :]).astype(I32)
    incl = jnp.cumsum(onehot, axis=0)
    counts = incl[-1]
    pos = jnp.sum((incl - onehot) * onehot, axis=1)
    nblk = (counts + MOE_BLK - 1) // MOE_BLK
    bend = jnp.cumsum(nblk)
    bstart = bend - nblk
    dest = jnp.sum(onehot * bstart[None, :], axis=1) * MOE_BLK + pos

    nvis = (nblk + MOE_G - 1) // MOE_G
    bpv = (nblk + jnp.maximum(nvis, 1) - 1) // jnp.maximum(nvis, 1)
    vend = jnp.cumsum(nvis)
    vstart = vend - nvis
    n_used = vend[-1]
    v = jnp.minimum(jnp.arange(n_vis, dtype=I32), n_used - 1)[:, None]
    sel = ((vstart[None, :] <= v) & (v < vend[None, :])).astype(I32)
    pick = lambda table: jnp.sum(sel * table[None, :], axis=1)
    visit_e = pick(jnp.arange(N_EXPERTS, dtype=I32))
    local = v[:, 0] - pick(vstart)
    visit_first = pick(bstart) + local * pick(bpv)
    visit_nb = jnp.clip(pick(nblk) - local * pick(bpv), 0, pick(bpv))
    visit_nb = jnp.where(jnp.arange(n_vis) < n_used, visit_nb, 0)

    blk = jnp.arange(n_blocks, dtype=I32)
    is_last = jnp.any((blk[:, None] == (bend - 1)[None, :]) & (nblk > 0)[None, :], axis=1)
    partial = (is_last | (blk >= bend[-1])).astype(I32)
    as_i32 = lambda a: a.astype(I32)
    return (as_i32(dest), as_i32(visit_e), as_i32(visit_first), as_i32(visit_nb),
            as_i32(n_used).reshape(1), as_i32(bend[-1]).reshape(1), partial)


def kernel(x, mem, norm_mix_w, w_in, conv_w, conv_b, if_bias, m_head_norm_w, w_branch_m, w_branch_a, rel_bias, w_out, norm_x_w, norm_mem_w, wq_x, wkv_x, wo_x, norm_ffn_w, w_router, b_router, w_gu, b_gu, w_down, b_down, final_norm_w):
    B, S, D = x.shape
    T = B * S
    l = 0
    x2d = x.reshape(T, D)

    wi = w_in[l]
    o_qm, o_km, o_vm, o_om = 0, M_WIDTH, 2 * M_WIDTH, 3 * M_WIDTH
    o_if = 4 * M_WIDTH
    o_qa = o_if + 2 * M_HEADS
    o_ka, o_va = o_qa + A_WIDTH, o_qa + 2 * A_WIDTH
    o_gm = o_qa + 3 * A_WIDTH
    o_ga = o_gm + D
    seg = lambda o, w: wi[:, o:o + w].astype(BF16)
    w_main = jnp.concatenate(
        [seg(o_gm, D), seg(o_ga, D), seg(o_qm, M_WIDTH), seg(o_km, M_WIDTH), seg(o_vm, M_WIDTH),
         seg(o_om, M_WIDTH), seg(o_qa, A_WIDTH), seg(o_ka, A_WIDTH), seg(o_va, A_WIDTH)], axis=1)
    w_if = jnp.pad(seg(o_if, 2 * M_HEADS), ((0, 0), (0, LANES_V7X - 2 * M_HEADS)))

    proj, if_rows = _inproj(x2d, norm_mix_w[l].reshape(1, D), w_main, w_if)
    p3 = proj.reshape(B, S, N_MAIN)

    nc = S // CHUNK
    gcol3 = if_rows.reshape(B, S, LANES_V7X)
    grow4 = if_rows[:, :2 * M_HEADS].reshape(B, nc, CHUNK, 2 * M_HEADS).transpose(0, 1, 3, 2)
    bias_col = jnp.pad(if_bias[l], (0, LANES_V7X - 2 * M_HEADS)).reshape(1, LANES_V7X)
    bias_row = jnp.broadcast_to(if_bias[l][:, None], (2 * M_HEADS, CHUNK))
    hm = _mlstm(p3, gcol3, grow4, conv_w[l], conv_b[l].reshape(1, 2 * M_WIDTH), bias_col, bias_row,
                m_head_norm_w[l].reshape(1, M_WIDTH))

    ya = _band_attn(p3, _band_bias(rel_bias[l]))

    x1 = _merge(hm.reshape(T, M_WIDTH), ya.reshape(T, A_WIDTH), proj, x2d,
                w_branch_m[l].astype(BF16), w_branch_a[l].astype(BF16), w_out[l].astype(BF16))

    km, vm = _memkv(mem, norm_mem_w[l].reshape(1, D), wkv_x[l].astype(BF16))
    wr = jnp.pad(w_router[l], ((0, 0), (0, LANES_V7X - N_EXPERTS)))
    wr_hi = wr.astype(BF16)
    wr_lo = (wr - wr_hi.astype(F32)).astype(BF16)
    br = jnp.concatenate([b_router[l], jnp.full((LANES_V7X - N_EXPERTS,), NEG_BIG, F32)]).reshape(1, LANES_V7X)
    x2, h3p, te, tg = _xattn(x1, S, norm_x_w[l].reshape(1, D), wq_x[l].astype(BF16), km, vm,
                             wo_x[l].astype(BF16), norm_ffn_w[l].reshape(1, D), wr_hi, wr_lo, br)

    A = T * TOP_K
    n_blocks = A // MOE_BLK + N_EXPERTS
    n_vis = (n_blocks + N_EXPERTS * (MOE_G - 1)) // MOE_G
    dest, visit_e, visit_first, visit_nb, n_used, n_blocks_used, partial = _route(te[:, :TOP_K], n_blocks, n_vis)
    dest3 = (dest * TOK_SUB).reshape(T // min(MOE_TT, T), 1, -1)
    xs = _dispatch(partial, dest3, h3p)
    yrows = _experts(visit_e, visit_first, visit_nb, n_used, n_blocks_used, xs, w_gu[l],
                     b_gu[l].reshape(N_EXPERTS, 2 * D_FF // MOE_TF, MOE_TF),
                     w_down[l], b_down[l].reshape(N_EXPERTS, 1, D))
    out = _combine(dest3, tg, x2, yrows, final_norm_w.reshape(1, D))
    return out.reshape(B, S, D)
```
